```python
import math
import jax
import jax.numpy as jnp
from jax import lax
import numpy as np

D_MODEL = 2048
BATCH = 2
SEQ = 8192
DEPTH = 4
DEC_BATCH = 4
DEC_SEQ = 2048
PAST_LEN = 128

HEAD_DIM = 64
GRID_W = 64
NA_HEADS = 8
NA_ROWS = 8
NA_COLS = 16
DIL_PATTERNS = ((128, 1), (512, 4), (2048, 16))
DIL_HEADS_PER_GROUP = 4
DIL_HEADS = 12
DIL_BLOCK = 64
GA_Q_HEADS = 8
GA_KV_HEADS = 2
GA_BLOCK = 128
ROPE_THETA = 10000.0
SW_Q_HEADS = 8
SW_KV_HEADS = 2
SW_HALF_WINDOW = 128
SW_BLOCK = 128

N_BRANCHES = 4
D_FF = 4 * D_MODEL
RMS_EPS = 1e-6
NEG_INF = -1e30

A_W = NA_HEADS * HEAD_DIM
B_W = DIL_HEADS * HEAD_DIM
B_OUT = DIL_HEADS_PER_GROUP * HEAD_DIM
C_QW = GA_Q_HEADS * HEAD_DIM
C_KVW = GA_KV_HEADS * HEAD_DIM
D_QW = SW_Q_HEADS * HEAD_DIM
D_KVW = SW_KV_HEADS * HEAD_DIM
GATE_W = N_BRANCHES * D_MODEL
SPLIT_SIZES = (A_W, A_W, A_W, B_W, B_W, B_W, C_QW, C_KVW, C_KVW, D_QW, D_KVW, D_KVW, GATE_W)
IN_WIDTH = 3 * A_W + 3 * B_W + C_QW + 2 * C_KVW + D_QW + 2 * D_KVW + GATE_W

kernel_name = "hybrid_natten_dilated_axial_sink_encoder"


def rmsnorm(x, g):
    xf = x.astype(jnp.float32)
    y = xf * lax.rsqrt(jnp.mean(xf * xf, axis=-1, keepdims=True) + RMS_EPS)
    return (y * g.astype(jnp.float32)).astype(x.dtype)


def alibi_slopes(n):
    return (2.0 ** (-8.0 * np.arange(1, n + 1) / n)).astype(np.float32)


def axial_rope(x):
    L = x.shape[1]
    t = jnp.arange(L)
    half = HEAD_DIM // 2

    def rot(xp, pos):
        npair = xp.shape[-1] // 2
        freqs = ROPE_THETA ** (-jnp.arange(npair, dtype=jnp.float32) / npair)
        ang = pos.astype(jnp.float32)[:, None] * freqs[None, :]
        cos = jnp.cos(ang)[None, :, None, :].astype(x.dtype)
        sin = jnp.sin(ang)[None, :, None, :].astype(x.dtype)
        x1, x2 = xp[..., :npair], xp[..., npair:]
        return jnp.concatenate([x1 * cos - x2 * sin, x2 * cos + x1 * sin], axis=-1)

    return jnp.concatenate([rot(x[..., :half], t // GRID_W), rot(x[..., half:], t % GRID_W)], axis=-1)


def neighborhood_attention(q, k, v, rel_bias):
    B, L, H, Dh = q.shape
    R = L // GRID_W
    KH = min(NA_ROWS, R)
    NCB = GRID_W // NA_COLS
    KWB = 2 * NA_COLS
    qc = np.arange(GRID_W).reshape(NCB, NA_COLS)
    kc = np.clip(np.arange(NCB) * NA_COLS - NA_COLS // 2, 0, GRID_W - KWB)[:, None] + np.arange(KWB)
    c0 = np.clip(qc - NA_COLS // 2, 0, GRID_W - NA_COLS)
    col_valid = (kc[:, None, :] >= c0[..., None]) & (kc[:, None, :] < c0[..., None] + NA_COLS)
    dc = np.clip(kc[:, None, :] - qc[..., None], -(NA_COLS - 1), NA_COLS - 1) + NA_COLS - 1
    col_bias = rel_bias[:, :, dc]
    kg = k.reshape(B, R, GRID_W, H, Dh)
    vg = v.reshape(B, R, GRID_W, H, Dh)
    qg = q.reshape(B, R, NCB, NA_COLS, H, Dh)
    scale = Dh ** -0.5

    def row_step(r):
        r0 = jnp.clip(r - KH // 2, 0, R - KH)
        k_blk = lax.dynamic_slice_in_dim(kg, r0, KH, axis=1)[:, :, kc]
        v_blk = lax.dynamic_slice_in_dim(vg, r0, KH, axis=1)[:, :, kc]
        q_row = lax.dynamic_index_in_dim(qg, r, axis=1, keepdims=False)
        s = jnp.einsum('bnqhd,bknjhd->bhnqkj', q_row, k_blk).astype(jnp.float32) * scale
        dr = r0 + jnp.arange(KH) - r + NA_ROWS - 1
        bias = jnp.transpose(col_bias[:, dr], (0, 2, 3, 1, 4)).astype(jnp.float32)
        s = jnp.where(col_valid[None, None, :, :, None, :], s + bias[None], NEG_INF)
        p = jax.nn.softmax(s.reshape(B, H, NCB, NA_COLS, KH * KWB), axis=-1).reshape(s.shape)
        o = jnp.einsum('bhnqkj,bknjhd->bnqhd', p.astype(v.dtype), v_blk)
        return o.reshape(B, GRID_W, H, Dh)

    out = lax.map(row_step, jnp.arange(R))
    return jnp.moveaxis(out, 0, 1).reshape(B, L, H * Dh)


def banded_attention(q, k, v, half_window, block, slopes, dist_scale, sink):
    N, n, Hq, Dh = q.shape
    Hkv = k.shape[2]
    G = Hq // Hkv
    nb = -(-n // block)
    npad = nb * block
    qp = jnp.pad(q, ((0, 0), (0, npad - n), (0, 0), (0, 0))).reshape(N, nb, block, Hkv, G, Dh)

    def windows(t):
        tp = jnp.pad(t, ((0, 0), (block, npad - n + block), (0, 0), (0, 0))).reshape(N, nb + 2, block, Hkv, Dh)
        return jnp.concatenate([tp[:, :-2], tp[:, 1:-1], tp[:, 2:]], axis=2)

    kw = windows(k)
    vw = windows(v)
    s = jnp.einsum('nbqkgd,nbskd->nbkgqs', qp, kw).astype(jnp.float32) * (Dh ** -0.5)
    delta = np.arange(3 * block)[None, :] - block - np.arange(block)[:, None]
    kpos = (np.arange(nb)[:, None] - 1) * block + np.arange(3 * block)[None, :]
    valid = (np.abs(delta)[None] <= half_window) & (kpos[:, None, :] >= 0) & (kpos[:, None, :] < n)
    bias = -slopes.reshape(Hkv, G)[:, :, None, None] * (dist_scale * np.abs(delta)).astype(np.float32)[None, None]
    s = jnp.where(valid[None, :, None, None], s + bias, NEG_INF)
    m = jnp.max(s, axis=-1)
    if sink is not None:
        sk = sink.astype(jnp.float32).reshape(Hkv, G)[:, :, None]
        m = jnp.maximum(m, sk)
    p = jnp.exp(s - m[..., None])
    denom = jnp.sum(p, axis=-1)
    if sink is not None:
        denom = denom + jnp.exp(sk - m)
    o = jnp.einsum('nbkgqs,nbskd->nbqkgd', (p / denom[..., None]).astype(v.dtype), vw)
    lse = jnp.transpose(m + jnp.log(denom), (0, 1, 4, 2, 3))
    return (o.reshape(N, npad, Hq, Dh)[:, :n], lse.reshape(N, npad, Hq)[:, :n])


def dilated_attention(q, k, v):
    B, L, _, Dh = q.shape
    slopes = alibi_slopes(DIL_HEADS)
    hg = DIL_HEADS_PER_GROUP
    outs = []
    lses = []
    for g, (w, d) in enumerate(DIL_PATTERNS):
        n = L // d
        sl = slice(g * hg, (g + 1) * hg)

        def fold(t):
            return t.reshape(B, n, d, hg, Dh).swapaxes(1, 2).reshape(B * d, n, hg, Dh)

        o, lse = banded_attention(fold(q[:, :, sl]), fold(k[:, :, sl]), fold(v[:, :, sl]),
                                  w // (2 * d), DIL_BLOCK, slopes[sl], d, None)
        outs.append(o.reshape(B, d, n, hg, Dh).swapaxes(1, 2).reshape(B, L, hg, Dh))
        lses.append(lse.reshape(B, d, n, hg).swapaxes(1, 2).reshape(B, L, hg))
    wts = jax.nn.softmax(jnp.stack(lses, axis=0), axis=0)
    out = jnp.sum(wts[..., None] * jnp.stack(outs, axis=0).astype(jnp.float32), axis=0)
    return out.astype(q.dtype).reshape(B, L, hg * Dh)


def axial_global_attention(q, k, v, q_gain, k_gain):
    B, L, Hq, Dh = q.shape
    Hkv = k.shape[2]
    G = Hq // Hkv
    q = axial_rope(rmsnorm(q, q_gain))
    k = axial_rope(rmsnorm(k, k_gain))
    nqb = L // GA_BLOCK
    qb = jnp.moveaxis(q.reshape(B, nqb, GA_BLOCK, Hkv, G, Dh), 1, 0)
    scale = Dh ** -0.5

    def step(qblk):
        s = jnp.einsum('bqkgd,bskd->bkgqs', qblk, k).astype(jnp.float32) * scale
        p = jax.nn.softmax(s, axis=-1)
        return jnp.einsum('bkgqs,bskd->bqkgd', p.astype(v.dtype), v)

    o = lax.map(step, qb)
    return jnp.moveaxis(o, 0, 1).reshape(B, L, Hq * Dh)


def mixer_sublayer(x, norm_g, w_in, rel_bias_a, q_gain_c, k_gain_c, sink_d, w_a, w_b, w_c, w_d, w_out):
    B, L, _ = x.shape
    h = rmsnorm(x, norm_g)
    parts = jnp.split(h @ w_in, np.cumsum(SPLIT_SIZES)[:-1].tolist(), axis=-1)
    aq, ak, av, bq, bk, bv, cq, ck, cv, dq, dk, dv, gl = parts

    def hd(t):
        return t.reshape(B, L, -1, HEAD_DIM)

    ya = neighborhood_attention(hd(aq), hd(ak), hd(av), rel_bias_a)
    yb = dilated_attention(hd(bq), hd(bk), hd(bv))
    yc = axial_global_attention(hd(cq), hd(ck), hd(cv), q_gain_c, k_gain_c)
    yd, _ = banded_attention(hd(dq), hd(dk), hd(dv), SW_HALF_WINDOW, SW_BLOCK,
                             alibi_slopes(SW_Q_HEADS), 1, sink_d)
    yd = yd.reshape(B, L, D_QW)
    g = jax.nn.sigmoid(gl).reshape(B, L, N_BRANCHES, D_MODEL)
    merged = (g[:, :, 0] * (ya @ w_a) + g[:, :, 1] * (yb @ w_b)
              + g[:, :, 2] * (yc @ w_c) + g[:, :, 3] * (yd @ w_d))
    return x + merged @ w_out


def mlp_sublayer(x, norm_g, w_up, w_down):
    h = rmsnorm(x, norm_g)
    return x + jnp.square(jax.nn.relu(h @ w_up)) @ w_down


def trunk(x, norm_mix, w_in, rel_bias_a, q_gain_c, k_gain_c, sink_d, w_branch_a, w_branch_b,
          w_branch_c, w_branch_d, w_out, norm_mlp, w_up, w_down, norm_final):
    for l in range(DEPTH):
        x = mixer_sublayer(x, norm_mix[l], w_in[l], rel_bias_a[l], q_gain_c[l], k_gain_c[l], sink_d[l],
                           w_branch_a[l], w_branch_b[l], w_branch_c[l], w_branch_d[l], w_out[l])
        x = mlp_sublayer(x, norm_mlp[l], w_up[l], w_down[l])
    return rmsnorm(x, norm_final)


def setup_inputs(seed: int = 0) -> dict:
    key = jax.random.key(seed)
    ks = jax.random.split(key, 18)

    def nrm(k, shape, scale):
        return jax.random.normal(k, shape, jnp.float32) * scale

    return {
        "x_prompt": nrm(ks[0], (BATCH, SEQ, D_MODEL), 1.0),
        "x_sample": nrm(ks[1], (DEC_BATCH, DEC_SEQ, D_MODEL), 1.0),
        "norm_mix": 1.0 + nrm(ks[2], (DEPTH, D_MODEL), 0.02),
        "w_in": nrm(ks[3], (DEPTH, D_MODEL, IN_WIDTH), D_MODEL ** -0.5),
        "rel_bias_a": nrm(ks[4], (DEPTH, NA_HEADS, 2 * NA_ROWS - 1, 2 * NA_COLS - 1), 0.1),
        "q_gain_c": 1.0 + nrm(ks[5], (DEPTH, HEAD_DIM), 0.02),
        "k_gain_c": 1.0 + nrm(ks[6], (DEPTH, HEAD_DIM), 0.02),
        "sink_d": nrm(ks[7], (DEPTH, SW_Q_HEADS), 0.5),
        "w_branch_a": nrm(ks[8], (DEPTH, A_W, D_MODEL), A_W ** -0.5),
        "w_branch_b": nrm(ks[9], (DEPTH, B_OUT, D_MODEL), B_OUT ** -0.5),
        "w_branch_c": nrm(ks[10], (DEPTH, C_QW, D_MODEL), C_QW ** -0.5),
        "w_branch_d": nrm(ks[11], (DEPTH, D_QW, D_MODEL), D_QW ** -0.5),
        "w_out": nrm(ks[12], (DEPTH, D_MODEL, D_MODEL), D_MODEL ** -0.5),
        "norm_mlp": 1.0 + nrm(ks[13], (DEPTH, D_MODEL), 0.02),
        "w_up": nrm(ks[14], (DEPTH, D_MODEL, D_FF), D_MODEL ** -0.5),
        "w_down": nrm(ks[15], (DEPTH, D_FF, D_MODEL), D_FF ** -0.5),
        "norm_final": 1.0 + nrm(ks[16], (D_MODEL,), 0.02),
    }


def reference(x_prompt, x_sample, norm_mix, w_in, rel_bias_a, q_gain_c, k_gain_c, sink_d, w_branch_a,
              w_branch_b, w_branch_c, w_branch_d, w_out, norm_mlp, w_up, w_down, norm_final):
    y_prompt = trunk(x_prompt, norm_mix, w_in, rel_bias_a, q_gain_c, k_gain_c, sink_d, w_branch_a, w_branch_b,
                     w_branch_c, w_branch_d, w_out, norm_mlp, w_up, w_down, norm_final)
    y_sample = trunk(x_sample, norm_mix, w_in, rel_bias_a, q_gain_c, k_gain_c, sink_d, w_branch_a, w_branch_b,
                     w_branch_c, w_branch_d, w_out, norm_mlp, w_up, w_down, norm_final)
    return (y_prompt, y_sample)
```

```python
import functools

import numpy as np
import jax
import jax.numpy as jnp
from jax import lax
from jax.experimental import pallas as pl
from jax.experimental.pallas import tpu as pltpu

F32 = jnp.float32
BF16 = jnp.bfloat16

D_MODEL = 2048
HEAD_DIM = 64
GRID_W = 64
NA_HEADS = 8
NA_ROWS = 8
NA_COLS = 16
DIL_PATTERNS = ((128, 1), (512, 4), (2048, 16))
DIL_HEADS_PER_GROUP = 4
DIL_HEADS = 12
DIL_BLOCK = 64
GA_Q_HEADS = 8
GA_KV_HEADS = 2
ROPE_THETA = 10000.0
SW_Q_HEADS = 8
SW_KV_HEADS = 2
SW_HALF_WINDOW = 128
N_BRANCHES = 4
D_FF = 4 * D_MODEL
RMS_EPS = 1e-6
NEG_INF = -1e30

A_W = NA_HEADS * HEAD_DIM
B_W = DIL_HEADS * HEAD_DIM
B_OUT = DIL_HEADS_PER_GROUP * HEAD_DIM
C_QW = GA_Q_HEADS * HEAD_DIM
C_KVW = GA_KV_HEADS * HEAD_DIM
D_QW = SW_Q_HEADS * HEAD_DIM
D_KVW = SW_KV_HEADS * HEAD_DIM
GATE_W = N_BRANCHES * D_MODEL
SPLIT_SIZES = (A_W, A_W, A_W, B_W, B_W, B_W, C_QW, C_KVW, C_KVW, D_QW, D_KVW, D_KVW, GATE_W)

LANES = 128
V7X_VMEM_LIMIT = 56 * 1024 * 1024

COL_AQ = 0
COL_AK = 512
COL_AV = 1024
COL_CQ = 1536
COL_DQ = 2048
COL_BQ = 2560
COL_BK = 3328
COL_BV = 4096
COL_CK = 4864
COL_CV = 5120
COL_DK = 5376
COL_DV = 5632
P_WIDTH = 6144


def _alibi_slopes(n):
    return (2.0 ** (-8.0 * np.arange(1, n + 1) / n)).astype(np.float32)


def _half_masks():
    lane = lax.broadcasted_iota(jnp.int32, (1, LANES), 1)
    lo = jnp.where(lane < HEAD_DIM, 1.0, 0.0).astype(BF16)
    hi = jnp.where(lane >= HEAD_DIM, 1.0, 0.0).astype(BF16)
    return lane, (lo, hi)


def _dot_nt(a, b):
    return lax.dot_general(a, b, (((1,), (1,)), ((), ())), preferred_element_type=F32)


def _rms_rows(x, gain):
    ms = jnp.mean(x * x, axis=-1, keepdims=True)
    return x * lax.rsqrt(ms + RMS_EPS) * gain


def _rmsnorm_kernel(x_ref, g_ref, o_ref):
    o_ref[...] = _rms_rows(x_ref[...], g_ref[...]).astype(o_ref.dtype)


def _rmsnorm(x, gain, tm=512):
    t, d = x.shape
    return pl.pallas_call(
        _rmsnorm_kernel,
        grid=(t // tm,),
        in_specs=[pl.BlockSpec((tm, d), lambda i: (i, 0)), pl.BlockSpec((1, d), lambda i: (0, 0))],
        out_specs=pl.BlockSpec((tm, d), lambda i: (i, 0)),
        out_shape=jax.ShapeDtypeStruct((t, d), BF16),
        compiler_params=pltpu.CompilerParams(dimension_semantics=("parallel",), vmem_limit_bytes=V7X_VMEM_LIMIT),
        name="rmsnorm",
    )(x, gain.reshape(1, d))


def _proj_kernel(h_ref, w_ref, o_ref, *, row_chunk, sigmoid):
    for r in range(h_ref.shape[0] // row_chunk):
        rows = pl.ds(r * row_chunk, row_chunk)
        acc = jnp.dot(h_ref[rows, :], w_ref[...], preferred_element_type=F32)
        if sigmoid:
            acc = 1.0 / (1.0 + jnp.exp(-acc))
        o_ref[rows, :] = acc.astype(o_ref.dtype)


def _proj(h, w, *, sigmoid, tm=1024, tn=2048, name):
    t, k = h.shape
    n = w.shape[1]
    return pl.pallas_call(
        functools.partial(_proj_kernel, row_chunk=256, sigmoid=sigmoid),
        grid=(t // tm, n // tn),
        in_specs=[pl.BlockSpec((tm, k), lambda i, j: (i, 0)), pl.BlockSpec((k, tn), lambda i, j: (0, j))],
        out_specs=pl.BlockSpec((tm, tn), lambda i, j: (i, j)),
        out_shape=jax.ShapeDtypeStruct((t, n), BF16),
        compiler_params=pltpu.CompilerParams(
            dimension_semantics=("parallel", "arbitrary"), vmem_limit_bytes=V7X_VMEM_LIMIT),
        name=name,
    )(h, w)


NA_STEP_ROWS = 8
NA_STEP_TOK = NA_STEP_ROWS * GRID_W
NA_KEYS = NA_ROWS * GRID_W


def _natten_bias_table(rel_bias):
    qc = np.arange(GRID_W)[:, None]
    kc = np.arange(GRID_W)[None, :]
    c0 = np.clip(qc - NA_COLS // 2, 0, GRID_W - NA_COLS)
    valid = (kc >= c0) & (kc < c0 + NA_COLS)
    dc = np.clip(kc - qc, -(NA_COLS - 1), NA_COLS - 1) + NA_COLS - 1
    shift = np.arange(NA_ROWS)[:, None]
    krow = np.arange(NA_ROWS)[None, :]
    dr = krow - shift + NA_ROWS - 1
    tbl = rel_bias[:, dr][:, :, :, dc]
    tbl = jnp.where(valid[None, None, None], tbl, NEG_INF)
    tbl = jnp.transpose(tbl, (1, 0, 3, 2, 4))
    return tbl.reshape(NA_ROWS, NA_HEADS, GRID_W, NA_KEYS).astype(F32)


def _natten_kernel(q_ref, kp_ref, kc_ref, kn_ref, vp_ref, vc_ref, vn_ref, tbl_ref, o_ref, kbuf, vbuf, *, n_rows):
    i = pl.program_id(1)
    for s, (kr, vr) in enumerate(((kp_ref, vp_ref), (kc_ref, vc_ref), (kn_ref, vn_ref))):
        kbuf[s * NA_STEP_TOK:(s + 1) * NA_STEP_TOK, :] = kr[...]
        vbuf[s * NA_STEP_TOK:(s + 1) * NA_STEP_TOK, :] = vr[...]
    lane, hm = _half_masks()

    def row_body(j, carry):
        r = i * NA_STEP_ROWS + j
        r0 = jnp.clip(r - NA_ROWS // 2, 0, n_rows - NA_ROWS)
        shift = r - r0
        off = pl.multiple_of((r0 - (i - 1) * NA_STEP_ROWS) * GRID_W, GRID_W)
        qrows = pl.ds(pl.multiple_of(j * GRID_W, GRID_W), GRID_W)
        for pi in range(NA_HEADS // 2):
            cols = slice(pi * LANES, (pi + 1) * LANES)
            qpair = q_ref[qrows, cols]
            kwin = kbuf[pl.ds(off, NA_KEYS), cols]
            vwin = vbuf[pl.ds(off, NA_KEYS), cols]
            outs = []
            for e in range(2):
                s = _dot_nt(qpair * hm[e], kwin) + tbl_ref[shift, 2 * pi + e]
                m = jnp.max(s, axis=-1, keepdims=True)
                p = jnp.exp(s - m)
                l = jnp.sum(p, axis=-1, keepdims=True)
                outs.append(jnp.dot(p.astype(BF16), vwin, preferred_element_type=F32) / l)
            o_ref[qrows, cols] = jnp.where(lane < HEAD_DIM, outs[0], outs[1]).astype(o_ref.dtype)
        return carry

    lax.fori_loop(0, NA_STEP_ROWS, row_body, 0)


def _natten(p, tbl, batch, seq):
    t = batch * seq
    nb = seq // NA_STEP_TOK
    tok = NA_STEP_TOK

    def spec(col, which):
        cb = col // A_W
        if which == 0:
            return pl.BlockSpec((tok, A_W), lambda b, i: (b * nb + i, cb))
        if which < 0:
            return pl.BlockSpec((tok, A_W), lambda b, i: (b * nb + jnp.maximum(i - 1, 0), cb))
        return pl.BlockSpec((tok, A_W), lambda b, i: (b * nb + jnp.minimum(i + 1, nb - 1), cb))

    return pl.pallas_call(
        functools.partial(_natten_kernel, n_rows=seq // GRID_W),
        grid=(batch, nb),
        in_specs=[spec(COL_AQ, 0),
                  spec(COL_AK, -1), spec(COL_AK, 0), spec(COL_AK, 1),
                  spec(COL_AV, -1), spec(COL_AV, 0), spec(COL_AV, 1),
                  pl.BlockSpec(tbl.shape, lambda b, i: (0, 0, 0, 0))],
        out_specs=pl.BlockSpec((tok, A_W), lambda b, i: (b * nb + i, 0)),
        out_shape=jax.ShapeDtypeStruct((t, A_W), BF16),
        scratch_shapes=[pltpu.VMEM((3 * tok, A_W), BF16), pltpu.VMEM((3 * tok, A_W), BF16)],
        compiler_params=pltpu.CompilerParams(
            dimension_semantics=("parallel", "arbitrary"), vmem_limit_bytes=V7X_VMEM_LIMIT),
        name="natten",
    )(p, p, p, p, p, p, p, tbl)


def _band_table(slopes, dist_scale, w):
    delta = np.arange(3 * w)[None, :] - w - np.arange(w)[:, None]
    bias = -slopes[:, None, None] * (dist_scale * np.abs(delta)).astype(np.float32)[None]
    return np.where(np.abs(delta)[None] <= w, bias, NEG_INF).astype(np.float32)


def _banded_kernel(*refs, w, tq, n, kvmap, has_sink, has_lse):
    refs = list(refs)
    sink_ref = refs.pop(0) if has_sink else None
    q_ref, kp_ref, kc_ref, kn_ref, vp_ref, vc_ref, vn_ref, tbl_ref, o_ref = refs[:9]
    lse_ref = refs[9] if has_lse else None
    i = pl.program_id(2)
    kbuf = jnp.concatenate([kp_ref[...], kc_ref[...], kn_ref[...]], axis=0)
    vbuf = jnp.concatenate([vp_ref[...], vc_ref[...], vn_ref[...]], axis=0)
    lane, hm = _half_masks()
    for j in range(tq // w):
        rows = slice(j * w, (j + 1) * w)
        kpos = i * tq + (j - 1) * w + lax.broadcasted_iota(jnp.int32, (1, 3 * w), 1)
        range_bias = jnp.where(kpos < 0, NEG_INF, jnp.where(kpos >= n, NEG_INF, 0.0))
        for pi, kv in enumerate(kvmap):
            cols = slice(pi * LANES, (pi + 1) * LANES)
            kvcols = slice(kv * LANES, (kv + 1) * LANES)
            qpair = q_ref[rows, cols]
            kwin = kbuf[j * w:(j + 3) * w, kvcols]
            vwin = vbuf[j * w:(j + 3) * w, kvcols]
            outs, lses = [], []
            for e in range(2):
                h = 2 * pi + e
                s = _dot_nt(qpair * hm[e], kwin) + tbl_ref[h] + range_bias
                m = jnp.max(s, axis=-1, keepdims=True)
                if has_sink:
                    m = jnp.maximum(m, sink_ref[h])
                p = jnp.exp(s - m)
                l = jnp.sum(p, axis=-1, keepdims=True)
                if has_sink:
                    l = l + jnp.exp(sink_ref[h] - m)
                outs.append(jnp.dot(p.astype(BF16), vwin, preferred_element_type=F32) / l)
                lses.append(m + jnp.log(l))
            o_ref[rows, cols] = jnp.where(lane < HEAD_DIM, outs[0], outs[1]).astype(o_ref.dtype)
            if has_lse:
                lse_ref[rows, cols] = jnp.where(lane < HEAD_DIM, lses[0], lses[1])


def _banded(p, *, batch, seq, dil, w, col_q, col_k, col_v, wq, wkv, kvmap, table, sink, out_dtype, has_lse):
    n = seq // dil
    tq = min(512, n)
    sub = tq // w
    nblk = n // w
    pw = p.shape[-1]
    pv = p.reshape(batch, n, dil * pw)

    def qspec(col, width):
        return pl.BlockSpec((None, tq, width), lambda b, r, i: (b, i, r * (pw // width) + col // width))

    def prev_spec(col, width):
        return pl.BlockSpec((None, w, width),
                            lambda b, r, i: (b, jnp.maximum(i * sub - 1, 0), r * (pw // width) + col // width))

    def next_spec(col, width):
        return pl.BlockSpec((None, w, width),
                            lambda b, r, i: (b, jnp.minimum((i + 1) * sub, nblk - 1), r * (pw // width) + col // width))

    in_specs = [qspec(col_q, wq),
                prev_spec(col_k, wkv), qspec(col_k, wkv), next_spec(col_k, wkv),
                prev_spec(col_v, wkv), qspec(col_v, wkv), next_spec(col_v, wkv),
                pl.BlockSpec(table.shape, lambda b, r, i: (0, 0, 0))]
    args = [pv] * 7 + [jnp.asarray(table)]
    if sink is not None:
        in_specs = [pl.BlockSpec(memory_space=pltpu.SMEM)] + in_specs
        args = [sink.astype(F32)] + args
    out_block = pl.BlockSpec((None, tq, wq), lambda b, r, i: (b, i, r))
    out_shape = [jax.ShapeDtypeStruct((batch, n, dil * wq), out_dtype)]
    out_specs = [out_block]
    if has_lse:
        out_shape.append(jax.ShapeDtypeStruct((batch, n, dil * wq), F32))
        out_specs.append(out_block)
    res = pl.pallas_call(
        functools.partial(_banded_kernel, w=w, tq=tq, n=n, kvmap=kvmap, has_sink=sink is not None, has_lse=has_lse),
        grid=(batch, dil, n // tq),
        in_specs=in_specs,
        out_specs=out_specs,
        out_shape=out_shape,
        compiler_params=pltpu.CompilerParams(
            dimension_semantics=("parallel", "parallel", "arbitrary"), vmem_limit_bytes=V7X_VMEM_LIMIT),
        name="banded_d%d" % dil,
    )(*args)
    return [o.reshape(batch * seq, wq) for o in res]


def _rope_tables(seq):
    t = jnp.arange(seq)
    npair = HEAD_DIM // 4
    freqs = ROPE_THETA ** (-jnp.arange(npair, dtype=F32) / npair)

    def cs(pos):
        ang = pos.astype(F32)[:, None] * freqs[None, :]
        return jnp.cos(ang), jnp.sin(ang)

    cr, sr = cs(t // GRID_W)
    cc, sc = cs(t % GRID_W)
    cos = jnp.concatenate([cr, cr, cc, cc], axis=-1)
    sin = jnp.concatenate([-sr, sr, -sc, sc], axis=-1)
    return jnp.tile(cos, (1, 2)), jnp.tile(sin, (1, 2))


def _cprep_kernel(q_ref, k_ref, cos_ref, sin_ref, qg_ref, kg_ref, avg_ref, qo_ref, ko_ref):
    lane = lax.broadcasted_iota(jnp.int32, (1, LANES), 1)
    first = (lane & (HEAD_DIM // 4)) == 0
    cos = cos_ref[...]
    sin = sin_ref[...]
    avg = avg_ref[...]

    def norm_rope(x, gain):
        x = x.astype(F32)
        sq = x * x
        hi = sq.astype(BF16)
        r1 = sq - hi.astype(F32)
        mid = r1.astype(BF16)
        lo = (r1 - mid.astype(F32)).astype(BF16)
        ms = (jnp.dot(hi, avg, preferred_element_type=F32) + jnp.dot(mid, avg, preferred_element_type=F32)
              + jnp.dot(lo, avg, preferred_element_type=F32))
        y = x * lax.rsqrt(ms + RMS_EPS) * gain
        quarter = HEAD_DIM // 4
        partner = jnp.where(first, pltpu.roll(y, LANES - quarter, 1), pltpu.roll(y, quarter, 1))
        return y * cos + partner * sin

    for c in range(q_ref.shape[1] // LANES):
        cols = slice(c * LANES, (c + 1) * LANES)
        qo_ref[:, cols] = norm_rope(q_ref[:, cols], qg_ref[...]).astype(qo_ref.dtype)
    ko_ref[...] = norm_rope(k_ref[...], kg_ref[...]).astype(ko_ref.dtype)


def _cprep(p, cos, sin, q_gain, k_gain, batch, seq, tm=512):
    t = batch * seq
    nt = seq // tm
    avg = np.zeros((LANES, LANES), np.float32)
    avg[:HEAD_DIM, :HEAD_DIM] = 1.0 / HEAD_DIM
    avg[HEAD_DIM:, HEAD_DIM:] = 1.0 / HEAD_DIM
    gw = C_QW // GA_KV_HEADS
    qg = jnp.tile(q_gain.astype(F32) * (HEAD_DIM ** -0.5), 2).reshape(1, LANES)
    kg = jnp.tile(k_gain.astype(F32), 2).reshape(1, LANES)
    return pl.pallas_call(
        _cprep_kernel,
        grid=(t // tm, GA_KV_HEADS),
        in_specs=[pl.BlockSpec((tm, gw), lambda i, g: (i, COL_CQ // gw + g)),
                  pl.BlockSpec((tm, LANES), lambda i, g: (i, COL_CK // LANES + g)),
                  pl.BlockSpec((tm, LANES), lambda i, g: (i % nt, 0)),
                  pl.BlockSpec((tm, LANES), lambda i, g: (i % nt, 0)),
                  pl.BlockSpec((1, LANES), lambda i, g: (0, 0)),
                  pl.BlockSpec((1, LANES), lambda i, g: (0, 0)),
                  pl.BlockSpec((LANES, LANES), lambda i, g: (0, 0))],
        out_specs=[pl.BlockSpec((tm, gw), lambda i, g: (i, g)),
                   pl.BlockSpec((tm, LANES), lambda i, g: (i, g))],
        out_shape=[jax.ShapeDtypeStruct((t, C_QW), BF16), jax.ShapeDtypeStruct((t, GA_KV_HEADS * LANES), BF16)],
        compiler_params=pltpu.CompilerParams(
            dimension_semantics=("parallel", "arbitrary"), vmem_limit_bytes=V7X_VMEM_LIMIT),
        name="cprep",
    )(p, p, cos, sin, qg, kg, jnp.asarray(avg, BF16))


def _flash_kernel(q_ref, k_ref, v_ref, o_ref, *, tk):
    tq = q_ref.shape[0]
    nkv = k_ref.shape[0] // tk
    lane, hm = _half_masks()
    for pi in range(q_ref.shape[1] // LANES):
        cols = slice(pi * LANES, (pi + 1) * LANES)
        qpair = q_ref[:, cols]
        outs = []
        for e in range(2):
            qm = qpair * hm[e]

            def body(t, carry, qm=qm):
                m, l, acc = carry
                rows = pl.ds(pl.multiple_of(t * tk, tk), tk)
                s = _dot_nt(qm, k_ref[rows, :])
                m_new = jnp.maximum(m, jnp.max(s, axis=-1, keepdims=True))
                alpha = jnp.exp(m - m_new)
                p = jnp.exp(s - m_new)
                l = alpha * l + jnp.sum(p, axis=-1, keepdims=True)
                acc = alpha * acc + jnp.dot(p.astype(BF16), v_ref[rows, :], preferred_element_type=F32)
                return m_new, l, acc

            init = (jnp.full((tq, 1), NEG_INF, F32), jnp.zeros((tq, 1), F32), jnp.zeros((tq, LANES), F32))
            _, l, acc = lax.fori_loop(0, nkv, body, init)
            outs.append(acc / l)
        o_ref[:, cols] = jnp.where(lane < HEAD_DIM, outs[0], outs[1]).astype(o_ref.dtype)


def _flash(q, k, p, batch, seq, tq=256, tk=512):
    t = batch * seq
    nq = seq // tq
    gw = C_QW // GA_KV_HEADS
    return pl.pallas_call(
        functools.partial(_flash_kernel, tk=tk),
        grid=(batch, GA_KV_HEADS, nq),
        in_specs=[pl.BlockSpec((tq, gw), lambda b, g, i: (b * nq + i, g)),
                  pl.BlockSpec((seq, LANES), lambda b, g, i: (b, g)),
                  pl.BlockSpec((seq, LANES), lambda b, g, i: (b, COL_CV // LANES + g))],
        out_specs=pl.BlockSpec((tq, gw), lambda b, g, i: (b * nq + i, g)),
        out_shape=jax.ShapeDtypeStruct((t, C_QW), BF16),
        compiler_params=pltpu.CompilerParams(
            dimension_semantics=("parallel", "parallel", "arbitrary"), vmem_limit_bytes=V7X_VMEM_LIMIT),
        name="flash",
    )(q, k, p)


def _merge_kernel(x_ref, ya_ref, ob0_ref, ob1_ref, ob2_ref, ls0_ref, ls1_ref, ls2_ref, yc_ref, yd_ref,
                  ga_ref, gb_ref, gc_ref, gd_ref, wa_ref, wb_ref, wc_ref, wd_ref, wo_ref, gn_ref,
                  xo_ref, ho_ref):
    l0, l1, l2 = ls0_ref[...], ls1_ref[...], ls2_ref[...]
    mx = jnp.maximum(jnp.maximum(l0, l1), l2)
    e0, e1, e2 = jnp.exp(l0 - mx), jnp.exp(l1 - mx), jnp.exp(l2 - mx)
    yb = (e0 * ob0_ref[...] + e1 * ob1_ref[...] + e2 * ob2_ref[...]) / (e0 + e1 + e2)

    def branch(g_ref, y, w_ref):
        return g_ref[...].astype(F32) * jnp.dot(y, w_ref[...], preferred_element_type=F32)

    merged = (branch(ga_ref, ya_ref[...], wa_ref) + branch(gb_ref, yb.astype(BF16), wb_ref)
              + branch(gc_ref, yc_ref[...], wc_ref) + branch(gd_ref, yd_ref[...], wd_ref))
    out = x_ref[...] + jnp.dot(merged.astype(BF16), wo_ref[...], preferred_element_type=F32)
    xo_ref[...] = out
    ho_ref[...] = _rms_rows(out, gn_ref[...]).astype(ho_ref.dtype)


def _merge(x, ya, obs, lss, yc, yd, gates, wa, wb, wc, wd, wo, gain, tm=256):
    t, d = x.shape

    def rows(width, col=0):
        return pl.BlockSpec((tm, width), lambda i: (i, col))

    def const(arr):
        return pl.BlockSpec(arr.shape, lambda i: (0, 0), pipeline_mode=pl.Buffered(1))

    gain = gain.reshape(1, d)
    return pl.pallas_call(
        _merge_kernel,
        grid=(t // tm,),
        in_specs=[rows(d), rows(A_W), rows(B_OUT), rows(B_OUT), rows(B_OUT), rows(B_OUT), rows(B_OUT), rows(B_OUT),
                  rows(C_QW), rows(D_QW), rows(d, 0), rows(d, 1), rows(d, 2), rows(d, 3),
                  const(wa), const(wb), const(wc), const(wd), const(wo), const(gain)],
        out_specs=[rows(d), rows(d)],
        out_shape=[jax.ShapeDtypeStruct((t, d), F32), jax.ShapeDtypeStruct((t, d), BF16)],
        compiler_params=pltpu.CompilerParams(dimension_semantics=("parallel",), vmem_limit_bytes=V7X_VMEM_LIMIT),
        name="merge",
    )(x, ya, *obs, *lss, yc, yd, gates, gates, gates, gates, wa, wb, wc, wd, wo, gain)


def _mlp_kernel(h_ref, x_ref, wu_ref, wd_ref, gn_ref, *rest, final):
    if final:
        yo_ref, acc_ref = rest
    else:
        xo_ref, ho_ref, acc_ref = rest
    f = pl.program_id(1)
    u = jnp.dot(h_ref[...], wu_ref[...], preferred_element_type=F32)
    a = jnp.square(jnp.maximum(u, 0.0)).astype(BF16)
    part = jnp.dot(a, wd_ref[...], preferred_element_type=F32)

    @pl.when(f == 0)
    def _():
        acc_ref[...] = x_ref[...] + part

    @pl.when(f > 0)
    def _():
        acc_ref[...] += part

    @pl.when(f == pl.num_programs(1) - 1)
    def _():
        out = acc_ref[...]
        normed = _rms_rows(out, gn_ref[...])
        if final:
            yo_ref[...] = normed
        else:
            xo_ref[...] = out
            ho_ref[...] = normed.astype(ho_ref.dtype)


def _mlp(h, x, wu, wd, gain, *, final, tm=512, tf=1024):
    t, d = x.shape
    ff = wu.shape[1]
    rows = pl.BlockSpec((tm, d), lambda i, f: (i, 0))
    if final:
        out_specs = [rows]
        out_shape = [jax.ShapeDtypeStruct((t, d), F32)]
    else:
        out_specs = [rows, rows]
        out_shape = [jax.ShapeDtypeStruct((t, d), F32), jax.ShapeDtypeStruct((t, d), BF16)]
    return pl.pallas_call(
        functools.partial(_mlp_kernel, final=final),
        grid=(t // tm, ff // tf),
        in_specs=[rows, rows,
                  pl.BlockSpec((d, tf), lambda i, f: (0, f)),
                  pl.BlockSpec((tf, d), lambda i, f: (f, 0)),
                  pl.BlockSpec((1, d), lambda i, f: (0, 0))],
        out_specs=out_specs,
        out_shape=out_shape,
        scratch_shapes=[pltpu.VMEM((tm, d), F32)],
        compiler_params=pltpu.CompilerParams(
            dimension_semantics=("parallel", "arbitrary"), vmem_limit_bytes=V7X_VMEM_LIMIT),
        name="mlp_final" if final else "mlp",
    )(h, x, wu, wd, gain.reshape(1, d))


def _prep_in_weights(w_in):
    offs = np.cumsum((0,) + SPLIT_SIZES)
    aq, ak, av, bq, bk, bv, cq, ck, cv, dq, dk, dv, gl = [w_in[..., offs[s]:offs[s + 1]] for s in range(13)]
    scale = HEAD_DIM ** -0.5

    def dup(t):
        k0, k1 = t[..., :HEAD_DIM], t[..., HEAD_DIM:]
        return jnp.concatenate([k0, k0, k1, k1], axis=-1)

    pad = jnp.zeros(w_in.shape[:-1] + (P_WIDTH - COL_DV - 2 * D_KVW,), w_in.dtype)
    w_mix = jnp.concatenate([aq * scale, ak, av, cq, dq * scale, bq * scale, bk, bv,
                             dup(ck), dup(cv), dup(dk), dup(dv), pad], axis=-1)
    return w_mix.astype(BF16), gl.astype(BF16)


def _trunk(x, wts, tables):
    batch, seq, d = x.shape
    t = batch * seq
    depth = wts["w_mix"].shape[0]
    cos, sin = _rope_tables(seq)
    x2 = x.reshape(t, d)
    h = _rmsnorm(x2, wts["norm_mix"][0])
    b_slopes = _alibi_slopes(DIL_HEADS)
    d_table = _band_table(_alibi_slopes(SW_Q_HEADS), 1, SW_HALF_WINDOW)
    for l in range(depth):
        p = _proj(h, wts["w_mix"][l], sigmoid=False, name="proj_mix")
        gates = _proj(h, wts["w_gate"][l], sigmoid=True, name="proj_gate")
        ya = _natten(p, tables["natten"][l], batch, seq)
        obs, lss = [], []
        hg = DIL_HEADS_PER_GROUP
        for g, (win, dil) in enumerate(DIL_PATTERNS):
            ob, ls = _banded(p, batch=batch, seq=seq, dil=dil, w=win // (2 * dil),
                             col_q=COL_BQ + g * B_OUT, col_k=COL_BK + g * B_OUT, col_v=COL_BV + g * B_OUT,
                             wq=B_OUT, wkv=B_OUT, kvmap=(0, 1),
                             table=_band_table(b_slopes[g * hg:(g + 1) * hg], dil, win // (2 * dil)),
                             sink=None, out_dtype=F32, has_lse=True)
            obs.append(ob)
            lss.append(ls)
        qr, kr = _cprep(p, cos, sin, wts["q_gain_c"][l], wts["k_gain_c"][l], batch, seq)
        yc = _flash(qr, kr, p, batch, seq)
        (yd,) = _banded(p, batch=batch, seq=seq, dil=1, w=SW_HALF_WINDOW,
                        col_q=COL_DQ, col_k=COL_DK, col_v=COL_DV, wq=D_QW, wkv=2 * D_KVW, kvmap=(0, 0, 1, 1),
                        table=d_table, sink=wts["sink_d"][l], out_dtype=BF16, has_lse=False)
        x2, h = _merge(x2, ya, obs, lss, yc, yd, gates, wts["w_a"][l], wts["w_b"][l], wts["w_c"][l],
                       wts["w_d"][l], wts["w_out"][l], wts["norm_mlp"][l])
        if l + 1 < depth:
            x2, h = _mlp(h, x2, wts["w_up"][l], wts["w_down"][l], wts["norm_mix"][l + 1], final=False)
        else:
            (y,) = _mlp(h, x2, wts["w_up"][l], wts["w_down"][l], wts["norm_final"], final=True)
    return y.reshape(batch, seq, d)


def kernel(x_prompt, x_sample, norm_mix, w_in, rel_bias_a, q_gain_c, k_gain_c, sink_d, w_branch_a, w_branch_b,
           w_branch_c, w_branch_d, w_out, norm_mlp, w_up, w_down, norm_final):
    w_mix, w_gate = _prep_in_weights(w_in)
    wts = dict(norm_mix=norm_mix, w_mix=w_mix, w_gate=w_gate, q_gain_c=q_gain_c, k_gain_c=k_gain_c, sink_d=sink_d,
               w_a=w_branch_a.astype(BF16), w_b=w_branch_b.astype(BF16), w_c=w_branch_c.astype(BF16),
               w_d=w_branch_d.astype(BF16), w_out=w_out.astype(BF16), norm_mlp=norm_mlp,
               w_up=w_up.astype(BF16), w_down=w_down.astype(BF16), norm_final=norm_final)
    tables = dict(natten=jax.vmap(_natten_bias_table)(rel_bias_a))
    return (_trunk(x_prompt, wts, tables), _trunk(x_sample, wts, tables))
```

```python
import functools

import numpy as np
import jax
import jax.numpy as jnp
from jax import lax
from jax.experimental import pallas as pl
from jax.experimental.pallas import tpu as pltpu

F32 = jnp.float32
BF16 = jnp.bfloat16

D_MODEL = 2048
HEAD_DIM = 64
GRID_W = 64
NA_HEADS = 8
NA_ROWS = 8
NA_COLS = 16
DIL_PATTERNS = ((128, 1), (512, 4), (2048, 16))
DIL_HEADS_PER_GROUP = 4
DIL_HEADS = 12
DIL_BLOCK = 64
GA_Q_HEADS = 8
GA_KV_HEADS = 2
ROPE_THETA = 10000.0
SW_Q_HEADS = 8
SW_KV_HEADS = 2
SW_HALF_WINDOW = 128
N_BRANCHES = 4
D_FF = 4 * D_MODEL
RMS_EPS = 1e-6
NEG_INF = -1e30

A_W = NA_HEADS * HEAD_DIM
B_W = DIL_HEADS * HEAD_DIM
B_OUT = DIL_HEADS_PER_GROUP * HEAD_DIM
C_QW = GA_Q_HEADS * HEAD_DIM
C_KVW = GA_KV_HEADS * HEAD_DIM
D_QW = SW_Q_HEADS * HEAD_DIM
D_KVW = SW_KV_HEADS * HEAD_DIM
GATE_W = N_BRANCHES * D_MODEL
SPLIT_SIZES = (A_W, A_W, A_W, B_W, B_W, B_W, C_QW, C_KVW, C_KVW, D_QW, D_KVW, D_KVW, GATE_W)

LANES = 128
V7X_VMEM_LIMIT = 56 * 1024 * 1024

COL_AQ = 0
COL_AK = 512
COL_AV = 1024
COL_CQ = 1536
COL_DQ = 2048
COL_B = 2560
COL_CK = 4864
COL_CV = 5120
COL_DK = 5376
COL_DV = 5632
P_WIDTH = 6144


def _alibi_slopes(n):
    return (2.0 ** (-8.0 * np.arange(1, n + 1) / n)).astype(np.float32)


def _half_masks():
    lane = lax.broadcasted_iota(jnp.int32, (1, LANES), 1)
    lo = jnp.where(lane < HEAD_DIM, 1.0, 0.0).astype(BF16)
    hi = jnp.where(lane >= HEAD_DIM, 1.0, 0.0).astype(BF16)
    return lane, (lo, hi)


def _dot_nt(a, b):
    return lax.dot_general(a, b, (((1,), (1,)), ((), ())), preferred_element_type=F32)


def _rms_rows(x, gain):
    ms = jnp.mean(x * x, axis=-1, keepdims=True)
    return x * lax.rsqrt(ms + RMS_EPS) * gain


def _rmsnorm_kernel(x_ref, g_ref, o_ref):
    o_ref[...] = _rms_rows(x_ref[...], g_ref[...]).astype(o_ref.dtype)


def _rmsnorm(x, gain, tm=512):
    t, d = x.shape
    return pl.pallas_call(
        _rmsnorm_kernel,
        grid=(t // tm,),
        in_specs=[pl.BlockSpec((tm, d), lambda i: (i, 0)), pl.BlockSpec((1, d), lambda i: (0, 0))],
        out_specs=pl.BlockSpec((tm, d), lambda i: (i, 0)),
        out_shape=jax.ShapeDtypeStruct((t, d), BF16),
        compiler_params=pltpu.CompilerParams(dimension_semantics=("parallel",), vmem_limit_bytes=V7X_VMEM_LIMIT),
        name="rmsnorm",
    )(x, gain.reshape(1, d))


def _proj_kernel(h_ref, w_ref, o_ref, *, row_chunk, sigmoid):
    for r in range(h_ref.shape[0] // row_chunk):
        rows = pl.ds(r * row_chunk, row_chunk)
        acc = jnp.dot(h_ref[rows, :], w_ref[...], preferred_element_type=F32)
        if sigmoid:
            acc = 1.0 / (1.0 + jnp.exp(-acc))
        o_ref[rows, :] = acc.astype(o_ref.dtype)


def _proj(h, w, *, sigmoid, tm=1024, tn=2048, name):
    t, k = h.shape
    n = w.shape[1]
    return pl.pallas_call(
        functools.partial(_proj_kernel, row_chunk=256, sigmoid=sigmoid),
        grid=(t // tm, n // tn),
        in_specs=[pl.BlockSpec((tm, k), lambda i, j: (i, 0)), pl.BlockSpec((k, tn), lambda i, j: (0, j))],
        out_specs=pl.BlockSpec((tm, tn), lambda i, j: (i, j)),
        out_shape=jax.ShapeDtypeStruct((t, n), BF16),
        compiler_params=pltpu.CompilerParams(
            dimension_semantics=("parallel", "arbitrary"), vmem_limit_bytes=V7X_VMEM_LIMIT),
        name=name,
    )(h, w)


NA_STEP_ROWS = 8
NA_STEP_TOK = NA_STEP_ROWS * GRID_W
NA_KEYS = NA_ROWS * GRID_W


def _natten_bias_table(rel_bias):
    qc = np.arange(GRID_W)[:, None]
    kc = np.arange(GRID_W)[None, :]
    c0 = np.clip(qc - NA_COLS // 2, 0, GRID_W - NA_COLS)
    valid = (kc >= c0) & (kc < c0 + NA_COLS)
    dc = np.clip(kc - qc, -(NA_COLS - 1), NA_COLS - 1) + NA_COLS - 1
    shift = np.arange(NA_ROWS)[:, None]
    krow = np.arange(NA_ROWS)[None, :]
    dr = krow - shift + NA_ROWS - 1
    tbl = rel_bias[:, dr][:, :, :, dc]
    tbl = jnp.where(valid[None, None, None], tbl, NEG_INF)
    tbl = jnp.transpose(tbl, (1, 0, 3, 2, 4))
    return tbl.reshape(NA_ROWS, NA_HEADS, GRID_W, NA_KEYS).astype(F32)


def _natten_kernel(q_ref, kp_ref, kc_ref, kn_ref, vp_ref, vc_ref, vn_ref, tbl_ref, o_ref, kbuf, vbuf, *, n_rows):
    i = pl.program_id(1)
    for s, (kr, vr) in enumerate(((kp_ref, vp_ref), (kc_ref, vc_ref), (kn_ref, vn_ref))):
        kbuf[s * NA_STEP_TOK:(s + 1) * NA_STEP_TOK, :] = kr[...]
        vbuf[s * NA_STEP_TOK:(s + 1) * NA_STEP_TOK, :] = vr[...]
    lane, hm = _half_masks()

    def row_body(j, carry):
        r = i * NA_STEP_ROWS + j
        r0 = jnp.clip(r - NA_ROWS // 2, 0, n_rows - NA_ROWS)
        shift = r - r0
        off = pl.multiple_of((r0 - (i - 1) * NA_STEP_ROWS) * GRID_W, GRID_W)
        qrows = pl.ds(pl.multiple_of(j * GRID_W, GRID_W), GRID_W)
        for pi in range(NA_HEADS // 2):
            cols = slice(pi * LANES, (pi + 1) * LANES)
            qpair = q_ref[qrows, cols]
            kwin = kbuf[pl.ds(off, NA_KEYS), cols]
            vwin = vbuf[pl.ds(off, NA_KEYS), cols]
            outs = []
            for e in range(2):
                s = _dot_nt(qpair * hm[e], kwin) + tbl_ref[shift, 2 * pi + e]
                m = jnp.max(s, axis=-1, keepdims=True)
                p = jnp.exp(s - m)
                l = jnp.sum(p, axis=-1, keepdims=True)
                outs.append(jnp.dot(p.astype(BF16), vwin, preferred_element_type=F32) / l)
            o_ref[qrows, cols] = jnp.where(lane < HEAD_DIM, outs[0], outs[1]).astype(o_ref.dtype)
        return carry

    lax.fori_loop(0, NA_STEP_ROWS, row_body, 0)


def _natten(p, tbl, batch, seq):
    t = batch * seq
    nb = seq // NA_STEP_TOK
    tok = NA_STEP_TOK

    def spec(col, which):
        cb = col // A_W
        if which == 0:
            return pl.BlockSpec((tok, A_W), lambda b, i: (b * nb + i, cb))
        if which < 0:
            return pl.BlockSpec((tok, A_W), lambda b, i: (b * nb + jnp.maximum(i - 1, 0), cb))
        return pl.BlockSpec((tok, A_W), lambda b, i: (b * nb + jnp.minimum(i + 1, nb - 1), cb))

    return pl.pallas_call(
        functools.partial(_natten_kernel, n_rows=seq // GRID_W),
        grid=(batch, nb),
        in_specs=[spec(COL_AQ, 0),
                  spec(COL_AK, -1), spec(COL_AK, 0), spec(COL_AK, 1),
                  spec(COL_AV, -1), spec(COL_AV, 0), spec(COL_AV, 1),
                  pl.BlockSpec(tbl.shape, lambda b, i: (0, 0, 0, 0))],
        out_specs=pl.BlockSpec((tok, A_W), lambda b, i: (b * nb + i, 0)),
        out_shape=jax.ShapeDtypeStruct((t, A_W), BF16),
        scratch_shapes=[pltpu.VMEM((3 * tok, A_W), BF16), pltpu.VMEM((3 * tok, A_W), BF16)],
        compiler_params=pltpu.CompilerParams(
            dimension_semantics=("parallel", "arbitrary"), vmem_limit_bytes=V7X_VMEM_LIMIT),
        name="natten",
    )(p, p, p, p, p, p, p, tbl)


def _band_table(slopes, dist_scale, w):
    delta = np.arange(3 * w)[None, :] - w - np.arange(w)[:, None]
    bias = -slopes[:, None, None] * (dist_scale * np.abs(delta)).astype(np.float32)[None]
    return np.where(np.abs(delta)[None] <= w, bias, NEG_INF).astype(np.float32)


def _banded_kernel(*refs, w, tq, n, kvmap, has_sink, has_lse):
    refs = list(refs)
    sink_ref = refs.pop(0) if has_sink else None
    q_ref, kp_ref, kc_ref, kn_ref, vp_ref, vc_ref, vn_ref, tbl_ref, o_ref = refs[:9]
    lse_ref = refs[9] if has_lse else None
    i = pl.program_id(1)
    kbuf = jnp.concatenate([kp_ref[...], kc_ref[...], kn_ref[...]], axis=0)
    vbuf = jnp.concatenate([vp_ref[...], vc_ref[...], vn_ref[...]], axis=0)
    lane, hm = _half_masks()
    for j in range(tq // w):
        rows = slice(j * w, (j + 1) * w)
        kpos = i * tq + (j - 1) * w + lax.broadcasted_iota(jnp.int32, (1, 3 * w), 1)
        range_bias = jnp.where(kpos < 0, NEG_INF, jnp.where(kpos >= n, NEG_INF, 0.0))
        for pi, kv in enumerate(kvmap):
            cols = slice(pi * LANES, (pi + 1) * LANES)
            kvcols = slice(kv * LANES, (kv + 1) * LANES)
            qpair = q_ref[rows, cols]
            kwin = kbuf[j * w:(j + 3) * w, kvcols]
            vwin = vbuf[j * w:(j + 3) * w, kvcols]
            outs, lses = [], []
            for e in range(2):
                h = 2 * pi + e
                s = _dot_nt(qpair * hm[e], kwin) + tbl_ref[h] + range_bias
                m = jnp.max(s, axis=-1, keepdims=True)
                if has_sink:
                    m = jnp.maximum(m, sink_ref[h])
                p = jnp.exp(s - m)
                l = jnp.sum(p, axis=-1, keepdims=True)
                if has_sink:
                    l = l + jnp.exp(sink_ref[h] - m)
                outs.append(jnp.dot(p.astype(BF16), vwin, preferred_element_type=F32) / l)
                lses.append(m + jnp.log(l))
            o_ref[rows, cols] = jnp.where(lane < HEAD_DIM, outs[0], outs[1]).astype(o_ref.dtype)
            if has_lse:
                lse_ref[rows, cols] = jnp.where(lane < HEAD_DIM, lses[0], lses[1])


def _banded(arr, *, w, col_q, col_k, col_v, wq, wkv, kvmap, table, sink, out_dtype, has_lse, name):
    classes, n, _ = arr.shape
    tq = min(512, n)
    sub = tq // w
    nblk = n // w

    def qspec(col, width):
        return pl.BlockSpec((None, tq, width), lambda c, i: (c, i, col // width))

    def prev_spec(col, width):
        return pl.BlockSpec((None, w, width), lambda c, i: (c, jnp.maximum(i * sub - 1, 0), col // width))

    def next_spec(col, width):
        return pl.BlockSpec((None, w, width), lambda c, i: (c, jnp.minimum((i + 1) * sub, nblk - 1), col // width))

    in_specs = [qspec(col_q, wq),
                prev_spec(col_k, wkv), qspec(col_k, wkv), next_spec(col_k, wkv),
                prev_spec(col_v, wkv), qspec(col_v, wkv), next_spec(col_v, wkv),
                pl.BlockSpec(table.shape, lambda c, i: (0, 0, 0))]
    args = [arr] * 7 + [jnp.asarray(table)]
    if sink is not None:
        in_specs = [pl.BlockSpec(memory_space=pltpu.SMEM)] + in_specs
        args = [sink.astype(F32)] + args
    out_block = pl.BlockSpec((None, tq, wq), lambda c, i: (c, i, 0))
    out_shape = [jax.ShapeDtypeStruct((classes, n, wq), out_dtype)]
    out_specs = [out_block]
    if has_lse:
        out_shape.append(jax.ShapeDtypeStruct((classes, n, wq), F32))
        out_specs.append(out_block)
    return pl.pallas_call(
        functools.partial(_banded_kernel, w=w, tq=tq, n=n, kvmap=kvmap, has_sink=sink is not None, has_lse=has_lse),
        grid=(classes, n // tq),
        in_specs=in_specs,
        out_specs=out_specs,
        out_shape=out_shape,
        compiler_params=pltpu.CompilerParams(
            dimension_semantics=("parallel", "arbitrary"), vmem_limit_bytes=V7X_VMEM_LIMIT),
        name=name,
    )(*args)


def _fold(x, batch, dil):
    c = x.shape[-1]
    return x.reshape(batch, -1, dil, c).swapaxes(1, 2).reshape(batch * dil, -1, c)


def _unfold(x, batch, dil):
    c = x.shape[-1]
    return x.reshape(batch, dil, -1, c).swapaxes(1, 2).reshape(-1, c)


def _rope_tables(seq):
    t = jnp.arange(seq)
    npair = HEAD_DIM // 4
    freqs = ROPE_THETA ** (-jnp.arange(npair, dtype=F32) / npair)

    def cs(pos):
        ang = pos.astype(F32)[:, None] * freqs[None, :]
        return jnp.cos(ang), jnp.sin(ang)

    cr, sr = cs(t // GRID_W)
    cc, sc = cs(t % GRID_W)
    cos = jnp.concatenate([cr, cr, cc, cc], axis=-1)
    sin = jnp.concatenate([-sr, sr, -sc, sc], axis=-1)
    return jnp.tile(cos, (1, 2)), jnp.tile(sin, (1, 2))


def _cprep_kernel(q_ref, k_ref, v_ref, cos_ref, sin_ref, qg_ref, kg_ref, avg_ref, qo_ref, ko_ref, ve_ref, vo_ref):
    lane = lax.broadcasted_iota(jnp.int32, (1, LANES), 1)
    v = v_ref[...]
    ones = jnp.ones(v.shape, v.dtype)
    ve_ref[...] = jnp.where(lane < HEAD_DIM, v, ones)
    vo_ref[...] = jnp.where(lane < HEAD_DIM, ones, v)
    first = (lane & (HEAD_DIM // 4)) == 0
    cos = cos_ref[...]
    sin = sin_ref[...]
    avg = avg_ref[...]

    def norm_rope(x, gain):
        x = x.astype(F32)
        sq = x * x
        hi = sq.astype(BF16)
        r1 = sq - hi.astype(F32)
        mid = r1.astype(BF16)
        lo = (r1 - mid.astype(F32)).astype(BF16)
        ms = (jnp.dot(hi, avg, preferred_element_type=F32) + jnp.dot(mid, avg, preferred_element_type=F32)
              + jnp.dot(lo, avg, preferred_element_type=F32))
        y = x * lax.rsqrt(ms + RMS_EPS) * gain
        quarter = HEAD_DIM // 4
        partner = jnp.where(first, pltpu.roll(y, LANES - quarter, 1), pltpu.roll(y, quarter, 1))
        return y * cos + partner * sin

    for c in range(q_ref.shape[1] // LANES):
        cols = slice(c * LANES, (c + 1) * LANES)
        qo_ref[:, cols] = norm_rope(q_ref[:, cols], qg_ref[...]).astype(qo_ref.dtype)
    ko_ref[...] = norm_rope(k_ref[...], kg_ref[...]).astype(ko_ref.dtype)


def _cprep(p, cos, sin, q_gain, k_gain, batch, seq, tm=512):
    t = batch * seq
    nt = seq // tm
    avg = np.zeros((LANES, LANES), np.float32)
    avg[:HEAD_DIM, :HEAD_DIM] = 1.0 / HEAD_DIM
    avg[HEAD_DIM:, HEAD_DIM:] = 1.0 / HEAD_DIM
    gw = C_QW // GA_KV_HEADS
    qg = jnp.tile(q_gain.astype(F32) * (HEAD_DIM ** -0.5 * np.log2(np.e)), 2).reshape(1, LANES)
    kg = jnp.tile(k_gain.astype(F32), 2).reshape(1, LANES)
    return pl.pallas_call(
        _cprep_kernel,
        grid=(t // tm, GA_KV_HEADS),
        in_specs=[pl.BlockSpec((tm, gw), lambda i, g: (i, COL_CQ // gw + g)),
                  pl.BlockSpec((tm, LANES), lambda i, g: (i, COL_CK // LANES + g)),
                  pl.BlockSpec((tm, LANES), lambda i, g: (i, COL_CV // LANES + g)),
                  pl.BlockSpec((tm, LANES), lambda i, g: (i % nt, 0)),
                  pl.BlockSpec((tm, LANES), lambda i, g: (i % nt, 0)),
                  pl.BlockSpec((1, LANES), lambda i, g: (0, 0)),
                  pl.BlockSpec((1, LANES), lambda i, g: (0, 0)),
                  pl.BlockSpec((LANES, LANES), lambda i, g: (0, 0))],
        out_specs=[pl.BlockSpec((tm, gw), lambda i, g: (i, g))] + [pl.BlockSpec((tm, LANES), lambda i, g: (i, g))] * 3,
        out_shape=[jax.ShapeDtypeStruct((t, C_QW), BF16)] + [jax.ShapeDtypeStruct((t, GA_KV_HEADS * LANES), BF16)] * 3,
        compiler_params=pltpu.CompilerParams(
            dimension_semantics=("parallel", "arbitrary"), vmem_limit_bytes=V7X_VMEM_LIMIT),
        name="cprep",
    )(p, p, p, cos, sin, qg, kg, jnp.asarray(avg, BF16))


def _flash_kernel(q_ref, k_ref, ve_ref, vo_ref, o_ref, qs_ref, m_ref, acc_ref, *, tk):
    nkv = k_ref.shape[0] // tk
    nheads = qs_ref.shape[0]
    lane, hm = _half_masks()
    for h in range(nheads):
        qs_ref[h] = q_ref[:, (h // 2) * LANES:(h // 2 + 1) * LANES] * hm[h % 2]
    m_ref[...] = jnp.full(m_ref.shape, NEG_INF, F32)
    acc_ref[...] = jnp.zeros(acc_ref.shape, F32)

    def body(t, carry):
        rows = pl.ds(pl.multiple_of(t * tk, tk), tk)
        k = k_ref[rows, :]
        vs = (ve_ref[rows, :], vo_ref[rows, :])
        for h in range(nheads):
            s = _dot_nt(qs_ref[h], k)
            m_old = m_ref[h]
            m_new = jnp.maximum(m_old, jnp.max(s, axis=-1, keepdims=True))
            alpha = jnp.exp2(m_old - m_new)
            p = jnp.concatenate([jnp.exp2(s[:, c * LANES:(c + 1) * LANES] - m_new) for c in range(tk // LANES)],
                                axis=1)
            acc_ref[h] = alpha * acc_ref[h] + jnp.dot(p.astype(BF16), vs[h % 2], preferred_element_type=F32)
            m_ref[h] = m_new
        return carry

    lax.fori_loop(0, nkv, body, 0)
    for pi in range(nheads // 2):
        even, odd = acc_ref[2 * pi], acc_ref[2 * pi + 1]
        even = even / pltpu.roll(even, HEAD_DIM, 1)
        odd = odd / pltpu.roll(odd, HEAD_DIM, 1)
        o_ref[:, pi * LANES:(pi + 1) * LANES] = jnp.where(lane < HEAD_DIM, even, odd).astype(o_ref.dtype)


def _flash(q, k, ve, vo, batch, seq, tq=1024, tk=512):
    t = batch * seq
    nq = seq // tq
    gw = C_QW // GA_KV_HEADS
    nheads = GA_Q_HEADS // GA_KV_HEADS
    kv_spec = pl.BlockSpec((seq, LANES), lambda b, g, i: (b, g))
    return pl.pallas_call(
        functools.partial(_flash_kernel, tk=tk),
        grid=(batch, GA_KV_HEADS, nq),
        in_specs=[pl.BlockSpec((tq, gw), lambda b, g, i: (b * nq + i, g)), kv_spec, kv_spec, kv_spec],
        out_specs=pl.BlockSpec((tq, gw), lambda b, g, i: (b * nq + i, g)),
        out_shape=jax.ShapeDtypeStruct((t, C_QW), BF16),
        scratch_shapes=[pltpu.VMEM((nheads, tq, LANES), BF16), pltpu.VMEM((nheads, tq, LANES), F32),
                        pltpu.VMEM((nheads, tq, LANES), F32)],
        compiler_params=pltpu.CompilerParams(
            dimension_semantics=("parallel", "parallel", "arbitrary"), vmem_limit_bytes=V7X_VMEM_LIMIT),
        name="flash",
    )(q, k, ve, vo)


def _merge_kernel(x_ref, ya_ref, ob0_ref, ob1_ref, ob2_ref, ls0_ref, ls1_ref, ls2_ref, yc_ref, yd_ref,
                  ga_ref, gb_ref, gc_ref, gd_ref, wa_ref, wb_ref, wc_ref, wd_ref, wo_ref, gn_ref,
                  xo_ref, ho_ref):
    l0, l1, l2 = ls0_ref[...], ls1_ref[...], ls2_ref[...]
    mx = jnp.maximum(jnp.maximum(l0, l1), l2)
    e0, e1, e2 = jnp.exp(l0 - mx), jnp.exp(l1 - mx), jnp.exp(l2 - mx)
    yb = (e0 * ob0_ref[...] + e1 * ob1_ref[...] + e2 * ob2_ref[...]) / (e0 + e1 + e2)

    def branch(g_ref, y, w_ref):
        return g_ref[...].astype(F32) * jnp.dot(y, w_ref[...], preferred_element_type=F32)

    merged = (branch(ga_ref, ya_ref[...], wa_ref) + branch(gb_ref, yb.astype(BF16), wb_ref)
              + branch(gc_ref, yc_ref[...], wc_ref) + branch(gd_ref, yd_ref[...], wd_ref))
    out = x_ref[...] + jnp.dot(merged.astype(BF16), wo_ref[...], preferred_element_type=F32)
    xo_ref[...] = out
    ho_ref[...] = _rms_rows(out, gn_ref[...]).astype(ho_ref.dtype)


def _merge(x, ya, obs, lss, yc, yd, gates, wa, wb, wc, wd, wo, gain, tm=256):
    t, d = x.shape

    def rows(width, col=0):
        return pl.BlockSpec((tm, width), lambda i: (i, col))

    def const(arr):
        return pl.BlockSpec(arr.shape, lambda i: (0, 0), pipeline_mode=pl.Buffered(1))

    gain = gain.reshape(1, d)
    return pl.pallas_call(
        _merge_kernel,
        grid=(t // tm,),
        in_specs=[rows(d), rows(A_W), rows(B_OUT), rows(B_OUT), rows(B_OUT), rows(B_OUT), rows(B_OUT), rows(B_OUT),
                  rows(C_QW), rows(D_QW), rows(d, 0), rows(d, 1), rows(d, 2), rows(d, 3),
                  const(wa), const(wb), const(wc), const(wd), const(wo), const(gain)],
        out_specs=[rows(d), rows(d)],
        out_shape=[jax.ShapeDtypeStruct((t, d), F32), jax.ShapeDtypeStruct((t, d), BF16)],
        compiler_params=pltpu.CompilerParams(dimension_semantics=("parallel",), vmem_limit_bytes=V7X_VMEM_LIMIT),
        name="merge",
    )(x, ya, *obs, *lss, yc, yd, gates, gates, gates, gates, wa, wb, wc, wd, wo, gain)


def _mlp_kernel(h_ref, x_ref, wu_ref, wd_ref, gn_ref, *rest, final):
    if final:
        yo_ref, acc_ref = rest
    else:
        xo_ref, ho_ref, acc_ref = rest
    f = pl.program_id(1)
    u = jnp.dot(h_ref[...], wu_ref[...], preferred_element_type=F32)
    a = jnp.square(jnp.maximum(u, 0.0)).astype(BF16)
    part = jnp.dot(a, wd_ref[...], preferred_element_type=F32)

    @pl.when(f == 0)
    def _():
        acc_ref[...] = x_ref[...] + part

    @pl.when(f > 0)
    def _():
        acc_ref[...] += part

    @pl.when(f == pl.num_programs(1) - 1)
    def _():
        out = acc_ref[...]
        normed = _rms_rows(out, gn_ref[...])
        if final:
            yo_ref[...] = normed
        else:
            xo_ref[...] = out
            ho_ref[...] = normed.astype(ho_ref.dtype)


def _mlp(h, x, wu, wd, gain, *, final, tm=512, tf=1024):
    t, d = x.shape
    ff = wu.shape[1]
    rows = pl.BlockSpec((tm, d), lambda i, f: (i, 0))
    if final:
        out_specs = [rows]
        out_shape = [jax.ShapeDtypeStruct((t, d), F32)]
    else:
        out_specs = [rows, rows]
        out_shape = [jax.ShapeDtypeStruct((t, d), F32), jax.ShapeDtypeStruct((t, d), BF16)]
    return pl.pallas_call(
        functools.partial(_mlp_kernel, final=final),
        grid=(t // tm, ff // tf),
        in_specs=[rows, rows,
                  pl.BlockSpec((d, tf), lambda i, f: (0, f)),
                  pl.BlockSpec((tf, d), lambda i, f: (f, 0)),
                  pl.BlockSpec((1, d), lambda i, f: (0, 0))],
        out_specs=out_specs,
        out_shape=out_shape,
        scratch_shapes=[pltpu.VMEM((tm, d), F32)],
        compiler_params=pltpu.CompilerParams(
            dimension_semantics=("parallel", "arbitrary"), vmem_limit_bytes=V7X_VMEM_LIMIT),
        name="mlp_final" if final else "mlp",
    )(h, x, wu, wd, gain.reshape(1, d))


def _prep_in_weights(w_in):
    offs = np.cumsum((0,) + SPLIT_SIZES)
    aq, ak, av, bq, bk, bv, cq, ck, cv, dq, dk, dv, gl = [w_in[..., offs[s]:offs[s + 1]] for s in range(13)]
    scale = HEAD_DIM ** -0.5

    def dup(t):
        k0, k1 = t[..., :HEAD_DIM], t[..., HEAD_DIM:]
        return jnp.concatenate([k0, k0, k1, k1], axis=-1)

    b_groups = []
    for g in range(len(DIL_PATTERNS)):
        sl = slice(g * B_OUT, (g + 1) * B_OUT)
        b_groups += [bq[..., sl] * scale, bk[..., sl], bv[..., sl]]
    pad = jnp.zeros(w_in.shape[:-1] + (P_WIDTH - COL_DV - 2 * D_KVW,), w_in.dtype)
    w_mix = jnp.concatenate([aq * scale, ak, av, cq, dq * scale] + b_groups
                            + [dup(ck), dup(cv), dup(dk), dup(dv), pad], axis=-1)
    return w_mix.astype(BF16), gl.astype(BF16)


def _trunk(x, wts, tables):
    batch, seq, d = x.shape
    t = batch * seq
    depth = wts["w_mix"].shape[0]
    cos, sin = _rope_tables(seq)
    x2 = x.reshape(t, d)
    h = _rmsnorm(x2, wts["norm_mix"][0])
    b_slopes = _alibi_slopes(DIL_HEADS)
    d_table = _band_table(_alibi_slopes(SW_Q_HEADS), 1, SW_HALF_WINDOW)
    for l in range(depth):
        p = _proj(h, wts["w_mix"][l], sigmoid=False, name="proj_mix")
        gates = _proj(h, wts["w_gate"][l], sigmoid=True, name="proj_gate")
        ya = _natten(p, tables["natten"][l], batch, seq)
        obs, lss = [], []
        hg = DIL_HEADS_PER_GROUP
        p3 = p.reshape(batch, seq, P_WIDTH)
        for g, (win, dil) in enumerate(DIL_PATTERNS):
            col = COL_B + 3 * g * B_OUT
            if dil == 1:
                arr = p3
            else:
                arr, col = _fold(p[:, col:col + 3 * B_OUT], batch, dil), 0
            ob, ls = _banded(arr, w=win // (2 * dil), col_q=col, col_k=col + B_OUT, col_v=col + 2 * B_OUT,
                             wq=B_OUT, wkv=B_OUT, kvmap=(0, 1),
                             table=_band_table(b_slopes[g * hg:(g + 1) * hg], dil, win // (2 * dil)),
                             sink=None, out_dtype=F32, has_lse=True, name="banded_d%d" % dil)
            obs.append(_unfold(ob, batch, dil))
            lss.append(_unfold(ls, batch, dil))
        qr, kr, ve, vo = _cprep(p, cos, sin, wts["q_gain_c"][l], wts["k_gain_c"][l], batch, seq)
        yc = _flash(qr, kr, ve, vo, batch, seq)
        (yd,) = _banded(p3, w=SW_HALF_WINDOW, col_q=COL_DQ, col_k=COL_DK, col_v=COL_DV, wq=D_QW, wkv=2 * D_KVW,
                        kvmap=(0, 0, 1, 1), table=d_table, sink=wts["sink_d"][l], out_dtype=BF16, has_lse=False,
                        name="banded_sink")
        yd = yd.reshape(t, D_QW)
        x2, h = _merge(x2, ya, obs, lss, yc, yd, gates, wts["w_a"][l], wts["w_b"][l], wts["w_c"][l],
                       wts["w_d"][l], wts["w_out"][l], wts["norm_mlp"][l])
        if l + 1 < depth:
            x2, h = _mlp(h, x2, wts["w_up"][l], wts["w_down"][l], wts["norm_mix"][l + 1], final=False)
        else:
            (y,) = _mlp(h, x2, wts["w_up"][l], wts["w_down"][l], wts["norm_final"], final=True)
    return y.reshape(batch, seq, d)


def kernel(x_prompt, x_sample, norm_mix, w_in, rel_bias_a, q_gain_c, k_gain_c, sink_d, w_branch_a, w_branch_b,
           w_branch_c, w_branch_d, w_out, norm_mlp, w_up, w_down, norm_final):
    w_mix, w_gate = _prep_in_weights(w_in)
    wts = dict(norm_mix=norm_mix, w_mix=w_mix, w_gate=w_gate, q_gain_c=q_gain_c, k_gain_c=k_gain_c, sink_d=sink_d,
               w_a=w_branch_a.astype(BF16), w_b=w_branch_b.astype(BF16), w_c=w_branch_c.astype(BF16),
               w_d=w_branch_d.astype(BF16), w_out=w_out.astype(BF16), norm_mlp=norm_mlp,
               w_up=w_up.astype(BF16), w_down=w_down.astype(BF16), norm_final=norm_final)
    tables = dict(natten=jax.vmap(_natten_bias_table)(rel_bias_a))
    return (_trunk(x_prompt, wts, tables), _trunk(x_sample, wts, tables))
```

```python
import functools

import numpy as np
import jax
import jax.numpy as jnp
from jax import lax
from jax.experimental import pallas as pl
from jax.experimental.pallas import tpu as pltpu

F32 = jnp.float32
BF16 = jnp.bfloat16

D_MODEL = 2048
HEAD_DIM = 64
GRID_W = 64
NA_HEADS = 8
NA_ROWS = 8
NA_COLS = 16
DIL_PATTERNS = ((128, 1), (512, 4), (2048, 16))
DIL_HEADS_PER_GROUP = 4
DIL_HEADS = 12
DIL_BLOCK = 64
GA_Q_HEADS = 8
GA_KV_HEADS = 2
ROPE_THETA = 10000.0
SW_Q_HEADS = 8
SW_KV_HEADS = 2
SW_HALF_WINDOW = 128
N_BRANCHES = 4
D_FF = 4 * D_MODEL
RMS_EPS = 1e-6
NEG_INF = -1e30

A_W = NA_HEADS * HEAD_DIM
B_W = DIL_HEADS * HEAD_DIM
B_OUT = DIL_HEADS_PER_GROUP * HEAD_DIM
C_QW = GA_Q_HEADS * HEAD_DIM
C_KVW = GA_KV_HEADS * HEAD_DIM
D_QW = SW_Q_HEADS * HEAD_DIM
D_KVW = SW_KV_HEADS * HEAD_DIM
GATE_W = N_BRANCHES * D_MODEL
SPLIT_SIZES = (A_W, A_W, A_W, B_W, B_W, B_W, C_QW, C_KVW, C_KVW, D_QW, D_KVW, D_KVW, GATE_W)

LANES = 128
V7X_VMEM_LIMIT = 56 * 1024 * 1024

COL_AQ = 0
COL_AK = 512
COL_AV = 1024
COL_CQ = 1536
COL_DQ = 2048
COL_B = 2560
COL_CK = 4864
COL_CV = 5120
COL_DK = 5376
COL_DV = 5632
P_WIDTH = 6144


def _alibi_slopes(n):
    return (2.0 ** (-8.0 * np.arange(1, n + 1) / n)).astype(np.float32)


def _half_masks():
    lane = lax.broadcasted_iota(jnp.int32, (1, LANES), 1)
    lo = jnp.where(lane < HEAD_DIM, 1.0, 0.0).astype(BF16)
    hi = jnp.where(lane >= HEAD_DIM, 1.0, 0.0).astype(BF16)
    return lane, (lo, hi)


def _dot_nt(a, b):
    return lax.dot_general(a, b, (((1,), (1,)), ((), ())), preferred_element_type=F32)


def _rms_rows(x, gain):
    ms = jnp.mean(x * x, axis=-1, keepdims=True)
    return x * lax.rsqrt(ms + RMS_EPS) * gain


def _rmsnorm_kernel(x_ref, g_ref, o_ref):
    o_ref[...] = _rms_rows(x_ref[...], g_ref[...]).astype(o_ref.dtype)


def _rmsnorm(x, gain, tm=512):
    t, d = x.shape
    return pl.pallas_call(
        _rmsnorm_kernel,
        grid=(t // tm,),
        in_specs=[pl.BlockSpec((tm, d), lambda i: (i, 0)), pl.BlockSpec((1, d), lambda i: (0, 0))],
        out_specs=pl.BlockSpec((tm, d), lambda i: (i, 0)),
        out_shape=jax.ShapeDtypeStruct((t, d), BF16),
        compiler_params=pltpu.CompilerParams(dimension_semantics=("parallel",), vmem_limit_bytes=V7X_VMEM_LIMIT),
        name="rmsnorm",
    )(x, gain.reshape(1, d))


def _proj_kernel(h_ref, w_ref, o_ref, *, row_chunk, sigmoid):
    for r in range(h_ref.shape[0] // row_chunk):
        rows = pl.ds(r * row_chunk, row_chunk)
        acc = jnp.dot(h_ref[rows, :], w_ref[...], preferred_element_type=F32)
        if sigmoid:
            acc = 1.0 / (1.0 + jnp.exp(-acc))
        o_ref[rows, :] = acc.astype(o_ref.dtype)


def _proj(h, w, *, sigmoid, tm=1024, tn=2048, name):
    t, k = h.shape
    n = w.shape[1]
    return pl.pallas_call(
        functools.partial(_proj_kernel, row_chunk=256, sigmoid=sigmoid),
        grid=(t // tm, n // tn),
        in_specs=[pl.BlockSpec((tm, k), lambda i, j: (i, 0)), pl.BlockSpec((k, tn), lambda i, j: (0, j))],
        out_specs=pl.BlockSpec((tm, tn), lambda i, j: (i, j)),
        out_shape=jax.ShapeDtypeStruct((t, n), BF16),
        compiler_params=pltpu.CompilerParams(
            dimension_semantics=("parallel", "arbitrary"), vmem_limit_bytes=V7X_VMEM_LIMIT),
        name=name,
    )(h, w)


NA_STEP_ROWS = 8
NA_STEP_TOK = NA_STEP_ROWS * GRID_W
NA_KEYS = NA_ROWS * GRID_W


def _natten_bias_table(rel_bias):
    qc = np.arange(GRID_W)[:, None]
    kc = np.arange(GRID_W)[None, :]
    c0 = np.clip(qc - NA_COLS // 2, 0, GRID_W - NA_COLS)
    valid = (kc >= c0) & (kc < c0 + NA_COLS)
    dc = np.clip(kc - qc, -(NA_COLS - 1), NA_COLS - 1) + NA_COLS - 1
    shift = np.arange(NA_ROWS)[:, None]
    krow = np.arange(NA_ROWS)[None, :]
    dr = krow - shift + NA_ROWS - 1
    tbl = rel_bias[:, dr][:, :, :, dc]
    tbl = jnp.where(valid[None, None, None], tbl, NEG_INF)
    tbl = jnp.transpose(tbl, (1, 0, 3, 2, 4))
    return tbl.reshape(NA_ROWS, NA_HEADS // 2, 2 * GRID_W, NA_KEYS).astype(F32)


def _natten_kernel(q_ref, kp_ref, kc_ref, kn_ref, vp_ref, vc_ref, vn_ref, tbl_ref, o_ref, qs, kbuf, vbuf, *, n_rows):
    i = pl.program_id(1)
    for s, (kr, vr) in enumerate(((kp_ref, vp_ref), (kc_ref, vc_ref), (kn_ref, vn_ref))):
        kbuf[s * NA_STEP_TOK:(s + 1) * NA_STEP_TOK, :] = kr[...]
        vbuf[s * NA_STEP_TOK:(s + 1) * NA_STEP_TOK, :] = vr[...]
    lane, hm = _half_masks()
    for j in range(NA_STEP_ROWS):
        for pi in range(NA_HEADS // 2):
            qpair = q_ref[j * GRID_W:(j + 1) * GRID_W, pi * LANES:(pi + 1) * LANES]
            for e in range(2):
                qs[j, pi, e * GRID_W:(e + 1) * GRID_W, :] = qpair * hm[e]

    shifts, offs = [], []
    for j in range(NA_STEP_ROWS):
        r = i * NA_STEP_ROWS + j
        r0 = jnp.clip(r - NA_ROWS // 2, 0, n_rows - NA_ROWS)
        shifts.append(r - r0)
        offs.append(pl.multiple_of((r0 - (i - 1) * NA_STEP_ROWS) * GRID_W, GRID_W))
    for pi in range(NA_HEADS // 2):
        cols = slice(pi * LANES, (pi + 1) * LANES)
        s = jnp.concatenate([_dot_nt(qs[j, pi], kbuf[pl.ds(offs[j], NA_KEYS), cols]) + tbl_ref[shifts[j], pi]
                             for j in range(NA_STEP_ROWS)], axis=0)
        m = jnp.broadcast_to(jnp.max(s, axis=-1, keepdims=True), (s.shape[0], LANES))
        p = jnp.concatenate([jnp.exp(s[:, k * LANES:(k + 1) * LANES] - m) for k in range(NA_KEYS // LANES)], axis=1)
        inv = 1.0 / jnp.broadcast_to(jnp.sum(p, axis=-1, keepdims=True), (s.shape[0], LANES))
        p = p.astype(BF16)
        for j in range(NA_STEP_ROWS):
            rows = slice(j * 2 * GRID_W, (j + 1) * 2 * GRID_W)
            o = jnp.dot(p[rows], vbuf[pl.ds(offs[j], NA_KEYS), cols], preferred_element_type=F32) * inv[rows]
            o_ref[j * GRID_W:(j + 1) * GRID_W, cols] = jnp.where(lane < HEAD_DIM, o[:GRID_W], o[GRID_W:]).astype(
                o_ref.dtype)


def _natten(p, tbl, batch, seq):
    t = batch * seq
    nb = seq // NA_STEP_TOK
    tok = NA_STEP_TOK

    def spec(col, which):
        cb = col // A_W
        if which == 0:
            return pl.BlockSpec((tok, A_W), lambda b, i: (b * nb + i, cb))
        if which < 0:
            return pl.BlockSpec((tok, A_W), lambda b, i: (b * nb + jnp.maximum(i - 1, 0), cb))
        return pl.BlockSpec((tok, A_W), lambda b, i: (b * nb + jnp.minimum(i + 1, nb - 1), cb))

    return pl.pallas_call(
        functools.partial(_natten_kernel, n_rows=seq // GRID_W),
        grid=(batch, nb),
        in_specs=[spec(COL_AQ, 0),
                  spec(COL_AK, -1), spec(COL_AK, 0), spec(COL_AK, 1),
                  spec(COL_AV, -1), spec(COL_AV, 0), spec(COL_AV, 1),
                  pl.BlockSpec(tbl.shape, lambda b, i: (0, 0, 0, 0))],
        out_specs=pl.BlockSpec((tok, A_W), lambda b, i: (b * nb + i, 0)),
        out_shape=jax.ShapeDtypeStruct((t, A_W), BF16),
        scratch_shapes=[pltpu.VMEM((NA_STEP_ROWS, NA_HEADS // 2, 2 * GRID_W, LANES), BF16),
                        pltpu.VMEM((3 * tok, A_W), BF16), pltpu.VMEM((3 * tok, A_W), BF16)],
        compiler_params=pltpu.CompilerParams(
            dimension_semantics=("parallel", "arbitrary"), vmem_limit_bytes=V7X_VMEM_LIMIT),
        name="natten",
    )(p, p, p, p, p, p, p, tbl)


BAND_SUB = 128


def _band_table(slopes, dist_scale, w, chains):
    delta = np.arange(BAND_SUB + 2 * w)[None, :] - w - np.arange(BAND_SUB)[:, None]
    bias = -slopes[:, None, None] * (dist_scale * np.abs(delta)).astype(np.float32)[None]
    tbl = np.where(np.abs(delta)[None] <= w, bias, NEG_INF).astype(np.float32)
    return tbl.reshape(chains, -1, BAND_SUB + 2 * w)


def _banded_kernel(*refs, w, tq, n, kvmap, has_sink, has_lse):
    refs = list(refs)
    sink_ref = refs.pop(0) if has_sink else None
    q_ref, kp_ref, kc_ref, kn_ref, vp_ref, vc_ref, vn_ref, tbl_ref, o_ref = refs[:9]
    lse_ref = refs[9] if has_lse else None
    qs, kbuf, vbuf = refs[-3:]
    sub = BAND_SUB
    nchains = len(kvmap)
    blocks = q_ref.shape[1] // LANES // nchains
    i = pl.program_id(1)
    kbuf[0:w] = kp_ref[...]
    kbuf[w:w + tq] = kc_ref[...]
    kbuf[w + tq:] = kn_ref[...]
    vbuf[0:w] = vp_ref[...]
    vbuf[w:w + tq] = vc_ref[...]
    vbuf[w + tq:] = vn_ref[...]
    lane, hm = _half_masks()
    for j in range(tq // sub):
        for pi in range(nchains * blocks):
            qpair = q_ref[j * sub:(j + 1) * sub, pi * LANES:(pi + 1) * LANES]
            for e in range(2):
                qs[j, (2 * pi + e) * sub:(2 * pi + e + 1) * sub, :] = qpair * hm[e]
    nsub = tq // sub
    rows = 2 * blocks * sub
    nkeys = sub + 2 * w
    range_bias = []
    for j in range(nsub):
        kpos = i * tq + j * sub - w + lax.broadcasted_iota(jnp.int32, (1, nkeys), 1)
        range_bias.append(jnp.where(kpos < 0, NEG_INF, jnp.where(kpos >= n, NEG_INF, 0.0)))
    for c, kv in enumerate(kvmap):
        kvcols = slice(kv * LANES, (kv + 1) * LANES)
        tbl = tbl_ref[c]
        s = jnp.concatenate([_dot_nt(qs[j, c * rows:(c + 1) * rows, :], kbuf[j * sub:j * sub + nkeys, kvcols])
                             + tbl + range_bias[j] for j in range(nsub)], axis=0)
        m = jnp.max(s, axis=-1, keepdims=True)
        if has_sink:
            sink = jnp.concatenate([jnp.full((sub, LANES), sink_ref[c * 2 * blocks + h], F32)
                                    for h in range(2 * blocks)] * nsub, axis=0)
            m = jnp.maximum(m, sink)
        else:
            m = jnp.broadcast_to(m, (m.shape[0], LANES))
        p = jnp.concatenate([jnp.exp(s[:, k * LANES:(k + 1) * LANES] - m) for k in range(nkeys // LANES)], axis=1)
        l = jnp.sum(p, axis=-1, keepdims=True)
        if has_sink:
            l = l + jnp.exp(sink - m)
        else:
            l = jnp.broadcast_to(l, (l.shape[0], LANES))
        inv = 1.0 / l
        lse = m + jnp.log(l)
        p = p.astype(BF16)
        for j in range(nsub):
            jr = slice(j * rows, (j + 1) * rows)
            o = jnp.dot(p[jr], vbuf[j * sub:j * sub + nkeys, kvcols], preferred_element_type=F32) * inv[jr]
            for x in range(blocks):
                cols = slice((c * blocks + x) * LANES, (c * blocks + x + 1) * LANES)
                even, odd = slice(2 * x * sub, (2 * x + 1) * sub), slice((2 * x + 1) * sub, (2 * x + 2) * sub)
                o_ref[j * sub:(j + 1) * sub, cols] = jnp.where(lane < HEAD_DIM, o[even], o[odd]).astype(o_ref.dtype)
                if has_lse:
                    lse_j = lse[jr]
                    lse_ref[j * sub:(j + 1) * sub, cols] = jnp.where(lane < HEAD_DIM, lse_j[even], lse_j[odd])


def _banded(arr, *, w, col_q, col_k, col_v, wq, wkv, kvmap, table, sink, out_dtype, has_lse, name):
    classes, n, _ = arr.shape
    tq = min(512, n)
    sub = tq // w
    nblk = n // w

    def qspec(col, width):
        return pl.BlockSpec((None, tq, width), lambda c, i: (c, i, col // width))

    def prev_spec(col, width):
        return pl.BlockSpec((None, w, width), lambda c, i: (c, jnp.maximum(i * sub - 1, 0), col // width))

    def next_spec(col, width):
        return pl.BlockSpec((None, w, width), lambda c, i: (c, jnp.minimum((i + 1) * sub, nblk - 1), col // width))

    in_specs = [qspec(col_q, wq),
                prev_spec(col_k, wkv), qspec(col_k, wkv), next_spec(col_k, wkv),
                prev_spec(col_v, wkv), qspec(col_v, wkv), next_spec(col_v, wkv),
                pl.BlockSpec(table.shape, lambda c, i: (0, 0, 0))]
    args = [arr] * 7 + [jnp.asarray(table)]
    if sink is not None:
        in_specs = [pl.BlockSpec(memory_space=pltpu.SMEM)] + in_specs
        args = [sink.astype(F32)] + args
    out_block = pl.BlockSpec((None, tq, wq), lambda c, i: (c, i, 0))
    out_shape = [jax.ShapeDtypeStruct((classes, n, wq), out_dtype)]
    out_specs = [out_block]
    if has_lse:
        out_shape.append(jax.ShapeDtypeStruct((classes, n, wq), F32))
        out_specs.append(out_block)
    return pl.pallas_call(
        functools.partial(_banded_kernel, w=w, tq=tq, n=n, kvmap=kvmap, has_sink=sink is not None, has_lse=has_lse),
        grid=(classes, n // tq),
        in_specs=in_specs,
        out_specs=out_specs,
        out_shape=out_shape,
        scratch_shapes=[pltpu.VMEM((tq // BAND_SUB, 2 * (wq // LANES) * BAND_SUB, LANES), BF16),
                        pltpu.VMEM((tq + 2 * w, wkv), BF16), pltpu.VMEM((tq + 2 * w, wkv), BF16)],
        compiler_params=pltpu.CompilerParams(
            dimension_semantics=("parallel", "arbitrary"), vmem_limit_bytes=V7X_VMEM_LIMIT),
        name=name,
    )(*args)


def _fold(x, batch, dil):
    c = x.shape[-1]
    return x.reshape(batch, -1, dil, c).swapaxes(1, 2).reshape(batch * dil, -1, c)


def _unfold(x, batch, dil):
    c = x.shape[-1]
    return x.reshape(batch, dil, -1, c).swapaxes(1, 2).reshape(-1, c)


def _rope_tables(seq):
    t = jnp.arange(seq)
    npair = HEAD_DIM // 4
    freqs = ROPE_THETA ** (-jnp.arange(npair, dtype=F32) / npair)

    def cs(pos):
        ang = pos.astype(F32)[:, None] * freqs[None, :]
        return jnp.cos(ang), jnp.sin(ang)

    cr, sr = cs(t // GRID_W)
    cc, sc = cs(t % GRID_W)
    cos = jnp.concatenate([cr, cr, cc, cc], axis=-1)
    sin = jnp.concatenate([-sr, sr, -sc, sc], axis=-1)
    return jnp.tile(cos, (1, 2)), jnp.tile(sin, (1, 2))


def _cprep_kernel(q_ref, k_ref, v_ref, cos_ref, sin_ref, qg_ref, kg_ref, avg_ref, qo_ref, ko_ref, ve_ref, vo_ref):
    lane = lax.broadcasted_iota(jnp.int32, (1, LANES), 1)
    v = v_ref[...]
    ones = jnp.ones(v.shape, v.dtype)
    ve_ref[...] = jnp.where(lane < HEAD_DIM, v, ones)
    vo_ref[...] = jnp.where(lane < HEAD_DIM, ones, v)
    first = (lane & (HEAD_DIM // 4)) == 0
    cos = cos_ref[...]
    sin = sin_ref[...]
    avg = avg_ref[...]

    def norm_rope(x, gain):
        x = x.astype(F32)
        sq = x * x
        hi = sq.astype(BF16)
        r1 = sq - hi.astype(F32)
        mid = r1.astype(BF16)
        lo = (r1 - mid.astype(F32)).astype(BF16)
        ms = (jnp.dot(hi, avg, preferred_element_type=F32) + jnp.dot(mid, avg, preferred_element_type=F32)
              + jnp.dot(lo, avg, preferred_element_type=F32))
        y = x * lax.rsqrt(ms + RMS_EPS) * gain
        quarter = HEAD_DIM // 4
        partner = jnp.where(first, pltpu.roll(y, LANES - quarter, 1), pltpu.roll(y, quarter, 1))
        return y * cos + partner * sin

    for c in range(q_ref.shape[1] // LANES):
        cols = slice(c * LANES, (c + 1) * LANES)
        qo_ref[:, cols] = norm_rope(q_ref[:, cols], qg_ref[...]).astype(qo_ref.dtype)
    ko_ref[...] = norm_rope(k_ref[...], kg_ref[...]).astype(ko_ref.dtype)


def _cprep(p, cos, sin, q_gain, k_gain, batch, seq, tm=512):
    t = batch * seq
    nt = seq // tm
    avg = np.zeros((LANES, LANES), np.float32)
    avg[:HEAD_DIM, :HEAD_DIM] = 1.0 / HEAD_DIM
    avg[HEAD_DIM:, HEAD_DIM:] = 1.0 / HEAD_DIM
    gw = C_QW // GA_KV_HEADS
    qg = jnp.tile(q_gain.astype(F32) * (HEAD_DIM ** -0.5 * np.log2(np.e)), 2).reshape(1, LANES)
    kg = jnp.tile(k_gain.astype(F32), 2).reshape(1, LANES)
    return pl.pallas_call(
        _cprep_kernel,
        grid=(t // tm, GA_KV_HEADS),
        in_specs=[pl.BlockSpec((tm, gw), lambda i, g: (i, COL_CQ // gw + g)),
                  pl.BlockSpec((tm, LANES), lambda i, g: (i, COL_CK // LANES + g)),
                  pl.BlockSpec((tm, LANES), lambda i, g: (i, COL_CV // LANES + g)),
                  pl.BlockSpec((tm, LANES), lambda i, g: (i % nt, 0)),
                  pl.BlockSpec((tm, LANES), lambda i, g: (i % nt, 0)),
                  pl.BlockSpec((1, LANES), lambda i, g: (0, 0)),
                  pl.BlockSpec((1, LANES), lambda i, g: (0, 0)),
                  pl.BlockSpec((LANES, LANES), lambda i, g: (0, 0))],
        out_specs=[pl.BlockSpec((tm, gw), lambda i, g: (i, g))] + [pl.BlockSpec((tm, LANES), lambda i, g: (i, g))] * 3,
        out_shape=[jax.ShapeDtypeStruct((t, C_QW), BF16)] + [jax.ShapeDtypeStruct((t, GA_KV_HEADS * LANES), BF16)] * 3,
        compiler_params=pltpu.CompilerParams(
            dimension_semantics=("parallel", "arbitrary"), vmem_limit_bytes=V7X_VMEM_LIMIT),
        name="cprep",
    )(p, p, p, cos, sin, qg, kg, jnp.asarray(avg, BF16))


def _flash_kernel(q_ref, k_ref, ve_ref, vo_ref, o_ref, qs_ref, m_ref, acc_ref, *, tk):
    nkv = k_ref.shape[0] // tk
    nheads = qs_ref.shape[0]
    lane, hm = _half_masks()
    for h in range(nheads):
        qs_ref[h] = q_ref[:, (h // 2) * LANES:(h // 2 + 1) * LANES] * hm[h % 2]
    m_ref[...] = jnp.full(m_ref.shape, NEG_INF, F32)
    acc_ref[...] = jnp.zeros(acc_ref.shape, F32)

    def body(t, carry):
        rows = pl.ds(pl.multiple_of(t * tk, tk), tk)
        k = k_ref[rows, :]
        vs = (ve_ref[rows, :], vo_ref[rows, :])
        for h in range(nheads):
            s = _dot_nt(qs_ref[h], k)
            m_old = m_ref[h]
            m_new = jnp.maximum(m_old, jnp.max(s, axis=-1, keepdims=True))
            alpha = jnp.exp2(m_old - m_new)
            p = jnp.concatenate([jnp.exp2(s[:, c * LANES:(c + 1) * LANES] - m_new) for c in range(tk // LANES)],
                                axis=1)
            acc_ref[h] = alpha * acc_ref[h] + jnp.dot(p.astype(BF16), vs[h % 2], preferred_element_type=F32)
            m_ref[h] = m_new
        return carry

    lax.fori_loop(0, nkv, body, 0)
    for pi in range(nheads // 2):
        even, odd = acc_ref[2 * pi], acc_ref[2 * pi + 1]
        even = even / pltpu.roll(even, HEAD_DIM, 1)
        odd = odd / pltpu.roll(odd, HEAD_DIM, 1)
        o_ref[:, pi * LANES:(pi + 1) * LANES] = jnp.where(lane < HEAD_DIM, even, odd).astype(o_ref.dtype)


def _flash(q, k, ve, vo, batch, seq, tq=1024, tk=512):
    t = batch * seq
    nq = seq // tq
    gw = C_QW // GA_KV_HEADS
    nheads = GA_Q_HEADS // GA_KV_HEADS
    kv_spec = pl.BlockSpec((seq, LANES), lambda b, g, i: (b, g))
    return pl.pallas_call(
        functools.partial(_flash_kernel, tk=tk),
        grid=(batch, GA_KV_HEADS, nq),
        in_specs=[pl.BlockSpec((tq, gw), lambda b, g, i: (b * nq + i, g)), kv_spec, kv_spec, kv_spec],
        out_specs=pl.BlockSpec((tq, gw), lambda b, g, i: (b * nq + i, g)),
        out_shape=jax.ShapeDtypeStruct((t, C_QW), BF16),
        scratch_shapes=[pltpu.VMEM((nheads, tq, LANES), BF16), pltpu.VMEM((nheads, tq, LANES), F32),
                        pltpu.VMEM((nheads, tq, LANES), F32)],
        compiler_params=pltpu.CompilerParams(
            dimension_semantics=("parallel", "parallel", "arbitrary"), vmem_limit_bytes=V7X_VMEM_LIMIT),
        name="flash",
    )(q, k, ve, vo)


def _merge_kernel(x_ref, ya_ref, ob0_ref, ob1_ref, ob2_ref, ls0_ref, ls1_ref, ls2_ref, yc_ref, yd_ref,
                  ga_ref, gb_ref, gc_ref, gd_ref, wa_ref, wb_ref, wc_ref, wd_ref, wo_ref, gn_ref,
                  xo_ref, ho_ref):
    l0, l1, l2 = ls0_ref[...], ls1_ref[...], ls2_ref[...]
    mx = jnp.maximum(jnp.maximum(l0, l1), l2)
    e0, e1, e2 = jnp.exp(l0 - mx), jnp.exp(l1 - mx), jnp.exp(l2 - mx)
    yb = (e0 * ob0_ref[...] + e1 * ob1_ref[...] + e2 * ob2_ref[...]) / (e0 + e1 + e2)

    def branch(g_ref, y, w_ref):
        return g_ref[...].astype(F32) * jnp.dot(y, w_ref[...], preferred_element_type=F32)

    merged = (branch(ga_ref, ya_ref[...], wa_ref) + branch(gb_ref, yb.astype(BF16), wb_ref)
              + branch(gc_ref, yc_ref[...], wc_ref) + branch(gd_ref, yd_ref[...], wd_ref))
    out = x_ref[...] + jnp.dot(merged.astype(BF16), wo_ref[...], preferred_element_type=F32)
    xo_ref[...] = out
    ho_ref[...] = _rms_rows(out, gn_ref[...]).astype(ho_ref.dtype)


def _merge(x, ya, obs, lss, yc, yd, gates, wa, wb, wc, wd, wo, gain, tm=256):
    t, d = x.shape

    def rows(width, col=0):
        return pl.BlockSpec((tm, width), lambda i: (i, col))

    def const(arr):
        return pl.BlockSpec(arr.shape, lambda i: (0, 0), pipeline_mode=pl.Buffered(1))

    gain = gain.reshape(1, d)
    return pl.pallas_call(
        _merge_kernel,
        grid=(t // tm,),
        in_specs=[rows(d), rows(A_W), rows(B_OUT), rows(B_OUT), rows(B_OUT), rows(B_OUT), rows(B_OUT), rows(B_OUT),
                  rows(C_QW), rows(D_QW), rows(d, 0), rows(d, 1), rows(d, 2), rows(d, 3),
                  const(wa), const(wb), const(wc), const(wd), const(wo), const(gain)],
        out_specs=[rows(d), rows(d)],
        out_shape=[jax.ShapeDtypeStruct((t, d), F32), jax.ShapeDtypeStruct((t, d), BF16)],
        compiler_params=pltpu.CompilerParams(dimension_semantics=("parallel",), vmem_limit_bytes=V7X_VMEM_LIMIT),
        name="merge",
    )(x, ya, *obs, *lss, yc, yd, gates, gates, gates, gates, wa, wb, wc, wd, wo, gain)


def _mlp_kernel(h_ref, x_ref, wu_ref, wd_ref, gn_ref, *rest, final):
    if final:
        yo_ref, acc_ref = rest
    else:
        xo_ref, ho_ref, acc_ref = rest
    f = pl.program_id(1)
    u = jnp.dot(h_ref[...], wu_ref[...], preferred_element_type=F32)
    a = jnp.square(jnp.maximum(u, 0.0)).astype(BF16)
    part = jnp.dot(a, wd_ref[...], preferred_element_type=F32)

    @pl.when(f == 0)
    def _():
        acc_ref[...] = x_ref[...] + part

    @pl.when(f > 0)
    def _():
        acc_ref[...] += part

    @pl.when(f == pl.num_programs(1) - 1)
    def _():
        out = acc_ref[...]
        normed = _rms_rows(out, gn_ref[...])
        if final:
            yo_ref[...] = normed
        else:
            xo_ref[...] = out
            ho_ref[...] = normed.astype(ho_ref.dtype)


def _mlp(h, x, wu, wd, gain, *, final, tm=512, tf=1024):
    t, d = x.shape
    ff = wu.shape[1]
    rows = pl.BlockSpec((tm, d), lambda i, f: (i, 0))
    if final:
        out_specs = [rows]
        out_shape = [jax.ShapeDtypeStruct((t, d), F32)]
    else:
        out_specs = [rows, rows]
        out_shape = [jax.ShapeDtypeStruct((t, d), F32), jax.ShapeDtypeStruct((t, d), BF16)]
    return pl.pallas_call(
        functools.partial(_mlp_kernel, final=final),
        grid=(t // tm, ff // tf),
        in_specs=[rows, rows,
                  pl.BlockSpec((d, tf), lambda i, f: (0, f)),
                  pl.BlockSpec((tf, d), lambda i, f: (f, 0)),
                  pl.BlockSpec((1, d), lambda i, f: (0, 0))],
        out_specs=out_specs,
        out_shape=out_shape,
        scratch_shapes=[pltpu.VMEM((tm, d), F32)],
        compiler_params=pltpu.CompilerParams(
            dimension_semantics=("parallel", "arbitrary"), vmem_limit_bytes=V7X_VMEM_LIMIT),
        name="mlp_final" if final else "mlp",
    )(h, x, wu, wd, gain.reshape(1, d))


def _prep_in_weights(w_in):
    offs = np.cumsum((0,) + SPLIT_SIZES)
    aq, ak, av, bq, bk, bv, cq, ck, cv, dq, dk, dv, gl = [w_in[..., offs[s]:offs[s + 1]] for s in range(13)]
    scale = HEAD_DIM ** -0.5

    def dup(t):
        k0, k1 = t[..., :HEAD_DIM], t[..., HEAD_DIM:]
        return jnp.concatenate([k0, k0, k1, k1], axis=-1)

    b_groups = []
    for g in range(len(DIL_PATTERNS)):
        sl = slice(g * B_OUT, (g + 1) * B_OUT)
        b_groups += [bq[..., sl] * scale, bk[..., sl], bv[..., sl]]
    pad = jnp.zeros(w_in.shape[:-1] + (P_WIDTH - COL_DV - 2 * D_KVW,), w_in.dtype)
    w_mix = jnp.concatenate([aq * scale, ak, av, cq, dq * scale] + b_groups
                            + [dup(ck), dup(cv), dup(dk), dup(dv), pad], axis=-1)
    return w_mix.astype(BF16), gl.astype(BF16)


def _trunk(x, wts, tables):
    batch, seq, d = x.shape
    t = batch * seq
    depth = wts["w_mix"].shape[0]
    cos, sin = _rope_tables(seq)
    x2 = x.reshape(t, d)
    h = _rmsnorm(x2, wts["norm_mix"][0])
    b_slopes = _alibi_slopes(DIL_HEADS)
    d_table = _band_table(_alibi_slopes(SW_Q_HEADS), 1, SW_HALF_WINDOW, SW_KV_HEADS)
    for l in range(depth):
        p = _proj(h, wts["w_mix"][l], sigmoid=False, name="proj_mix")
        gates = _proj(h, wts["w_gate"][l], sigmoid=True, name="proj_gate")
        ya = _natten(p, tables["natten"][l], batch, seq)
        obs, lss = [], []
        hg = DIL_HEADS_PER_GROUP
        p3 = p.reshape(batch, seq, P_WIDTH)
        for g, (win, dil) in enumerate(DIL_PATTERNS):
            col = COL_B + 3 * g * B_OUT
            if dil == 1:
                arr = p3
            else:
                arr, col = _fold(p[:, col:col + 3 * B_OUT], batch, dil), 0
            ob, ls = _banded(arr, w=win // (2 * dil), col_q=col, col_k=col + B_OUT, col_v=col + 2 * B_OUT,
                             wq=B_OUT, wkv=B_OUT, kvmap=(0, 1),
                             table=_band_table(b_slopes[g * hg:(g + 1) * hg], dil, win // (2 * dil), 2),
                             sink=None, out_dtype=F32, has_lse=True, name="banded_d%d" % dil)
            obs.append(_unfold(ob, batch, dil))
            lss.append(_unfold(ls, batch, dil))
        qr, kr, ve, vo = _cprep(p, cos, sin, wts["q_gain_c"][l], wts["k_gain_c"][l], batch, seq)
        yc = _flash(qr, kr, ve, vo, batch, seq)
        (yd,) = _banded(p3, w=SW_HALF_WINDOW, col_q=COL_DQ, col_k=COL_DK, col_v=COL_DV, wq=D_QW, wkv=2 * D_KVW,
                        kvmap=(0, 1), table=d_table, sink=wts["sink_d"][l], out_dtype=BF16, has_lse=False,
                        name="banded_sink")
        yd = yd.reshape(t, D_QW)
        x2, h = _merge(x2, ya, obs, lss, yc, yd, gates, wts["w_a"][l], wts["w_b"][l], wts["w_c"][l],
                       wts["w_d"][l], wts["w_out"][l], wts["norm_mlp"][l])
        if l + 1 < depth:
            x2, h = _mlp(h, x2, wts["w_up"][l], wts["w_down"][l], wts["norm_mix"][l + 1], final=False)
        else:
            (y,) = _mlp(h, x2, wts["w_up"][l], wts["w_down"][l], wts["norm_final"], final=True)
    return y.reshape(batch, seq, d)


def kernel(x_prompt, x_sample, norm_mix, w_in, rel_bias_a, q_gain_c, k_gain_c, sink_d, w_branch_a, w_branch_b,
           w_branch_c, w_branch_d, w_out, norm_mlp, w_up, w_down, norm_final):
    w_mix, w_gate = _prep_in_weights(w_in)
    wts = dict(norm_mix=norm_mix, w_mix=w_mix, w_gate=w_gate, q_gain_c=q_gain_c, k_gain_c=k_gain_c, sink_d=sink_d,
               w_a=w_branch_a.astype(BF16), w_b=w_branch_b.astype(BF16), w_c=w_branch_c.astype(BF16),
               w_d=w_branch_d.astype(BF16), w_out=w_out.astype(BF16), norm_mlp=norm_mlp,
               w_up=w_up.astype(BF16), w_down=w_down.astype(BF16), norm_final=norm_final)
    tables = dict(natten=jax.vmap(_natten_bias_table)(rel_bias_a))
    return (_trunk(x_prompt, wts, tables), _trunk(x_sample, wts, tables))
```

```python
import functools

import numpy as np
import jax
import jax.numpy as jnp
from jax import lax
from jax.experimental import pallas as pl
from jax.experimental.pallas import tpu as pltpu

F32 = jnp.float32
BF16 = jnp.bfloat16

D_MODEL = 2048
HEAD_DIM = 64
GRID_W = 64
NA_HEADS = 8
NA_ROWS = 8
NA_COLS = 16
DIL_PATTERNS = ((128, 1), (512, 4), (2048, 16))
DIL_HEADS_PER_GROUP = 4
DIL_HEADS = 12
DIL_BLOCK = 64
GA_Q_HEADS = 8
GA_KV_HEADS = 2
ROPE_THETA = 10000.0
SW_Q_HEADS = 8
SW_KV_HEADS = 2
SW_HALF_WINDOW = 128
N_BRANCHES = 4
D_FF = 4 * D_MODEL
RMS_EPS = 1e-6
NEG_INF = -1e30

A_W = NA_HEADS * HEAD_DIM
B_W = DIL_HEADS * HEAD_DIM
B_OUT = DIL_HEADS_PER_GROUP * HEAD_DIM
C_QW = GA_Q_HEADS * HEAD_DIM
C_KVW = GA_KV_HEADS * HEAD_DIM
D_QW = SW_Q_HEADS * HEAD_DIM
D_KVW = SW_KV_HEADS * HEAD_DIM
GATE_W = N_BRANCHES * D_MODEL
SPLIT_SIZES = (A_W, A_W, A_W, B_W, B_W, B_W, C_QW, C_KVW, C_KVW, D_QW, D_KVW, D_KVW, GATE_W)

LANES = 128
V7X_VMEM_LIMIT = 56 * 1024 * 1024

COL_AQ = 0
COL_AK = 512
COL_AV = 1024
COL_CQ = 1536
COL_DQ = 2048
COL_B0 = 2560
COL_CK = 3328
COL_CV = 3584
COL_DK = 3840
COL_DV = 4096
COL_BD = 4352
P_WIDTH = 6144
PROJ_TN = 2048
BD_TILE = COL_BD // PROJ_TN
BD_LOCAL = COL_BD % PROJ_TN
BD_WIDTH = 3 * B_OUT
DILATED = tuple(d for _, d in DIL_PATTERNS if d > 1)


def _alibi_slopes(n):
    return (2.0 ** (-8.0 * np.arange(1, n + 1) / n)).astype(np.float32)


def _half_masks():
    lane = lax.broadcasted_iota(jnp.int32, (1, LANES), 1)
    lo = jnp.where(lane < HEAD_DIM, 1.0, 0.0).astype(BF16)
    hi = jnp.where(lane >= HEAD_DIM, 1.0, 0.0).astype(BF16)
    return lane, (lo, hi)


def _dot_nt(a, b):
    return lax.dot_general(a, b, (((1,), (1,)), ((), ())), preferred_element_type=F32)


def _rms_rows(x, gain):
    ms = jnp.mean(x * x, axis=-1, keepdims=True)
    return x * lax.rsqrt(ms + RMS_EPS) * gain


def _rmsnorm_kernel(x_ref, g_ref, o_ref):
    o_ref[...] = _rms_rows(x_ref[...], g_ref[...]).astype(o_ref.dtype)


def _rmsnorm(x, gain, tm=512):
    t, d = x.shape
    return pl.pallas_call(
        _rmsnorm_kernel,
        grid=(t // tm,),
        in_specs=[pl.BlockSpec((tm, d), lambda i: (i, 0)), pl.BlockSpec((1, d), lambda i: (0, 0))],
        out_specs=pl.BlockSpec((tm, d), lambda i: (i, 0)),
        out_shape=jax.ShapeDtypeStruct((t, d), BF16),
        compiler_params=pltpu.CompilerParams(dimension_semantics=("parallel",), vmem_limit_bytes=V7X_VMEM_LIMIT),
        name="rmsnorm",
    )(x, gain.reshape(1, d))


def _proj_kernel(h_ref, w_ref, o_ref, *, row_chunk, sigmoid):
    for r in range(h_ref.shape[0] // row_chunk):
        rows = pl.ds(r * row_chunk, row_chunk)
        acc = jnp.dot(h_ref[rows, :], w_ref[...], preferred_element_type=F32)
        if sigmoid:
            acc = 1.0 / (1.0 + jnp.exp(-acc))
        o_ref[rows, :] = acc.astype(o_ref.dtype)


def _proj(h, w, *, sigmoid, tm=1024, tn=PROJ_TN, name):
    t, k = h.shape
    n = w.shape[1]
    return pl.pallas_call(
        functools.partial(_proj_kernel, row_chunk=256, sigmoid=sigmoid),
        grid=(t // tm, n // tn),
        in_specs=[pl.BlockSpec((tm, k), lambda i, j: (i, 0)), pl.BlockSpec((k, tn), lambda i, j: (0, j))],
        out_specs=pl.BlockSpec((tm, tn), lambda i, j: (i, j)),
        out_shape=jax.ShapeDtypeStruct((t, n), BF16),
        compiler_params=pltpu.CompilerParams(
            dimension_semantics=("parallel", "arbitrary"), vmem_limit_bytes=V7X_VMEM_LIMIT),
        name=name,
    )(h, w)


def _proj_mix_kernel(h_ref, w_ref, o_ref, *rest, row_chunk):
    fold_refs, stage = rest[:-1], rest[-1]
    j = pl.program_id(1)
    tm = h_ref.shape[0]
    for r in range(tm // row_chunk):
        rows = pl.ds(r * row_chunk, row_chunk)
        acc = jnp.dot(h_ref[rows, :], w_ref[...], preferred_element_type=F32)
        o_ref[rows, :] = acc.astype(o_ref.dtype)

        @pl.when(j == BD_TILE)
        def _():
            for s in range(stage.shape[0]):
                stage[s, rows, :] = acc[:, BD_LOCAL + s * LANES:BD_LOCAL + (s + 1) * LANES]

    @pl.when(j == BD_TILE)
    def _():
        for g, (f_ref, dil) in enumerate(zip(fold_refs, DILATED)):
            for res in range(dil):
                for s in range(BD_WIDTH // LANES):
                    slab = stage[g * (BD_WIDTH // LANES) + s, pl.ds(res, tm // dil, stride=dil), :]
                    f_ref[res, :, s * LANES:(s + 1) * LANES] = slab.astype(f_ref.dtype)


def _proj_mix(h, w, batch, seq, tm=1024):
    t, k = h.shape
    nb = seq // tm
    fold_specs = [pl.BlockSpec((None, dil, tm // dil, BD_WIDTH), lambda i, j: (i // nb, 0, i % nb, 0))
                  for dil in DILATED]
    fold_shapes = [jax.ShapeDtypeStruct((batch, dil, seq // dil, BD_WIDTH), BF16) for dil in DILATED]
    return pl.pallas_call(
        functools.partial(_proj_mix_kernel, row_chunk=256),
        grid=(t // tm, P_WIDTH // PROJ_TN),
        in_specs=[pl.BlockSpec((tm, k), lambda i, j: (i, 0)), pl.BlockSpec((k, PROJ_TN), lambda i, j: (0, j))],
        out_specs=[pl.BlockSpec((tm, PROJ_TN), lambda i, j: (i, j))] + fold_specs,
        out_shape=[jax.ShapeDtypeStruct((t, P_WIDTH), BF16)] + fold_shapes,
        scratch_shapes=[pltpu.VMEM((len(DILATED) * BD_WIDTH // LANES, tm, LANES), F32)],
        compiler_params=pltpu.CompilerParams(
            dimension_semantics=("parallel", "arbitrary"), vmem_limit_bytes=V7X_VMEM_LIMIT),
        name="proj_mix",
    )(h, w)


NA_STEP_ROWS = 8
NA_STEP_TOK = NA_STEP_ROWS * GRID_W
NA_KEYS = NA_ROWS * GRID_W


def _natten_bias_table(rel_bias):
    qc = np.arange(GRID_W)[:, None]
    kc = np.arange(GRID_W)[None, :]
    c0 = np.clip(qc - NA_COLS // 2, 0, GRID_W - NA_COLS)
    valid = (kc >= c0) & (kc < c0 + NA_COLS)
    dc = np.clip(kc - qc, -(NA_COLS - 1), NA_COLS - 1) + NA_COLS - 1
    shift = np.arange(NA_ROWS)[:, None]
    krow = np.arange(NA_ROWS)[None, :]
    dr = krow - shift + NA_ROWS - 1
    tbl = rel_bias[:, dr][:, :, :, dc]
    tbl = jnp.where(valid[None, None, None], tbl, NEG_INF)
    tbl = jnp.transpose(tbl, (1, 0, 3, 2, 4))
    return tbl.reshape(NA_ROWS, NA_HEADS // 2, 2 * GRID_W, NA_KEYS).astype(F32)


def _natten_kernel(q_ref, kp_ref, kc_ref, kn_ref, vp_ref, vc_ref, vn_ref, tbl_ref, o_ref, qs, kbuf, vbuf, *, n_rows):
    i = pl.program_id(1)
    for s, (kr, vr) in enumerate(((kp_ref, vp_ref), (kc_ref, vc_ref), (kn_ref, vn_ref))):
        kbuf[s * NA_STEP_TOK:(s + 1) * NA_STEP_TOK, :] = kr[...]
        vbuf[s * NA_STEP_TOK:(s + 1) * NA_STEP_TOK, :] = vr[...]
    lane, hm = _half_masks()
    for j in range(NA_STEP_ROWS):
        for pi in range(NA_HEADS // 2):
            qpair = q_ref[j * GRID_W:(j + 1) * GRID_W, pi * LANES:(pi + 1) * LANES]
            for e in range(2):
                qs[j, pi, e * GRID_W:(e + 1) * GRID_W, :] = qpair * hm[e]

    shifts, offs = [], []
    for j in range(NA_STEP_ROWS):
        r = i * NA_STEP_ROWS + j
        r0 = jnp.clip(r - NA_ROWS // 2, 0, n_rows - NA_ROWS)
        shifts.append(r - r0)
        offs.append(pl.multiple_of((r0 - (i - 1) * NA_STEP_ROWS) * GRID_W, GRID_W))
    for pi in range(NA_HEADS // 2):
        cols = slice(pi * LANES, (pi + 1) * LANES)
        s = jnp.concatenate([_dot_nt(qs[j, pi], kbuf[pl.ds(offs[j], NA_KEYS), cols]) + tbl_ref[shifts[j], pi]
                             for j in range(NA_STEP_ROWS)], axis=0)
        m = jnp.broadcast_to(jnp.max(s, axis=-1, keepdims=True), (s.shape[0], LANES))
        p = jnp.concatenate([jnp.exp(s[:, k * LANES:(k + 1) * LANES] - m) for k in range(NA_KEYS // LANES)], axis=1)
        inv = 1.0 / jnp.broadcast_to(jnp.sum(p, axis=-1, keepdims=True), (s.shape[0], LANES))
        p = p.astype(BF16)
        for j in range(NA_STEP_ROWS):
            rows = slice(j * 2 * GRID_W, (j + 1) * 2 * GRID_W)
            o = jnp.dot(p[rows], vbuf[pl.ds(offs[j], NA_KEYS), cols], preferred_element_type=F32) * inv[rows]
            o_ref[j * GRID_W:(j + 1) * GRID_W, cols] = jnp.where(lane < HEAD_DIM, o[:GRID_W], o[GRID_W:]).astype(
                o_ref.dtype)


def _natten(p, tbl, batch, seq):
    t = batch * seq
    nb = seq // NA_STEP_TOK
    tok = NA_STEP_TOK

    def spec(col, which):
        cb = col // A_W
        if which == 0:
            return pl.BlockSpec((tok, A_W), lambda b, i: (b * nb + i, cb))
        if which < 0:
            return pl.BlockSpec((tok, A_W), lambda b, i: (b * nb + jnp.maximum(i - 1, 0), cb))
        return pl.BlockSpec((tok, A_W), lambda b, i: (b * nb + jnp.minimum(i + 1, nb - 1), cb))

    return pl.pallas_call(
        functools.partial(_natten_kernel, n_rows=seq // GRID_W),
        grid=(batch, nb),
        in_specs=[spec(COL_AQ, 0),
                  spec(COL_AK, -1), spec(COL_AK, 0), spec(COL_AK, 1),
                  spec(COL_AV, -1), spec(COL_AV, 0), spec(COL_AV, 1),
                  pl.BlockSpec(tbl.shape, lambda b, i: (0, 0, 0, 0))],
        out_specs=pl.BlockSpec((tok, A_W), lambda b, i: (b * nb + i, 0)),
        out_shape=jax.ShapeDtypeStruct((t, A_W), BF16),
        scratch_shapes=[pltpu.VMEM((NA_STEP_ROWS, NA_HEADS // 2, 2 * GRID_W, LANES), BF16),
                        pltpu.VMEM((3 * tok, A_W), BF16), pltpu.VMEM((3 * tok, A_W), BF16)],
        compiler_params=pltpu.CompilerParams(
            dimension_semantics=("parallel", "arbitrary"), vmem_limit_bytes=V7X_VMEM_LIMIT),
        name="natten",
    )(p, p, p, p, p, p, p, tbl)


BAND_SUB = 128


def _band_table(slopes, dist_scale, w, chains):
    delta = np.arange(BAND_SUB + 2 * w)[None, :] - w - np.arange(BAND_SUB)[:, None]
    bias = -slopes[:, None, None] * (dist_scale * np.abs(delta)).astype(np.float32)[None]
    tbl = np.where(np.abs(delta)[None] <= w, bias, NEG_INF).astype(np.float32)
    return tbl.reshape(chains, -1, BAND_SUB + 2 * w)


def _banded_kernel(*refs, w, tq, n, kvmap, has_sink, has_lse):
    refs = list(refs)
    sink_ref = refs.pop(0) if has_sink else None
    q_ref, kp_ref, kc_ref, kn_ref, vp_ref, vc_ref, vn_ref, tbl_ref, o_ref = refs[:9]
    lse_ref = refs[9] if has_lse else None
    qs, kbuf, vbuf = refs[-3:]
    sub = BAND_SUB
    nchains = len(kvmap)
    blocks = q_ref.shape[1] // LANES // nchains
    i = pl.program_id(1)
    kbuf[0:w] = kp_ref[...]
    kbuf[w:w + tq] = kc_ref[...]
    kbuf[w + tq:] = kn_ref[...]
    vbuf[0:w] = vp_ref[...]
    vbuf[w:w + tq] = vc_ref[...]
    vbuf[w + tq:] = vn_ref[...]
    lane, hm = _half_masks()
    for j in range(tq // sub):
        for pi in range(nchains * blocks):
            qpair = q_ref[j * sub:(j + 1) * sub, pi * LANES:(pi + 1) * LANES]
            for e in range(2):
                qs[j, (2 * pi + e) * sub:(2 * pi + e + 1) * sub, :] = qpair * hm[e]
    nsub = tq // sub
    rows = 2 * blocks * sub
    nkeys = sub + 2 * w
    range_bias = []
    for j in range(nsub):
        kpos = i * tq + j * sub - w + lax.broadcasted_iota(jnp.int32, (1, nkeys), 1)
        range_bias.append(jnp.where(kpos < 0, NEG_INF, jnp.where(kpos >= n, NEG_INF, 0.0)))
    for c, kv in enumerate(kvmap):
        kvcols = slice(kv * LANES, (kv + 1) * LANES)
        tbl = tbl_ref[c]
        s = jnp.concatenate([_dot_nt(qs[j, c * rows:(c + 1) * rows, :], kbuf[j * sub:j * sub + nkeys, kvcols])
                             + tbl + range_bias[j] for j in range(nsub)], axis=0)
        m = jnp.max(s, axis=-1, keepdims=True)
        if has_sink:
            sink = jnp.concatenate([jnp.full((sub, LANES), sink_ref[c * 2 * blocks + h], F32)
                                    for h in range(2 * blocks)] * nsub, axis=0)
            m = jnp.maximum(m, sink)
        else:
            m = jnp.broadcast_to(m, (m.shape[0], LANES))
        p = jnp.concatenate([jnp.exp(s[:, k * LANES:(k + 1) * LANES] - m) for k in range(nkeys // LANES)], axis=1)
        l = jnp.sum(p, axis=-1, keepdims=True)
        if has_sink:
            l = l + jnp.exp(sink - m)
        else:
            l = jnp.broadcast_to(l, (l.shape[0], LANES))
        inv = 1.0 / l
        lse = m + jnp.log(l)
        p = p.astype(BF16)
        for j in range(nsub):
            jr = slice(j * rows, (j + 1) * rows)
            o = jnp.dot(p[jr], vbuf[j * sub:j * sub + nkeys, kvcols], preferred_element_type=F32) * inv[jr]
            for x in range(blocks):
                cols = slice((c * blocks + x) * LANES, (c * blocks + x + 1) * LANES)
                even, odd = slice(2 * x * sub, (2 * x + 1) * sub), slice((2 * x + 1) * sub, (2 * x + 2) * sub)
                o_ref[j * sub:(j + 1) * sub, cols] = jnp.where(lane < HEAD_DIM, o[even], o[odd]).astype(o_ref.dtype)
                if has_lse:
                    lse_j = lse[jr]
                    lse_ref[j * sub:(j + 1) * sub, cols] = jnp.where(lane < HEAD_DIM, lse_j[even], lse_j[odd])


def _banded(arr, *, w, col_q, col_k, col_v, wq, wkv, kvmap, table, sink, out_dtype, has_lse, name):
    classes, n, _ = arr.shape
    tq = min(512, n)
    sub = tq // w
    nblk = n // w

    def qspec(col, width):
        return pl.BlockSpec((None, tq, width), lambda c, i: (c, i, col // width))

    def prev_spec(col, width):
        return pl.BlockSpec((None, w, width), lambda c, i: (c, jnp.maximum(i * sub - 1, 0), col // width))

    def next_spec(col, width):
        return pl.BlockSpec((None, w, width), lambda c, i: (c, jnp.minimum((i + 1) * sub, nblk - 1), col // width))

    in_specs = [qspec(col_q, wq),
                prev_spec(col_k, wkv), qspec(col_k, wkv), next_spec(col_k, wkv),
                prev_spec(col_v, wkv), qspec(col_v, wkv), next_spec(col_v, wkv),
                pl.BlockSpec(table.shape, lambda c, i: (0, 0, 0))]
    args = [arr] * 7 + [jnp.asarray(table)]
    if sink is not None:
        in_specs = [pl.BlockSpec(memory_space=pltpu.SMEM)] + in_specs
        args = [sink.astype(F32)] + args
    out_block = pl.BlockSpec((None, tq, wq), lambda c, i: (c, i, 0))
    out_shape = [jax.ShapeDtypeStruct((classes, n, wq), out_dtype)]
    out_specs = [out_block]
    if has_lse:
        out_shape.append(jax.ShapeDtypeStruct((classes, n, wq), F32))
        out_specs.append(out_block)
    return pl.pallas_call(
        functools.partial(_banded_kernel, w=w, tq=tq, n=n, kvmap=kvmap, has_sink=sink is not None, has_lse=has_lse),
        grid=(classes, n // tq),
        in_specs=in_specs,
        out_specs=out_specs,
        out_shape=out_shape,
        scratch_shapes=[pltpu.VMEM((tq // BAND_SUB, 2 * (wq // LANES) * BAND_SUB, LANES), BF16),
                        pltpu.VMEM((tq + 2 * w, wkv), BF16), pltpu.VMEM((tq + 2 * w, wkv), BF16)],
        compiler_params=pltpu.CompilerParams(
            dimension_semantics=("parallel", "arbitrary"), vmem_limit_bytes=V7X_VMEM_LIMIT),
        name=name,
    )(*args)


def _rope_tables(seq):
    t = jnp.arange(seq)
    npair = HEAD_DIM // 4
    freqs = ROPE_THETA ** (-jnp.arange(npair, dtype=F32) / npair)

    def cs(pos):
        ang = pos.astype(F32)[:, None] * freqs[None, :]
        return jnp.cos(ang), jnp.sin(ang)

    cr, sr = cs(t // GRID_W)
    cc, sc = cs(t % GRID_W)
    cos = jnp.concatenate([cr, cr, cc, cc], axis=-1)
    sin = jnp.concatenate([-sr, sr, -sc, sc], axis=-1)
    return jnp.tile(cos, (1, 2)), jnp.tile(sin, (1, 2))


def _cprep_kernel(q_ref, k_ref, v_ref, cos_ref, sin_ref, qg_ref, kg_ref, avg_ref, qo_ref, ko_ref, ve_ref, vo_ref):
    lane = lax.broadcasted_iota(jnp.int32, (1, LANES), 1)
    v = v_ref[...]
    ones = jnp.ones(v.shape, v.dtype)
    ve_ref[...] = jnp.where(lane < HEAD_DIM, v, ones)
    vo_ref[...] = jnp.where(lane < HEAD_DIM, ones, v)
    first = (lane & (HEAD_DIM // 4)) == 0
    cos = cos_ref[...]
    sin = sin_ref[...]
    avg = avg_ref[...]

    def norm_rope(x, gain):
        x = x.astype(F32)
        sq = x * x
        hi = sq.astype(BF16)
        r1 = sq - hi.astype(F32)
        mid = r1.astype(BF16)
        lo = (r1 - mid.astype(F32)).astype(BF16)
        ms = (jnp.dot(hi, avg, preferred_element_type=F32) + jnp.dot(mid, avg, preferred_element_type=F32)
              + jnp.dot(lo, avg, preferred_element_type=F32))
        y = x * lax.rsqrt(ms + RMS_EPS) * gain
        quarter = HEAD_DIM // 4
        partner = jnp.where(first, pltpu.roll(y, LANES - quarter, 1), pltpu.roll(y, quarter, 1))
        return y * cos + partner * sin

    for c in range(q_ref.shape[1] // LANES):
        cols = slice(c * LANES, (c + 1) * LANES)
        qo_ref[:, cols] = norm_rope(q_ref[:, cols], qg_ref[...]).astype(qo_ref.dtype)
    ko_ref[...] = norm_rope(k_ref[...], kg_ref[...]).astype(ko_ref.dtype)


def _cprep(p, cos, sin, q_gain, k_gain, batch, seq, tm=512):
    t = batch * seq
    nt = seq // tm
    avg = np.zeros((LANES, LANES), np.float32)
    avg[:HEAD_DIM, :HEAD_DIM] = 1.0 / HEAD_DIM
    avg[HEAD_DIM:, HEAD_DIM:] = 1.0 / HEAD_DIM
    gw = C_QW // GA_KV_HEADS
    qg = jnp.tile(q_gain.astype(F32) * (HEAD_DIM ** -0.5 * np.log2(np.e)), 2).reshape(1, LANES)
    kg = jnp.tile(k_gain.astype(F32), 2).reshape(1, LANES)
    return pl.pallas_call(
        _cprep_kernel,
        grid=(t // tm, GA_KV_HEADS),
        in_specs=[pl.BlockSpec((tm, gw), lambda i, g: (i, COL_CQ // gw + g)),
                  pl.BlockSpec((tm, LANES), lambda i, g: (i, COL_CK // LANES + g)),
                  pl.BlockSpec((tm, LANES), lambda i, g: (i, COL_CV // LANES + g)),
                  pl.BlockSpec((tm, LANES), lambda i, g: (i % nt, 0)),
                  pl.BlockSpec((tm, LANES), lambda i, g: (i % nt, 0)),
                  pl.BlockSpec((1, LANES), lambda i, g: (0, 0)),
                  pl.BlockSpec((1, LANES), lambda i, g: (0, 0)),
                  pl.BlockSpec((LANES, LANES), lambda i, g: (0, 0))],
        out_specs=[pl.BlockSpec((tm, gw), lambda i, g: (i, g))] + [pl.BlockSpec((tm, LANES), lambda i, g: (i, g))] * 3,
        out_shape=[jax.ShapeDtypeStruct((t, C_QW), BF16)] + [jax.ShapeDtypeStruct((t, GA_KV_HEADS * LANES), BF16)] * 3,
        compiler_params=pltpu.CompilerParams(
            dimension_semantics=("parallel", "arbitrary"), vmem_limit_bytes=V7X_VMEM_LIMIT),
        name="cprep",
    )(p, p, p, cos, sin, qg, kg, jnp.asarray(avg, BF16))


def _flash_kernel(q_ref, k_ref, ve_ref, vo_ref, o_ref, qs_ref, m_ref, acc_ref, *, tk):
    nkv = k_ref.shape[0] // tk
    nheads = qs_ref.shape[0]
    lane, hm = _half_masks()
    for h in range(nheads):
        qs_ref[h] = q_ref[:, (h // 2) * LANES:(h // 2 + 1) * LANES] * hm[h % 2]
    m_ref[...] = jnp.full(m_ref.shape, NEG_INF, F32)
    acc_ref[...] = jnp.zeros(acc_ref.shape, F32)

    def body(t, carry):
        rows = pl.ds(pl.multiple_of(t * tk, tk), tk)
        k = k_ref[rows, :]
        vs = (ve_ref[rows, :], vo_ref[rows, :])
        for h in range(nheads):
            s = _dot_nt(qs_ref[h], k)
            m_old = m_ref[h]
            m_new = jnp.maximum(m_old, jnp.max(s, axis=-1, keepdims=True))
            alpha = jnp.exp2(m_old - m_new)
            p = jnp.concatenate([jnp.exp2(s[:, c * LANES:(c + 1) * LANES] - m_new) for c in range(tk // LANES)],
                                axis=1)
            acc_ref[h] = alpha * acc_ref[h] + jnp.dot(p.astype(BF16), vs[h % 2], preferred_element_type=F32)
            m_ref[h] = m_new
        return carry

    lax.fori_loop(0, nkv, body, 0)
    for pi in range(nheads // 2):
        even, odd = acc_ref[2 * pi], acc_ref[2 * pi + 1]
        even = even / pltpu.roll(even, HEAD_DIM, 1)
        odd = odd / pltpu.roll(odd, HEAD_DIM, 1)
        o_ref[:, pi * LANES:(pi + 1) * LANES] = jnp.where(lane < HEAD_DIM, even, odd).astype(o_ref.dtype)


def _flash(q, k, ve, vo, batch, seq, tq=1024, tk=1024):
    t = batch * seq
    nq = seq // tq
    gw = C_QW // GA_KV_HEADS
    nheads = GA_Q_HEADS // GA_KV_HEADS
    kv_spec = pl.BlockSpec((seq, LANES), lambda b, g, i: (b, g))
    return pl.pallas_call(
        functools.partial(_flash_kernel, tk=tk),
        grid=(batch, GA_KV_HEADS, nq),
        in_specs=[pl.BlockSpec((tq, gw), lambda b, g, i: (b * nq + i, g)), kv_spec, kv_spec, kv_spec],
        out_specs=pl.BlockSpec((tq, gw), lambda b, g, i: (b * nq + i, g)),
        out_shape=jax.ShapeDtypeStruct((t, C_QW), BF16),
        scratch_shapes=[pltpu.VMEM((nheads, tq, LANES), BF16), pltpu.VMEM((nheads, tq, LANES), F32),
                        pltpu.VMEM((nheads, tq, LANES), F32)],
        compiler_params=pltpu.CompilerParams(
            dimension_semantics=("parallel", "parallel", "arbitrary"), vmem_limit_bytes=V7X_VMEM_LIMIT),
        name="flash",
    )(q, k, ve, vo)


def _merge_kernel(x_ref, ya_ref, ob0_ref, ob1_ref, ob2_ref, ls0_ref, ls1_ref, ls2_ref, yc_ref, yd_ref,
                  ga_ref, gb_ref, gc_ref, gd_ref, wa_ref, wb_ref, wc_ref, wd_ref, wo_ref, gn_ref,
                  xo_ref, ho_ref, stage):
    tm = x_ref.shape[0]
    slots = iter(range(stage.shape[0]))

    def token_order(ref):
        dil = ref.shape[0]
        if dil == 1:
            return ref[0]
        slot = next(slots)
        for r in range(dil):
            for s in range(B_OUT // LANES):
                stage[slot, s, pl.ds(r, tm // dil, stride=dil), :] = ref[r, :, s * LANES:(s + 1) * LANES]
        return jnp.concatenate([stage[slot, s] for s in range(B_OUT // LANES)], axis=1)

    l0, l1, l2 = token_order(ls0_ref), token_order(ls1_ref), token_order(ls2_ref)
    mx = jnp.maximum(jnp.maximum(l0, l1), l2)
    e0, e1, e2 = jnp.exp(l0 - mx), jnp.exp(l1 - mx), jnp.exp(l2 - mx)
    yb = (e0 * token_order(ob0_ref) + e1 * token_order(ob1_ref) + e2 * token_order(ob2_ref)) / (e0 + e1 + e2)

    def branch(g_ref, y, w_ref):
        return g_ref[...].astype(F32) * jnp.dot(y, w_ref[...], preferred_element_type=F32)

    merged = (branch(ga_ref, ya_ref[...], wa_ref) + branch(gb_ref, yb.astype(BF16), wb_ref)
              + branch(gc_ref, yc_ref[...], wc_ref) + branch(gd_ref, yd_ref[...], wd_ref))
    out = x_ref[...] + jnp.dot(merged.astype(BF16), wo_ref[...], preferred_element_type=F32)
    xo_ref[...] = out
    ho_ref[...] = _rms_rows(out, gn_ref[...]).astype(ho_ref.dtype)


def _merge(x, ya, obs, lss, yc, yd, gates, wa, wb, wc, wd, wo, gain, seq, tm=256):
    t, d = x.shape
    nb = seq // tm

    def rows(width, col=0):
        return pl.BlockSpec((tm, width), lambda i: (i, col))

    def classes(arr):
        dil = arr.shape[1]
        return pl.BlockSpec((None, dil, tm // dil, B_OUT), lambda i: (i // nb, 0, i % nb, 0))

    def const(arr):
        return pl.BlockSpec(arr.shape, lambda i: (0, 0), pipeline_mode=pl.Buffered(1))

    gain = gain.reshape(1, d)
    return pl.pallas_call(
        _merge_kernel,
        grid=(t // tm,),
        in_specs=[rows(d), rows(A_W)] + [classes(a) for a in obs] + [classes(a) for a in lss]
                 + [rows(C_QW), rows(D_QW), rows(d, 0), rows(d, 1), rows(d, 2), rows(d, 3),
                    const(wa), const(wb), const(wc), const(wd), const(wo), const(gain)],
        out_specs=[rows(d), rows(d)],
        out_shape=[jax.ShapeDtypeStruct((t, d), F32), jax.ShapeDtypeStruct((t, d), BF16)],
        scratch_shapes=[pltpu.VMEM((2 * len(DILATED), B_OUT // LANES, tm, LANES), F32)],
        compiler_params=pltpu.CompilerParams(dimension_semantics=("parallel",), vmem_limit_bytes=V7X_VMEM_LIMIT),
        name="merge",
    )(x, ya, *obs, *lss, yc, yd, gates, gates, gates, gates, wa, wb, wc, wd, wo, gain)


def _mlp_kernel(h_ref, x_ref, wu_ref, wd_ref, gn_ref, *rest, final):
    if final:
        yo_ref, acc_ref = rest
    else:
        xo_ref, ho_ref, acc_ref = rest
    f = pl.program_id(1)
    u = jnp.dot(h_ref[...], wu_ref[...], preferred_element_type=F32)
    a = jnp.square(jnp.maximum(u, 0.0)).astype(BF16)
    part = jnp.dot(a, wd_ref[...], preferred_element_type=F32)

    @pl.when(f == 0)
    def _():
        acc_ref[...] = x_ref[...] + part

    @pl.when(f > 0)
    def _():
        acc_ref[...] += part

    @pl.when(f == pl.num_programs(1) - 1)
    def _():
        out = acc_ref[...]
        normed = _rms_rows(out, gn_ref[...])
        if final:
            yo_ref[...] = normed
        else:
            xo_ref[...] = out
            ho_ref[...] = normed.astype(ho_ref.dtype)


def _mlp(h, x, wu, wd, gain, *, final, tm=512, tf=1024):
    t, d = x.shape
    ff = wu.shape[1]
    rows = pl.BlockSpec((tm, d), lambda i, f: (i, 0))
    if final:
        out_specs = [rows]
        out_shape = [jax.ShapeDtypeStruct((t, d), F32)]
    else:
        out_specs = [rows, rows]
        out_shape = [jax.ShapeDtypeStruct((t, d), F32), jax.ShapeDtypeStruct((t, d), BF16)]
    return pl.pallas_call(
        functools.partial(_mlp_kernel, final=final),
        grid=(t // tm, ff // tf),
        in_specs=[rows, rows,
                  pl.BlockSpec((d, tf), lambda i, f: (0, f)),
                  pl.BlockSpec((tf, d), lambda i, f: (f, 0)),
                  pl.BlockSpec((1, d), lambda i, f: (0, 0))],
        out_specs=out_specs,
        out_shape=out_shape,
        scratch_shapes=[pltpu.VMEM((tm, d), F32)],
        compiler_params=pltpu.CompilerParams(
            dimension_semantics=("parallel", "arbitrary"), vmem_limit_bytes=V7X_VMEM_LIMIT),
        name="mlp_final" if final else "mlp",
    )(h, x, wu, wd, gain.reshape(1, d))


def _prep_in_weights(w_in):
    offs = np.cumsum((0,) + SPLIT_SIZES)
    aq, ak, av, bq, bk, bv, cq, ck, cv, dq, dk, dv, gl = [w_in[..., offs[s]:offs[s + 1]] for s in range(13)]
    scale = HEAD_DIM ** -0.5

    def dup(t):
        k0, k1 = t[..., :HEAD_DIM], t[..., HEAD_DIM:]
        return jnp.concatenate([k0, k0, k1, k1], axis=-1)

    b_groups = []
    for g in range(len(DIL_PATTERNS)):
        sl = slice(g * B_OUT, (g + 1) * B_OUT)
        b_groups += [bq[..., sl] * scale, bk[..., sl], bv[..., sl]]
    pad = jnp.zeros(w_in.shape[:-1] + (P_WIDTH - COL_BD - 2 * BD_WIDTH,), w_in.dtype)
    w_mix = jnp.concatenate([aq * scale, ak, av, cq, dq * scale] + b_groups[:3]
                            + [dup(ck), dup(cv), dup(dk), dup(dv)] + b_groups[3:] + [pad], axis=-1)
    return w_mix.astype(BF16), gl.astype(BF16)


def _trunk(x, wts, tables):
    batch, seq, d = x.shape
    t = batch * seq
    depth = wts["w_mix"].shape[0]
    cos, sin = _rope_tables(seq)
    x2 = x.reshape(t, d)
    h = _rmsnorm(x2, wts["norm_mix"][0])
    b_slopes = _alibi_slopes(DIL_HEADS)
    d_table = _band_table(_alibi_slopes(SW_Q_HEADS), 1, SW_HALF_WINDOW, SW_KV_HEADS)
    for l in range(depth):
        p, *folded = _proj_mix(h, wts["w_mix"][l], batch, seq)
        gates = _proj(h, wts["w_gate"][l], sigmoid=True, name="proj_gate")
        ya = _natten(p, tables["natten"][l], batch, seq)
        obs, lss = [], []
        hg = DIL_HEADS_PER_GROUP
        p3 = p.reshape(batch, seq, P_WIDTH)
        for g, (win, dil) in enumerate(DIL_PATTERNS):
            if dil == 1:
                arr, col = p3, COL_B0
            else:
                arr, col = folded[DILATED.index(dil)].reshape(batch * dil, seq // dil, BD_WIDTH), 0
            ob, ls = _banded(arr, w=win // (2 * dil), col_q=col, col_k=col + B_OUT, col_v=col + 2 * B_OUT,
                             wq=B_OUT, wkv=B_OUT, kvmap=(0, 1),
                             table=_band_table(b_slopes[g * hg:(g + 1) * hg], dil, win // (2 * dil), 2),
                             sink=None, out_dtype=F32, has_lse=True, name="banded_d%d" % dil)
            obs.append(ob.reshape(batch, dil, seq // dil, B_OUT))
            lss.append(ls.reshape(batch, dil, seq // dil, B_OUT))
        qr, kr, ve, vo = _cprep(p, cos, sin, wts["q_gain_c"][l], wts["k_gain_c"][l], batch, seq)
        yc = _flash(qr, kr, ve, vo, batch, seq)
        (yd,) = _banded(p3, w=SW_HALF_WINDOW, col_q=COL_DQ, col_k=COL_DK, col_v=COL_DV, wq=D_QW, wkv=2 * D_KVW,
                        kvmap=(0, 1), table=d_table, sink=wts["sink_d"][l], out_dtype=BF16, has_lse=False,
                        name="banded_sink")
        yd = yd.reshape(t, D_QW)
        x2, h = _merge(x2, ya, obs, lss, yc, yd, gates, wts["w_a"][l], wts["w_b"][l], wts["w_c"][l],
                       wts["w_d"][l], wts["w_out"][l], wts["norm_mlp"][l], seq)
        if l + 1 < depth:
            x2, h = _mlp(h, x2, wts["w_up"][l], wts["w_down"][l], wts["norm_mix"][l + 1], final=False)
        else:
            (y,) = _mlp(h, x2, wts["w_up"][l], wts["w_down"][l], wts["norm_final"], final=True)
    return y.reshape(batch, seq, d)


def kernel(x_prompt, x_sample, norm_mix, w_in, rel_bias_a, q_gain_c, k_gain_c, sink_d, w_branch_a, w_branch_b,
           w_branch_c, w_branch_d, w_out, norm_mlp, w_up, w_down, norm_final):
    w_mix, w_gate = _prep_in_weights(w_in)
    wts = dict(norm_mix=norm_mix, w_mix=w_mix, w_gate=w_gate, q_gain_c=q_gain_c, k_gain_c=k_gain_c, sink_d=sink_d,
               w_a=w_branch_a.astype(BF16), w_b=w_branch_b.astype(BF16), w_c=w_branch_c.astype(BF16),
               w_d=w_branch_d.astype(BF16), w_out=w_out.astype(BF16), norm_mlp=norm_mlp,
               w_up=w_up.astype(BF16), w_down=w_down.astype(BF16), norm_final=norm_final)
    tables = dict(natten=jax.vmap(_natten_bias_table)(rel_bias_a))
    return (_trunk(x_prompt, wts, tables), _trunk(x_sample, wts, tables))
```

```python
import functools

import numpy as np
import jax
import jax.numpy as jnp
from jax import lax
from jax.experimental import pallas as pl
from jax.experimental.pallas import tpu as pltpu

F32 = jnp.float32
BF16 = jnp.bfloat16

D_MODEL = 2048
HEAD_DIM = 64
GRID_W = 64
NA_HEADS = 8
NA_ROWS = 8
NA_COLS = 16
DIL_PATTERNS = ((128, 1), (512, 4), (2048, 16))
DIL_HEADS_PER_GROUP = 4
DIL_HEADS = 12
DIL_BLOCK = 64
GA_Q_HEADS = 8
GA_KV_HEADS = 2
ROPE_THETA = 10000.0
SW_Q_HEADS = 8
SW_KV_HEADS = 2
SW_HALF_WINDOW = 128
N_BRANCHES = 4
D_FF = 4 * D_MODEL
RMS_EPS = 1e-6
NEG_INF = -1e30

A_W = NA_HEADS * HEAD_DIM
B_W = DIL_HEADS * HEAD_DIM
B_OUT = DIL_HEADS_PER_GROUP * HEAD_DIM
C_QW = GA_Q_HEADS * HEAD_DIM
C_KVW = GA_KV_HEADS * HEAD_DIM
D_QW = SW_Q_HEADS * HEAD_DIM
D_KVW = SW_KV_HEADS * HEAD_DIM
GATE_W = N_BRANCHES * D_MODEL
SPLIT_SIZES = (A_W, A_W, A_W, B_W, B_W, B_W, C_QW, C_KVW, C_KVW, D_QW, D_KVW, D_KVW, GATE_W)

LANES = 128
V7X_VMEM_LIMIT = 56 * 1024 * 1024

COL_AQ = 0
COL_AK = 512
COL_AV = 1024
COL_CQ = 1536
COL_DQ = 2048
COL_B0 = 2560
COL_CK = 3328
COL_CV = 3584
COL_DK = 3840
COL_DV = 4096
COL_BD = 4352
P_WIDTH = 6144
PROJ_TN = 2048
BD_TILE = COL_BD // PROJ_TN
BD_LOCAL = COL_BD % PROJ_TN
BD_WIDTH = 3 * B_OUT
DILATED = tuple(d for _, d in DIL_PATTERNS if d > 1)


def _alibi_slopes(n):
    return (2.0 ** (-8.0 * np.arange(1, n + 1) / n)).astype(np.float32)


def _half_masks():
    lane = lax.broadcasted_iota(jnp.int32, (1, LANES), 1)
    lo = jnp.where(lane < HEAD_DIM, 1.0, 0.0).astype(BF16)
    hi = jnp.where(lane >= HEAD_DIM, 1.0, 0.0).astype(BF16)
    return lane, (lo, hi)


def _dot_nt(a, b):
    return lax.dot_general(a, b, (((1,), (1,)), ((), ())), preferred_element_type=F32)


def _rms_rows(x, gain):
    ms = jnp.mean(x * x, axis=-1, keepdims=True)
    return x * lax.rsqrt(ms + RMS_EPS) * gain


def _rmsnorm_kernel(x_ref, g_ref, o_ref):
    o_ref[...] = _rms_rows(x_ref[...], g_ref[...]).astype(o_ref.dtype)


def _rmsnorm(x, gain, tm=512):
    t, d = x.shape
    return pl.pallas_call(
        _rmsnorm_kernel,
        grid=(t // tm,),
        in_specs=[pl.BlockSpec((tm, d), lambda i: (i, 0)), pl.BlockSpec((1, d), lambda i: (0, 0))],
        out_specs=pl.BlockSpec((tm, d), lambda i: (i, 0)),
        out_shape=jax.ShapeDtypeStruct((t, d), BF16),
        compiler_params=pltpu.CompilerParams(dimension_semantics=("parallel",), vmem_limit_bytes=V7X_VMEM_LIMIT),
        name="rmsnorm",
    )(x, gain.reshape(1, d))


def _proj_kernel(h_ref, w_ref, o_ref, *, row_chunk, sigmoid):
    for r in range(h_ref.shape[0] // row_chunk):
        rows = pl.ds(r * row_chunk, row_chunk)
        acc = jnp.dot(h_ref[rows, :], w_ref[...], preferred_element_type=F32)
        if sigmoid:
            acc = 1.0 / (1.0 + jnp.exp(-acc))
        o_ref[rows, :] = acc.astype(o_ref.dtype)


def _proj(h, w, *, sigmoid, tm=1024, tn=PROJ_TN, name):
    t, k = h.shape
    n = w.shape[1]
    return pl.pallas_call(
        functools.partial(_proj_kernel, row_chunk=256, sigmoid=sigmoid),
        grid=(t // tm, n // tn),
        in_specs=[pl.BlockSpec((tm, k), lambda i, j: (i, 0)), pl.BlockSpec((k, tn), lambda i, j: (0, j))],
        out_specs=pl.BlockSpec((tm, tn), lambda i, j: (i, j)),
        out_shape=jax.ShapeDtypeStruct((t, n), BF16),
        compiler_params=pltpu.CompilerParams(
            dimension_semantics=("parallel", "arbitrary"), vmem_limit_bytes=V7X_VMEM_LIMIT),
        name=name,
    )(h, w)


def _proj_mix_kernel(h_ref, w_ref, o_ref, *rest, row_chunk):
    (f4_ref, f16_ref), (stage, stage4) = rest[:2], rest[2:]
    j = pl.program_id(1)
    tm = h_ref.shape[0]
    slabs = BD_WIDTH // LANES
    quarter = tm // 4
    for r in range(tm // row_chunk):
        rows = pl.ds(r * row_chunk, row_chunk)
        acc = jnp.dot(h_ref[rows, :], w_ref[...], preferred_element_type=F32)
        o_ref[rows, :] = acc.astype(o_ref.dtype)
        for s in range(2 * slabs):
            stage[s, rows, :] = acc[:, BD_LOCAL + s * LANES:BD_LOCAL + (s + 1) * LANES]

    @pl.when(j == BD_TILE)
    def _():
        for s in range(slabs):
            for r4 in range(4):
                f4_ref[r4, :, s * LANES:(s + 1) * LANES] = stage[s, pl.ds(r4, quarter, stride=4), :].astype(
                    f4_ref.dtype)
                stage4[s, r4 * quarter:(r4 + 1) * quarter, :] = stage[slabs + s, pl.ds(r4, quarter, stride=4), :]
        for s in range(slabs):
            for r4 in range(4):
                for r4b in range(4):
                    part = stage4[s, pl.ds(r4 * quarter + r4b, quarter // 4, stride=4), :]
                    f16_ref[4 * r4b + r4, :, s * LANES:(s + 1) * LANES] = part.astype(f16_ref.dtype)


def _proj_mix(h, w, batch, seq, tm=1024):
    assert DILATED == (4, 16), "the fold in _proj_mix_kernel is written for dilations 4 and 16"
    t, k = h.shape
    nb = seq // tm
    fold_specs = [pl.BlockSpec((None, dil, tm // dil, BD_WIDTH), lambda i, j: (i // nb, 0, i % nb, 0))
                  for dil in DILATED]
    fold_shapes = [jax.ShapeDtypeStruct((batch, dil, seq // dil, BD_WIDTH), BF16) for dil in DILATED]
    return pl.pallas_call(
        functools.partial(_proj_mix_kernel, row_chunk=256),
        grid=(t // tm, P_WIDTH // PROJ_TN),
        in_specs=[pl.BlockSpec((tm, k), lambda i, j: (i, 0)), pl.BlockSpec((k, PROJ_TN), lambda i, j: (0, j))],
        out_specs=[pl.BlockSpec((tm, PROJ_TN), lambda i, j: (i, j))] + fold_specs,
        out_shape=[jax.ShapeDtypeStruct((t, P_WIDTH), BF16)] + fold_shapes,
        scratch_shapes=[pltpu.VMEM((2 * BD_WIDTH // LANES, tm, LANES), F32),
                        pltpu.VMEM((BD_WIDTH // LANES, tm, LANES), F32)],
        compiler_params=pltpu.CompilerParams(
            dimension_semantics=("parallel", "arbitrary"), vmem_limit_bytes=V7X_VMEM_LIMIT),
        name="proj_mix",
    )(h, w)


NA_STEP_ROWS = 8
NA_STEP_TOK = NA_STEP_ROWS * GRID_W
NA_KEYS = NA_ROWS * GRID_W


def _natten_bias_table(rel_bias):
    qc = np.arange(GRID_W)[:, None]
    kc = np.arange(GRID_W)[None, :]
    c0 = np.clip(qc - NA_COLS // 2, 0, GRID_W - NA_COLS)
    valid = (kc >= c0) & (kc < c0 + NA_COLS)
    dc = np.clip(kc - qc, -(NA_COLS - 1), NA_COLS - 1) + NA_COLS - 1
    shift = np.arange(NA_ROWS)[:, None]
    krow = np.arange(NA_ROWS)[None, :]
    dr = krow - shift + NA_ROWS - 1
    tbl = rel_bias[:, dr][:, :, :, dc]
    tbl = jnp.where(valid[None, None, None], tbl, NEG_INF)
    tbl = jnp.transpose(tbl, (1, 0, 3, 2, 4))
    return tbl.reshape(NA_ROWS, NA_HEADS // 2, 2 * GRID_W, NA_KEYS).astype(F32)


def _natten_kernel(q_ref, kp_ref, kc_ref, kn_ref, vp_ref, vc_ref, vn_ref, tbl_ref, o_ref, qs, kbuf, vbuf, *, n_rows):
    i = pl.program_id(1)
    for s, (kr, vr) in enumerate(((kp_ref, vp_ref), (kc_ref, vc_ref), (kn_ref, vn_ref))):
        kbuf[s * NA_STEP_TOK:(s + 1) * NA_STEP_TOK, :] = kr[...]
        vbuf[s * NA_STEP_TOK:(s + 1) * NA_STEP_TOK, :] = vr[...]
    lane, hm = _half_masks()
    for j in range(NA_STEP_ROWS):
        for pi in range(NA_HEADS // 2):
            qpair = q_ref[j * GRID_W:(j + 1) * GRID_W, pi * LANES:(pi + 1) * LANES]
            for e in range(2):
                qs[j, pi, e * GRID_W:(e + 1) * GRID_W, :] = qpair * hm[e]

    shifts, offs = [], []
    for j in range(NA_STEP_ROWS):
        r = i * NA_STEP_ROWS + j
        r0 = jnp.clip(r - NA_ROWS // 2, 0, n_rows - NA_ROWS)
        shifts.append(r - r0)
        offs.append(pl.multiple_of((r0 - (i - 1) * NA_STEP_ROWS) * GRID_W, GRID_W))
    for pi in range(NA_HEADS // 2):
        cols = slice(pi * LANES, (pi + 1) * LANES)
        s = jnp.concatenate([_dot_nt(qs[j, pi], kbuf[pl.ds(offs[j], NA_KEYS), cols]) + tbl_ref[shifts[j], pi]
                             for j in range(NA_STEP_ROWS)], axis=0)
        m = jnp.broadcast_to(jnp.max(s, axis=-1, keepdims=True), (s.shape[0], LANES))
        p = jnp.concatenate([jnp.exp(s[:, k * LANES:(k + 1) * LANES] - m) for k in range(NA_KEYS // LANES)], axis=1)
        inv = 1.0 / jnp.broadcast_to(jnp.sum(p, axis=-1, keepdims=True), (s.shape[0], LANES))
        p = p.astype(BF16)
        for j in range(NA_STEP_ROWS):
            rows = slice(j * 2 * GRID_W, (j + 1) * 2 * GRID_W)
            o = jnp.dot(p[rows], vbuf[pl.ds(offs[j], NA_KEYS), cols], preferred_element_type=F32) * inv[rows]
            o_ref[j * GRID_W:(j + 1) * GRID_W, cols] = jnp.where(lane < HEAD_DIM, o[:GRID_W], o[GRID_W:]).astype(
                o_ref.dtype)


def _natten(p, tbl, batch, seq):
    t = batch * seq
    nb = seq // NA_STEP_TOK
    tok = NA_STEP_TOK

    def spec(col, which):
        cb = col // A_W
        if which == 0:
            return pl.BlockSpec((tok, A_W), lambda b, i: (b * nb + i, cb))
        if which < 0:
            return pl.BlockSpec((tok, A_W), lambda b, i: (b * nb + jnp.maximum(i - 1, 0), cb))
        return pl.BlockSpec((tok, A_W), lambda b, i: (b * nb + jnp.minimum(i + 1, nb - 1), cb))

    return pl.pallas_call(
        functools.partial(_natten_kernel, n_rows=seq // GRID_W),
        grid=(batch, nb),
        in_specs=[spec(COL_AQ, 0),
                  spec(COL_AK, -1), spec(COL_AK, 0), spec(COL_AK, 1),
                  spec(COL_AV, -1), spec(COL_AV, 0), spec(COL_AV, 1),
                  pl.BlockSpec(tbl.shape, lambda b, i: (0, 0, 0, 0))],
        out_specs=pl.BlockSpec((tok, A_W), lambda b, i: (b * nb + i, 0)),
        out_shape=jax.ShapeDtypeStruct((t, A_W), BF16),
        scratch_shapes=[pltpu.VMEM((NA_STEP_ROWS, NA_HEADS // 2, 2 * GRID_W, LANES), BF16),
                        pltpu.VMEM((3 * tok, A_W), BF16), pltpu.VMEM((3 * tok, A_W), BF16)],
        compiler_params=pltpu.CompilerParams(
            dimension_semantics=("parallel", "arbitrary"), vmem_limit_bytes=V7X_VMEM_LIMIT),
        name="natten",
    )(p, p, p, p, p, p, p, tbl)


BAND_SUB = 128


def _band_table(slopes, dist_scale, w, chains):
    delta = np.arange(BAND_SUB + 2 * w)[None, :] - w - np.arange(BAND_SUB)[:, None]
    bias = -slopes[:, None, None] * (dist_scale * np.abs(delta)).astype(np.float32)[None]
    tbl = np.where(np.abs(delta)[None] <= w, bias, NEG_INF).astype(np.float32)
    return tbl.reshape(chains, -1, BAND_SUB + 2 * w)


def _banded_kernel(*refs, w, tq, n, kvmap, has_sink, has_lse):
    refs = list(refs)
    sink_ref = refs.pop(0) if has_sink else None
    q_ref, kp_ref, kc_ref, kn_ref, vp_ref, vc_ref, vn_ref, tbl_ref, o_ref = refs[:9]
    lse_ref = refs[9] if has_lse else None
    qs, kbuf, vbuf = refs[-3:]
    sub = BAND_SUB
    nchains = len(kvmap)
    blocks = q_ref.shape[1] // LANES // nchains
    i = pl.program_id(1)
    kbuf[0:w] = kp_ref[...]
    kbuf[w:w + tq] = kc_ref[...]
    kbuf[w + tq:] = kn_ref[...]
    vbuf[0:w] = vp_ref[...]
    vbuf[w:w + tq] = vc_ref[...]
    vbuf[w + tq:] = vn_ref[...]
    lane, hm = _half_masks()
    for j in range(tq // sub):
        for pi in range(nchains * blocks):
            qpair = q_ref[j * sub:(j + 1) * sub, pi * LANES:(pi + 1) * LANES]
            for e in range(2):
                qs[j, (2 * pi + e) * sub:(2 * pi + e + 1) * sub, :] = qpair * hm[e]
    nsub = tq // sub
    rows = 2 * blocks * sub
    nkeys = sub + 2 * w
    range_bias = []
    for j in range(nsub):
        kpos = i * tq + j * sub - w + lax.broadcasted_iota(jnp.int32, (1, nkeys), 1)
        range_bias.append(jnp.where(kpos < 0, NEG_INF, jnp.where(kpos >= n, NEG_INF, 0.0)))
    for c, kv in enumerate(kvmap):
        kvcols = slice(kv * LANES, (kv + 1) * LANES)
        tbl = tbl_ref[c]
        s = jnp.concatenate([_dot_nt(qs[j, c * rows:(c + 1) * rows, :], kbuf[j * sub:j * sub + nkeys, kvcols])
                             + tbl + range_bias[j] for j in range(nsub)], axis=0)
        m = jnp.max(s, axis=-1, keepdims=True)
        if has_sink:
            sink = jnp.concatenate([jnp.full((sub, LANES), sink_ref[c * 2 * blocks + h], F32)
                                    for h in range(2 * blocks)] * nsub, axis=0)
            m = jnp.maximum(m, sink)
        else:
            m = jnp.broadcast_to(m, (m.shape[0], LANES))
        p = jnp.concatenate([jnp.exp(s[:, k * LANES:(k + 1) * LANES] - m) for k in range(nkeys // LANES)], axis=1)
        l = jnp.sum(p, axis=-1, keepdims=True)
        if has_sink:
            l = l + jnp.exp(sink - m)
        else:
            l = jnp.broadcast_to(l, (l.shape[0], LANES))
        inv = 1.0 / l
        lse = m + jnp.log(l)
        p = p.astype(BF16)
        for j in range(nsub):
            jr = slice(j * rows, (j + 1) * rows)
            o = jnp.dot(p[jr], vbuf[j * sub:j * sub + nkeys, kvcols], preferred_element_type=F32) * inv[jr]
            for x in range(blocks):
                cols = slice((c * blocks + x) * LANES, (c * blocks + x + 1) * LANES)
                even, odd = slice(2 * x * sub, (2 * x + 1) * sub), slice((2 * x + 1) * sub, (2 * x + 2) * sub)
                o_ref[j * sub:(j + 1) * sub, cols] = jnp.where(lane < HEAD_DIM, o[even], o[odd]).astype(o_ref.dtype)
                if has_lse:
                    lse_j = lse[jr]
                    lse_ref[j * sub:(j + 1) * sub, cols] = jnp.where(lane < HEAD_DIM, lse_j[even], lse_j[odd])


def _banded(arr, *, w, col_q, col_k, col_v, wq, wkv, kvmap, table, sink, out_dtype, has_lse, name):
    classes, n, _ = arr.shape
    tq = min(512, n)
    sub = tq // w
    nblk = n // w

    def qspec(col, width):
        return pl.BlockSpec((None, tq, width), lambda c, i: (c, i, col // width))

    def prev_spec(col, width):
        return pl.BlockSpec((None, w, width), lambda c, i: (c, jnp.maximum(i * sub - 1, 0), col // width))

    def next_spec(col, width):
        return pl.BlockSpec((None, w, width), lambda c, i: (c, jnp.minimum((i + 1) * sub, nblk - 1), col // width))

    in_specs = [qspec(col_q, wq),
                prev_spec(col_k, wkv), qspec(col_k, wkv), next_spec(col_k, wkv),
                prev_spec(col_v, wkv), qspec(col_v, wkv), next_spec(col_v, wkv),
                pl.BlockSpec(table.shape, lambda c, i: (0, 0, 0))]
    args = [arr] * 7 + [jnp.asarray(table)]
    if sink is not None:
        in_specs = [pl.BlockSpec(memory_space=pltpu.SMEM)] + in_specs
        args = [sink.astype(F32)] + args
    out_block = pl.BlockSpec((None, tq, wq), lambda c, i: (c, i, 0))
    out_shape = [jax.ShapeDtypeStruct((classes, n, wq), out_dtype)]
    out_specs = [out_block]
    if has_lse:
        out_shape.append(jax.ShapeDtypeStruct((classes, n, wq), F32))
        out_specs.append(out_block)
    return pl.pallas_call(
        functools.partial(_banded_kernel, w=w, tq=tq, n=n, kvmap=kvmap, has_sink=sink is not None, has_lse=has_lse),
        grid=(classes, n // tq),
        in_specs=in_specs,
        out_specs=out_specs,
        out_shape=out_shape,
        scratch_shapes=[pltpu.VMEM((tq // BAND_SUB, 2 * (wq // LANES) * BAND_SUB, LANES), BF16),
                        pltpu.VMEM((tq + 2 * w, wkv), BF16), pltpu.VMEM((tq + 2 * w, wkv), BF16)],
        compiler_params=pltpu.CompilerParams(
            dimension_semantics=("parallel", "arbitrary"), vmem_limit_bytes=V7X_VMEM_LIMIT),
        name=name,
    )(*args)


def _rope_tables(seq):
    t = jnp.arange(seq)
    npair = HEAD_DIM // 4
    freqs = ROPE_THETA ** (-jnp.arange(npair, dtype=F32) / npair)

    def cs(pos):
        ang = pos.astype(F32)[:, None] * freqs[None, :]
        return jnp.cos(ang), jnp.sin(ang)

    cr, sr = cs(t // GRID_W)
    cc, sc = cs(t % GRID_W)
    cos = jnp.concatenate([cr, cr, cc, cc], axis=-1)
    sin = jnp.concatenate([-sr, sr, -sc, sc], axis=-1)
    return jnp.tile(cos, (1, 2)), jnp.tile(sin, (1, 2))


def _cprep_kernel(q_ref, k_ref, v_ref, cos_ref, sin_ref, qg_ref, kg_ref, avg_ref, qo_ref, ko_ref, ve_ref, vo_ref):
    lane = lax.broadcasted_iota(jnp.int32, (1, LANES), 1)
    v = v_ref[...]
    ones = jnp.ones(v.shape, v.dtype)
    ve_ref[...] = jnp.where(lane < HEAD_DIM, v, ones)
    vo_ref[...] = jnp.where(lane < HEAD_DIM, ones, v)
    first = (lane & (HEAD_DIM // 4)) == 0
    cos = cos_ref[...]
    sin = sin_ref[...]
    avg = avg_ref[...]

    def norm_rope(x, gain):
        x = x.astype(F32)
        sq = x * x
        hi = sq.astype(BF16)
        r1 = sq - hi.astype(F32)
        mid = r1.astype(BF16)
        lo = (r1 - mid.astype(F32)).astype(BF16)
        ms = (jnp.dot(hi, avg, preferred_element_type=F32) + jnp.dot(mid, avg, preferred_element_type=F32)
              + jnp.dot(lo, avg, preferred_element_type=F32))
        y = x * lax.rsqrt(ms + RMS_EPS) * gain
        quarter = HEAD_DIM // 4
        partner = jnp.where(first, pltpu.roll(y, LANES - quarter, 1), pltpu.roll(y, quarter, 1))
        return y * cos + partner * sin

    for c in range(q_ref.shape[1] // LANES):
        cols = slice(c * LANES, (c + 1) * LANES)
        qo_ref[:, cols] = norm_rope(q_ref[:, cols], qg_ref[...]).astype(qo_ref.dtype)
    ko_ref[...] = norm_rope(k_ref[...], kg_ref[...]).astype(ko_ref.dtype)


def _cprep(p, cos, sin, q_gain, k_gain, batch, seq, tm=512):
    t = batch * seq
    nt = seq // tm
    avg = np.zeros((LANES, LANES), np.float32)
    avg[:HEAD_DIM, :HEAD_DIM] = 1.0 / HEAD_DIM
    avg[HEAD_DIM:, HEAD_DIM:] = 1.0 / HEAD_DIM
    gw = C_QW // GA_KV_HEADS
    qg = jnp.tile(q_gain.astype(F32) * (HEAD_DIM ** -0.5 * np.log2(np.e)), 2).reshape(1, LANES)
    kg = jnp.tile(k_gain.astype(F32), 2).reshape(1, LANES)
    return pl.pallas_call(
        _cprep_kernel,
        grid=(t // tm, GA_KV_HEADS),
        in_specs=[pl.BlockSpec((tm, gw), lambda i, g: (i, COL_CQ // gw + g)),
                  pl.BlockSpec((tm, LANES), lambda i, g: (i, COL_CK // LANES + g)),
                  pl.BlockSpec((tm, LANES), lambda i, g: (i, COL_CV // LANES + g)),
                  pl.BlockSpec((tm, LANES), lambda i, g: (i % nt, 0)),
                  pl.BlockSpec((tm, LANES), lambda i, g: (i % nt, 0)),
                  pl.BlockSpec((1, LANES), lambda i, g: (0, 0)),
                  pl.BlockSpec((1, LANES), lambda i, g: (0, 0)),
                  pl.BlockSpec((LANES, LANES), lambda i, g: (0, 0))],
        out_specs=[pl.BlockSpec((tm, gw), lambda i, g: (i, g))] + [pl.BlockSpec((tm, LANES), lambda i, g: (i, g))] * 3,
        out_shape=[jax.ShapeDtypeStruct((t, C_QW), BF16)] + [jax.ShapeDtypeStruct((t, GA_KV_HEADS * LANES), BF16)] * 3,
        compiler_params=pltpu.CompilerParams(
            dimension_semantics=("parallel", "arbitrary"), vmem_limit_bytes=V7X_VMEM_LIMIT),
        name="cprep",
    )(p, p, p, cos, sin, qg, kg, jnp.asarray(avg, BF16))


def _flash_kernel(q_ref, k_ref, ve_ref, vo_ref, o_ref, qs_ref, m_ref, acc_ref, *, tk):
    nkv = k_ref.shape[0] // tk
    nheads = qs_ref.shape[0]
    lane, hm = _half_masks()
    for h in range(nheads):
        qs_ref[h] = q_ref[:, (h // 2) * LANES:(h // 2 + 1) * LANES] * hm[h % 2]
    m_ref[...] = jnp.full(m_ref.shape, NEG_INF, F32)
    acc_ref[...] = jnp.zeros(acc_ref.shape, F32)

    def body(t, carry):
        rows = pl.ds(pl.multiple_of(t * tk, tk), tk)
        k = k_ref[rows, :]
        vs = (ve_ref[rows, :], vo_ref[rows, :])
        for h in range(nheads):
            s = _dot_nt(qs_ref[h], k)
            m_old = m_ref[h]
            m_new = jnp.maximum(m_old, jnp.max(s, axis=-1, keepdims=True))
            alpha = jnp.exp2(m_old - m_new)
            p = jnp.concatenate([jnp.exp2(s[:, c * LANES:(c + 1) * LANES] - m_new) for c in range(tk // LANES)],
                                axis=1)
            acc_ref[h] = alpha * acc_ref[h] + jnp.dot(p.astype(BF16), vs[h % 2], preferred_element_type=F32)
            m_ref[h] = m_new
        return carry

    lax.fori_loop(0, nkv, body, 0)
    for pi in range(nheads // 2):
        even, odd = acc_ref[2 * pi], acc_ref[2 * pi + 1]
        even = even / pltpu.roll(even, HEAD_DIM, 1)
        odd = odd / pltpu.roll(odd, HEAD_DIM, 1)
        o_ref[:, pi * LANES:(pi + 1) * LANES] = jnp.where(lane < HEAD_DIM, even, odd).astype(o_ref.dtype)


def _flash(q, k, ve, vo, batch, seq, tq=1024, tk=1024):
    t = batch * seq
    nq = seq // tq
    gw = C_QW // GA_KV_HEADS
    nheads = GA_Q_HEADS // GA_KV_HEADS
    kv_spec = pl.BlockSpec((seq, LANES), lambda b, g, i: (b, g))
    return pl.pallas_call(
        functools.partial(_flash_kernel, tk=tk),
        grid=(batch, GA_KV_HEADS, nq),
        in_specs=[pl.BlockSpec((tq, gw), lambda b, g, i: (b * nq + i, g)), kv_spec, kv_spec, kv_spec],
        out_specs=pl.BlockSpec((tq, gw), lambda b, g, i: (b * nq + i, g)),
        out_shape=jax.ShapeDtypeStruct((t, C_QW), BF16),
        scratch_shapes=[pltpu.VMEM((nheads, tq, LANES), BF16), pltpu.VMEM((nheads, tq, LANES), F32),
                        pltpu.VMEM((nheads, tq, LANES), F32)],
        compiler_params=pltpu.CompilerParams(
            dimension_semantics=("parallel", "parallel", "arbitrary"), vmem_limit_bytes=V7X_VMEM_LIMIT),
        name="flash",
    )(q, k, ve, vo)


def _merge_kernel(x_ref, ya_ref, ob0_ref, ob1_ref, ob2_ref, ls0_ref, ls1_ref, ls2_ref, yc_ref, yd_ref,
                  ga_ref, gb_ref, gc_ref, gd_ref, wa_ref, wb_ref, wc_ref, wd_ref, wo_ref, gn_ref,
                  xo_ref, ho_ref, stage):
    tm = x_ref.shape[0]
    slots = iter(range(stage.shape[0]))

    def token_order(ref):
        dil = ref.shape[0]
        if dil == 1:
            return ref[0]
        slot = next(slots)
        for r in range(dil):
            for s in range(B_OUT // LANES):
                stage[slot, s, pl.ds(r, tm // dil, stride=dil), :] = ref[r, :, s * LANES:(s + 1) * LANES]
        return jnp.concatenate([stage[slot, s] for s in range(B_OUT // LANES)], axis=1)

    l0, l1, l2 = token_order(ls0_ref), token_order(ls1_ref), token_order(ls2_ref)
    mx = jnp.maximum(jnp.maximum(l0, l1), l2)
    e0, e1, e2 = jnp.exp(l0 - mx), jnp.exp(l1 - mx), jnp.exp(l2 - mx)
    yb = (e0 * token_order(ob0_ref) + e1 * token_order(ob1_ref) + e2 * token_order(ob2_ref)) / (e0 + e1 + e2)

    def branch(g_ref, y, w_ref):
        return g_ref[...].astype(F32) * jnp.dot(y, w_ref[...], preferred_element_type=F32)

    merged = (branch(ga_ref, ya_ref[...], wa_ref) + branch(gb_ref, yb.astype(BF16), wb_ref)
              + branch(gc_ref, yc_ref[...], wc_ref) + branch(gd_ref, yd_ref[...], wd_ref))
    out = x_ref[...] + jnp.dot(merged.astype(BF16), wo_ref[...], preferred_element_type=F32)
    xo_ref[...] = out
    ho_ref[...] = _rms_rows(out, gn_ref[...]).astype(ho_ref.dtype)


def _merge(x, ya, obs, lss, yc, yd, gates, wa, wb, wc, wd, wo, gain, seq, tm=256):
    t, d = x.shape
    nb = seq // tm

    def rows(width, col=0):
        return pl.BlockSpec((tm, width), lambda i: (i, col))

    def classes(arr):
        dil = arr.shape[1]
        return pl.BlockSpec((None, dil, tm // dil, B_OUT), lambda i: (i // nb, 0, i % nb, 0))

    def const(arr):
        return pl.BlockSpec(arr.shape, lambda i: (0, 0), pipeline_mode=pl.Buffered(1))

    gain = gain.reshape(1, d)
    return pl.pallas_call(
        _merge_kernel,
        grid=(t // tm,),
        in_specs=[rows(d), rows(A_W)] + [classes(a) for a in obs] + [classes(a) for a in lss]
                 + [rows(C_QW), rows(D_QW), rows(d, 0), rows(d, 1), rows(d, 2), rows(d, 3),
                    const(wa), const(wb), const(wc), const(wd), const(wo), const(gain)],
        out_specs=[rows(d), rows(d)],
        out_shape=[jax.ShapeDtypeStruct((t, d), F32), jax.ShapeDtypeStruct((t, d), BF16)],
        scratch_shapes=[pltpu.VMEM((2 * len(DILATED), B_OUT // LANES, tm, LANES), F32)],
        compiler_params=pltpu.CompilerParams(dimension_semantics=("parallel",), vmem_limit_bytes=V7X_VMEM_LIMIT),
        name="merge",
    )(x, ya, *obs, *lss, yc, yd, gates, gates, gates, gates, wa, wb, wc, wd, wo, gain)


def _mlp_kernel(h_ref, x_ref, wu_ref, wd_ref, gn_ref, *rest, final, row_chunk):
    if final:
        yo_ref, acc_ref = rest
    else:
        xo_ref, ho_ref, acc_ref = rest
    f = pl.program_id(1)

    @pl.when(f == 0)
    def _():
        acc_ref[...] = x_ref[...]

    for r in range(h_ref.shape[0] // row_chunk):
        rows = pl.ds(r * row_chunk, row_chunk)
        u = jnp.dot(h_ref[rows, :], wu_ref[...], preferred_element_type=F32)
        a = jnp.square(jnp.maximum(u, 0.0)).astype(BF16)
        acc_ref[rows, :] += jnp.dot(a, wd_ref[...], preferred_element_type=F32)

    @pl.when(f == pl.num_programs(1) - 1)
    def _():
        out = acc_ref[...]
        normed = _rms_rows(out, gn_ref[...])
        if final:
            yo_ref[...] = normed
        else:
            xo_ref[...] = out
            ho_ref[...] = normed.astype(ho_ref.dtype)


def _mlp(h, x, wu, wd, gain, *, final, tm=512, tf=1024):
    t, d = x.shape
    ff = wu.shape[1]
    rows = pl.BlockSpec((tm, d), lambda i, f: (i, 0))
    if final:
        out_specs = [rows]
        out_shape = [jax.ShapeDtypeStruct((t, d), F32)]
    else:
        out_specs = [rows, rows]
        out_shape = [jax.ShapeDtypeStruct((t, d), F32), jax.ShapeDtypeStruct((t, d), BF16)]
    return pl.pallas_call(
        functools.partial(_mlp_kernel, final=final, row_chunk=256),
        grid=(t // tm, ff // tf),
        in_specs=[rows, rows,
                  pl.BlockSpec((d, tf), lambda i, f: (0, f)),
                  pl.BlockSpec((tf, d), lambda i, f: (f, 0)),
                  pl.BlockSpec((1, d), lambda i, f: (0, 0))],
        out_specs=out_specs,
        out_shape=out_shape,
        scratch_shapes=[pltpu.VMEM((tm, d), F32)],
        compiler_params=pltpu.CompilerParams(
            dimension_semantics=("parallel", "arbitrary"), vmem_limit_bytes=V7X_VMEM_LIMIT),
        name="mlp_final" if final else "mlp",
    )(h, x, wu, wd, gain.reshape(1, d))


def _prep_in_weights(w_in):
    offs = np.cumsum((0,) + SPLIT_SIZES)
    aq, ak, av, bq, bk, bv, cq, ck, cv, dq, dk, dv, gl = [w_in[..., offs[s]:offs[s + 1]] for s in range(13)]
    scale = HEAD_DIM ** -0.5

    def dup(t):
        k0, k1 = t[..., :HEAD_DIM], t[..., HEAD_DIM:]
        return jnp.concatenate([k0, k0, k1, k1], axis=-1)

    b_groups = []
    for g in range(len(DIL_PATTERNS)):
        sl = slice(g * B_OUT, (g + 1) * B_OUT)
        b_groups += [bq[..., sl] * scale, bk[..., sl], bv[..., sl]]
    pad = jnp.zeros(w_in.shape[:-1] + (P_WIDTH - COL_BD - 2 * BD_WIDTH,), w_in.dtype)
    w_mix = jnp.concatenate([aq * scale, ak, av, cq, dq * scale] + b_groups[:3]
                            + [dup(ck), dup(cv), dup(dk), dup(dv)] + b_groups[3:] + [pad], axis=-1)
    return w_mix.astype(BF16), gl.astype(BF16)


def _trunk(x, wts, tables):
    batch, seq, d = x.shape
    t = batch * seq
    depth = wts["w_mix"].shape[0]
    cos, sin = _rope_tables(seq)
    x2 = x.reshape(t, d)
    h = _rmsnorm(x2, wts["norm_mix"][0])
    b_slopes = _alibi_slopes(DIL_HEADS)
    d_table = _band_table(_alibi_slopes(SW_Q_HEADS), 1, SW_HALF_WINDOW, SW_KV_HEADS)
    for l in range(depth):
        p, *folded = _proj_mix(h, wts["w_mix"][l], batch, seq)
        gates = _proj(h, wts["w_gate"][l], sigmoid=True, name="proj_gate")
        ya = _natten(p, tables["natten"][l], batch, seq)
        obs, lss = [], []
        hg = DIL_HEADS_PER_GROUP
        p3 = p.reshape(batch, seq, P_WIDTH)
        for g, (win, dil) in enumerate(DIL_PATTERNS):
            if dil == 1:
                arr, col = p3, COL_B0
            else:
                arr, col = folded[DILATED.index(dil)].reshape(batch * dil, seq // dil, BD_WIDTH), 0
            ob, ls = _banded(arr, w=win // (2 * dil), col_q=col, col_k=col + B_OUT, col_v=col + 2 * B_OUT,
                             wq=B_OUT, wkv=B_OUT, kvmap=(0, 1),
                             table=_band_table(b_slopes[g * hg:(g + 1) * hg], dil, win // (2 * dil), 2),
                             sink=None, out_dtype=F32, has_lse=True, name="banded_d%d" % dil)
            obs.append(ob.reshape(batch, dil, seq // dil, B_OUT))
            lss.append(ls.reshape(batch, dil, seq // dil, B_OUT))
        qr, kr, ve, vo = _cprep(p, cos, sin, wts["q_gain_c"][l], wts["k_gain_c"][l], batch, seq)
        yc = _flash(qr, kr, ve, vo, batch, seq)
        (yd,) = _banded(p3, w=SW_HALF_WINDOW, col_q=COL_DQ, col_k=COL_DK, col_v=COL_DV, wq=D_QW, wkv=2 * D_KVW,
                        kvmap=(0, 1), table=d_table, sink=wts["sink_d"][l], out_dtype=BF16, has_lse=False,
                        name="banded_sink")
        yd = yd.reshape(t, D_QW)
        x2, h = _merge(x2, ya, obs, lss, yc, yd, gates, wts["w_a"][l], wts["w_b"][l], wts["w_c"][l],
                       wts["w_d"][l], wts["w_out"][l], wts["norm_mlp"][l], seq)
        if l + 1 < depth:
            x2, h = _mlp(h, x2, wts["w_up"][l], wts["w_down"][l], wts["norm_mix"][l + 1], final=False)
        else:
            (y,) = _mlp(h, x2, wts["w_up"][l], wts["w_down"][l], wts["norm_final"], final=True)
    return y.reshape(batch, seq, d)


def kernel(x_prompt, x_sample, norm_mix, w_in, rel_bias_a, q_gain_c, k_gain_c, sink_d, w_branch_a, w_branch_b,
           w_branch_c, w_branch_d, w_out, norm_mlp, w_up, w_down, norm_final):
    w_mix, w_gate = _prep_in_weights(w_in)
    wts = dict(norm_mix=norm_mix, w_mix=w_mix, w_gate=w_gate, q_gain_c=q_gain_c, k_gain_c=k_gain_c, sink_d=sink_d,
               w_a=w_branch_a.astype(BF16), w_b=w_branch_b.astype(BF16), w_c=w_branch_c.astype(BF16),
               w_d=w_branch_d.astype(BF16), w_out=w_out.astype(BF16), norm_mlp=norm_mlp,
               w_up=w_up.astype(BF16), w_down=w_down.astype(BF16), norm_final=norm_final)
    tables = dict(natten=jax.vmap(_natten_bias_table)(rel_bias_a))
    return (_trunk(x_prompt, wts, tables), _trunk(x_sample, wts, tables))
```

```python
import functools

import numpy as np
import jax
import jax.numpy as jnp
from jax import lax
from jax.experimental import pallas as pl
from jax.experimental.pallas import tpu as pltpu

F32 = jnp.float32
BF16 = jnp.bfloat16

D_MODEL = 2048
HEAD_DIM = 64
GRID_W = 64
NA_HEADS = 8
NA_ROWS = 8
NA_COLS = 16
DIL_PATTERNS = ((128, 1), (512, 4), (2048, 16))
DIL_HEADS_PER_GROUP = 4
DIL_HEADS = 12
DIL_BLOCK = 64
GA_Q_HEADS = 8
GA_KV_HEADS = 2
ROPE_THETA = 10000.0
SW_Q_HEADS = 8
SW_KV_HEADS = 2
SW_HALF_WINDOW = 128
N_BRANCHES = 4
D_FF = 4 * D_MODEL
RMS_EPS = 1e-6
NEG_INF = -1e30
LOG2E = float(np.log2(np.e))

A_W = NA_HEADS * HEAD_DIM
B_W = DIL_HEADS * HEAD_DIM
B_OUT = DIL_HEADS_PER_GROUP * HEAD_DIM
C_QW = GA_Q_HEADS * HEAD_DIM
C_KVW = GA_KV_HEADS * HEAD_DIM
D_QW = SW_Q_HEADS * HEAD_DIM
D_KVW = SW_KV_HEADS * HEAD_DIM
GATE_W = N_BRANCHES * D_MODEL
SPLIT_SIZES = (A_W, A_W, A_W, B_W, B_W, B_W, C_QW, C_KVW, C_KVW, D_QW, D_KVW, D_KVW, GATE_W)

LANES = 128
V7X_VMEM_LIMIT = 56 * 1024 * 1024

COL_AQ = 0
COL_AK = 512
COL_AV = 1024
COL_CQ = 1536
COL_DQ = 2048
COL_B0 = 2560
COL_CK = 3328
COL_CV = 3584
COL_DK = 3840
COL_DV = 4096
COL_BD = 4352
P_WIDTH = 6144
PROJ_TN = 2048
BD_TILE = COL_BD // PROJ_TN
BD_LOCAL = COL_BD % PROJ_TN
BD_WIDTH = 3 * B_OUT
DILATED = tuple(d for _, d in DIL_PATTERNS if d > 1)


def _alibi_slopes(n):
    return (2.0 ** (-8.0 * np.arange(1, n + 1) / n)).astype(np.float32)


def _half_masks():
    lane = lax.broadcasted_iota(jnp.int32, (1, LANES), 1)
    lo = jnp.where(lane < HEAD_DIM, 1.0, 0.0).astype(BF16)
    hi = jnp.where(lane >= HEAD_DIM, 1.0, 0.0).astype(BF16)
    return lane, (lo, hi)


def _dot_nt(a, b):
    return lax.dot_general(a, b, (((1,), (1,)), ((), ())), preferred_element_type=F32)


def _rms_rows(x, gain):
    ms = jnp.mean(x * x, axis=-1, keepdims=True)
    return x * lax.rsqrt(ms + RMS_EPS) * gain


def _rmsnorm_kernel(x_ref, g_ref, o_ref):
    o_ref[...] = _rms_rows(x_ref[...], g_ref[...]).astype(o_ref.dtype)


def _rmsnorm(x, gain, tm=512):
    t, d = x.shape
    return pl.pallas_call(
        _rmsnorm_kernel,
        grid=(t // tm,),
        in_specs=[pl.BlockSpec((tm, d), lambda i: (i, 0)), pl.BlockSpec((1, d), lambda i: (0, 0))],
        out_specs=pl.BlockSpec((tm, d), lambda i: (i, 0)),
        out_shape=jax.ShapeDtypeStruct((t, d), BF16),
        compiler_params=pltpu.CompilerParams(dimension_semantics=("parallel",), vmem_limit_bytes=V7X_VMEM_LIMIT),
        name="rmsnorm",
    )(x, gain.reshape(1, d))


def _proj_kernel(h_ref, w_ref, o_ref, *, row_chunk, sigmoid):
    for r in range(h_ref.shape[0] // row_chunk):
        rows = pl.ds(r * row_chunk, row_chunk)
        acc = jnp.dot(h_ref[rows, :], w_ref[...], preferred_element_type=F32)
        if sigmoid:
            acc = 1.0 / (1.0 + jnp.exp(-acc))
        o_ref[rows, :] = acc.astype(o_ref.dtype)


def _proj(h, w, *, sigmoid, tm=1024, tn=PROJ_TN, name):
    t, k = h.shape
    n = w.shape[1]
    return pl.pallas_call(
        functools.partial(_proj_kernel, row_chunk=256, sigmoid=sigmoid),
        grid=(t // tm, n // tn),
        in_specs=[pl.BlockSpec((tm, k), lambda i, j: (i, 0)), pl.BlockSpec((k, tn), lambda i, j: (0, j))],
        out_specs=pl.BlockSpec((tm, tn), lambda i, j: (i, j)),
        out_shape=jax.ShapeDtypeStruct((t, n), BF16),
        compiler_params=pltpu.CompilerParams(
            dimension_semantics=("parallel", "arbitrary"), vmem_limit_bytes=V7X_VMEM_LIMIT),
        name=name,
    )(h, w)


def _proj_mix_kernel(h_ref, w_ref, o_ref, *rest, row_chunk):
    (f4_ref, f16_ref), (stage, stage4) = rest[:2], rest[2:]
    j = pl.program_id(1)
    tm = h_ref.shape[0]
    slabs = BD_WIDTH // LANES
    quarter = tm // 4
    for r in range(tm // row_chunk):
        rows = pl.ds(r * row_chunk, row_chunk)
        acc = jnp.dot(h_ref[rows, :], w_ref[...], preferred_element_type=F32)
        o_ref[rows, :] = acc.astype(o_ref.dtype)
        for s in range(2 * slabs):
            stage[s, rows, :] = acc[:, BD_LOCAL + s * LANES:BD_LOCAL + (s + 1) * LANES]

    @pl.when(j == BD_TILE)
    def _():
        for s in range(slabs):
            for r4 in range(4):
                f4_ref[r4, :, s * LANES:(s + 1) * LANES] = stage[s, pl.ds(r4, quarter, stride=4), :].astype(
                    f4_ref.dtype)
                stage4[s, r4 * quarter:(r4 + 1) * quarter, :] = stage[slabs + s, pl.ds(r4, quarter, stride=4), :]
        for s in range(slabs):
            for r4 in range(4):
                for r4b in range(4):
                    part = stage4[s, pl.ds(r4 * quarter + r4b, quarter // 4, stride=4), :]
                    f16_ref[4 * r4b + r4, :, s * LANES:(s + 1) * LANES] = part.astype(f16_ref.dtype)


def _proj_mix(h, w, batch, seq, tm=1024):
    assert DILATED == (4, 16), "the fold in _proj_mix_kernel is written for dilations 4 and 16"
    t, k = h.shape
    nb = seq // tm
    fold_specs = [pl.BlockSpec((None, dil, tm // dil, BD_WIDTH), lambda i, j: (i // nb, 0, i % nb, 0))
                  for dil in DILATED]
    fold_shapes = [jax.ShapeDtypeStruct((batch, dil, seq // dil, BD_WIDTH), BF16) for dil in DILATED]
    return pl.pallas_call(
        functools.partial(_proj_mix_kernel, row_chunk=256),
        grid=(t // tm, P_WIDTH // PROJ_TN),
        in_specs=[pl.BlockSpec((tm, k), lambda i, j: (i, 0)), pl.BlockSpec((k, PROJ_TN), lambda i, j: (0, j))],
        out_specs=[pl.BlockSpec((tm, PROJ_TN), lambda i, j: (i, j))] + fold_specs,
        out_shape=[jax.ShapeDtypeStruct((t, P_WIDTH), BF16)] + fold_shapes,
        scratch_shapes=[pltpu.VMEM((2 * BD_WIDTH // LANES, tm, LANES), F32),
                        pltpu.VMEM((BD_WIDTH // LANES, tm, LANES), F32)],
        compiler_params=pltpu.CompilerParams(
            dimension_semantics=("parallel", "arbitrary"), vmem_limit_bytes=V7X_VMEM_LIMIT),
        name="proj_mix",
    )(h, w)


NA_STEP_ROWS = 8
NA_STEP_TOK = NA_STEP_ROWS * GRID_W
NA_KEYS = NA_ROWS * GRID_W


def _natten_bias_table(rel_bias):
    qc = np.arange(GRID_W)[:, None]
    kc = np.arange(GRID_W)[None, :]
    c0 = np.clip(qc - NA_COLS // 2, 0, GRID_W - NA_COLS)
    valid = (kc >= c0) & (kc < c0 + NA_COLS)
    dc = np.clip(kc - qc, -(NA_COLS - 1), NA_COLS - 1) + NA_COLS - 1
    shift = np.arange(NA_ROWS)[:, None]
    krow = np.arange(NA_ROWS)[None, :]
    dr = krow - shift + NA_ROWS - 1
    tbl = rel_bias[:, dr][:, :, :, dc]
    tbl = jnp.where(valid[None, None, None], tbl * LOG2E, NEG_INF)
    tbl = jnp.transpose(tbl, (1, 0, 3, 2, 4))
    return tbl.reshape(NA_ROWS, NA_HEADS // 2, 2 * GRID_W, NA_KEYS).astype(F32)


def _natten_kernel(q_ref, kp_ref, kc_ref, kn_ref, vp_ref, vc_ref, vn_ref, tbl_ref, o_ref, qs, kbuf, vbuf, *, n_rows):
    i = pl.program_id(1)
    for s, (kr, vr) in enumerate(((kp_ref, vp_ref), (kc_ref, vc_ref), (kn_ref, vn_ref))):
        kbuf[s * NA_STEP_TOK:(s + 1) * NA_STEP_TOK, :] = kr[...]
        vbuf[s * NA_STEP_TOK:(s + 1) * NA_STEP_TOK, :] = vr[...]
    lane, hm = _half_masks()
    for j in range(NA_STEP_ROWS):
        for pi in range(NA_HEADS // 2):
            qpair = q_ref[j * GRID_W:(j + 1) * GRID_W, pi * LANES:(pi + 1) * LANES]
            for e in range(2):
                qs[j, pi, e * GRID_W:(e + 1) * GRID_W, :] = qpair * hm[e]

    shifts, offs = [], []
    for j in range(NA_STEP_ROWS):
        r = i * NA_STEP_ROWS + j
        r0 = jnp.clip(r - NA_ROWS // 2, 0, n_rows - NA_ROWS)
        shifts.append(r - r0)
        offs.append(pl.multiple_of((r0 - (i - 1) * NA_STEP_ROWS) * GRID_W, GRID_W))
    for pi in range(NA_HEADS // 2):
        cols = slice(pi * LANES, (pi + 1) * LANES)
        s = jnp.concatenate([_dot_nt(qs[j, pi], kbuf[pl.ds(offs[j], NA_KEYS), cols]) + tbl_ref[shifts[j], pi]
                             for j in range(NA_STEP_ROWS)], axis=0)
        m = jnp.broadcast_to(jnp.max(s, axis=-1, keepdims=True), (s.shape[0], LANES))
        p = jnp.concatenate([jnp.exp2(s[:, k * LANES:(k + 1) * LANES] - m) for k in range(NA_KEYS // LANES)], axis=1)
        inv = 1.0 / jnp.broadcast_to(jnp.sum(p, axis=-1, keepdims=True), (s.shape[0], LANES))
        p = p.astype(BF16)
        for j in range(NA_STEP_ROWS):
            rows = slice(j * 2 * GRID_W, (j + 1) * 2 * GRID_W)
            o = jnp.dot(p[rows], vbuf[pl.ds(offs[j], NA_KEYS), cols], preferred_element_type=F32) * inv[rows]
            o_ref[j * GRID_W:(j + 1) * GRID_W, cols] = jnp.where(lane < HEAD_DIM, o[:GRID_W], o[GRID_W:]).astype(
                o_ref.dtype)


def _natten(p, tbl, batch, seq):
    t = batch * seq
    nb = seq // NA_STEP_TOK
    tok = NA_STEP_TOK

    def spec(col, which):
        cb = col // A_W
        if which == 0:
            return pl.BlockSpec((tok, A_W), lambda b, i: (b * nb + i, cb))
        if which < 0:
            return pl.BlockSpec((tok, A_W), lambda b, i: (b * nb + jnp.maximum(i - 1, 0), cb))
        return pl.BlockSpec((tok, A_W), lambda b, i: (b * nb + jnp.minimum(i + 1, nb - 1), cb))

    return pl.pallas_call(
        functools.partial(_natten_kernel, n_rows=seq // GRID_W),
        grid=(batch, nb),
        in_specs=[spec(COL_AQ, 0),
                  spec(COL_AK, -1), spec(COL_AK, 0), spec(COL_AK, 1),
                  spec(COL_AV, -1), spec(COL_AV, 0), spec(COL_AV, 1),
                  pl.BlockSpec(tbl.shape, lambda b, i: (0, 0, 0, 0))],
        out_specs=pl.BlockSpec((tok, A_W), lambda b, i: (b * nb + i, 0)),
        out_shape=jax.ShapeDtypeStruct((t, A_W), BF16),
        scratch_shapes=[pltpu.VMEM((NA_STEP_ROWS, NA_HEADS // 2, 2 * GRID_W, LANES), BF16),
                        pltpu.VMEM((3 * tok, A_W), BF16), pltpu.VMEM((3 * tok, A_W), BF16)],
        compiler_params=pltpu.CompilerParams(
            dimension_semantics=("parallel", "arbitrary"), vmem_limit_bytes=V7X_VMEM_LIMIT),
        name="natten",
    )(p, p, p, p, p, p, p, tbl)


BAND_SUB = 128


def _band_table(slopes, dist_scale, w, chains):
    delta = np.arange(BAND_SUB + 2 * w)[None, :] - w - np.arange(BAND_SUB)[:, None]
    bias = -slopes[:, None, None] * (dist_scale * np.abs(delta)).astype(np.float32)[None]
    tbl = np.where(np.abs(delta)[None] <= w, bias * LOG2E, NEG_INF).astype(np.float32)
    return tbl.reshape(chains, -1, BAND_SUB + 2 * w)


def _banded_kernel(*refs, w, tq, n, kvmap, has_sink, has_lse):
    refs = list(refs)
    sink_ref = refs.pop(0) if has_sink else None
    q_ref, kp_ref, kc_ref, kn_ref, vp_ref, vc_ref, vn_ref, tbl_ref, o_ref = refs[:9]
    lse_ref = refs[9] if has_lse else None
    qs, kbuf, vbuf = refs[-3:]
    sub = BAND_SUB
    nchains = len(kvmap)
    blocks = q_ref.shape[1] // LANES // nchains
    i = pl.program_id(1)
    kbuf[0:w] = kp_ref[...]
    kbuf[w:w + tq] = kc_ref[...]
    kbuf[w + tq:] = kn_ref[...]
    vbuf[0:w] = vp_ref[...]
    vbuf[w:w + tq] = vc_ref[...]
    vbuf[w + tq:] = vn_ref[...]
    lane, hm = _half_masks()
    for j in range(tq // sub):
        for pi in range(nchains * blocks):
            qpair = q_ref[j * sub:(j + 1) * sub, pi * LANES:(pi + 1) * LANES]
            for e in range(2):
                qs[j, (2 * pi + e) * sub:(2 * pi + e + 1) * sub, :] = qpair * hm[e]
    nsub = tq // sub
    rows = 2 * blocks * sub
    nkeys = sub + 2 * w
    range_bias = {}
    for j in {0, nsub - 1}:
        kpos = i * tq + j * sub - w + lax.broadcasted_iota(jnp.int32, (1, nkeys), 1)
        range_bias[j] = jnp.where(kpos < 0, NEG_INF, jnp.where(kpos >= n, NEG_INF, 0.0))
    for c, kv in enumerate(kvmap):
        kvcols = slice(kv * LANES, (kv + 1) * LANES)
        tbl = tbl_ref[c]
        parts = []
        for j in range(nsub):
            sj = _dot_nt(qs[j, c * rows:(c + 1) * rows, :], kbuf[j * sub:j * sub + nkeys, kvcols]) + tbl
            parts.append(sj + range_bias[j] if j in range_bias else sj)
        s = jnp.concatenate(parts, axis=0)
        m = jnp.max(s, axis=-1, keepdims=True)
        if has_sink:
            sink = jnp.concatenate([jnp.full((sub, LANES), sink_ref[c * 2 * blocks + h] * LOG2E, F32)
                                    for h in range(2 * blocks)] * nsub, axis=0)
            m = jnp.maximum(m, sink)
        else:
            m = jnp.broadcast_to(m, (m.shape[0], LANES))
        p = jnp.concatenate([jnp.exp2(s[:, k * LANES:(k + 1) * LANES] - m) for k in range(nkeys // LANES)], axis=1)
        l = jnp.sum(p, axis=-1, keepdims=True)
        if has_sink:
            l = l + jnp.exp2(sink - m)
        else:
            l = jnp.broadcast_to(l, (l.shape[0], LANES))
        inv = 1.0 / l
        lse = m + jnp.log2(l)
        p = p.astype(BF16)
        for j in range(nsub):
            jr = slice(j * rows, (j + 1) * rows)
            o = jnp.dot(p[jr], vbuf[j * sub:j * sub + nkeys, kvcols], preferred_element_type=F32) * inv[jr]
            for x in range(blocks):
                cols = slice((c * blocks + x) * LANES, (c * blocks + x + 1) * LANES)
                even, odd = slice(2 * x * sub, (2 * x + 1) * sub), slice((2 * x + 1) * sub, (2 * x + 2) * sub)
                o_ref[j * sub:(j + 1) * sub, cols] = jnp.where(lane < HEAD_DIM, o[even], o[odd]).astype(o_ref.dtype)
                if has_lse:
                    lse_j = lse[jr]
                    lse_ref[j * sub:(j + 1) * sub, cols] = jnp.where(lane < HEAD_DIM, lse_j[even], lse_j[odd])


def _banded(arr, *, w, col_q, col_k, col_v, wq, wkv, kvmap, table, sink, out_dtype, has_lse, name):
    assert w <= BAND_SUB
    classes, n, _ = arr.shape
    tq = min(512, n)
    sub = tq // w
    nblk = n // w

    def qspec(col, width):
        return pl.BlockSpec((None, tq, width), lambda c, i: (c, i, col // width))

    def prev_spec(col, width):
        return pl.BlockSpec((None, w, width), lambda c, i: (c, jnp.maximum(i * sub - 1, 0), col // width))

    def next_spec(col, width):
        return pl.BlockSpec((None, w, width), lambda c, i: (c, jnp.minimum((i + 1) * sub, nblk - 1), col // width))

    in_specs = [qspec(col_q, wq),
                prev_spec(col_k, wkv), qspec(col_k, wkv), next_spec(col_k, wkv),
                prev_spec(col_v, wkv), qspec(col_v, wkv), next_spec(col_v, wkv),
                pl.BlockSpec(table.shape, lambda c, i: (0, 0, 0))]
    args = [arr] * 7 + [jnp.asarray(table)]
    if sink is not None:
        in_specs = [pl.BlockSpec(memory_space=pltpu.SMEM)] + in_specs
        args = [sink.astype(F32)] + args
    out_block = pl.BlockSpec((None, tq, wq), lambda c, i: (c, i, 0))
    out_shape = [jax.ShapeDtypeStruct((classes, n, wq), out_dtype)]
    out_specs = [out_block]
    if has_lse:
        out_shape.append(jax.ShapeDtypeStruct((classes, n, wq), F32))
        out_specs.append(out_block)
    return pl.pallas_call(
        functools.partial(_banded_kernel, w=w, tq=tq, n=n, kvmap=kvmap, has_sink=sink is not None, has_lse=has_lse),
        grid=(classes, n // tq),
        in_specs=in_specs,
        out_specs=out_specs,
        out_shape=out_shape,
        scratch_shapes=[pltpu.VMEM((tq // BAND_SUB, 2 * (wq // LANES) * BAND_SUB, LANES), BF16),
                        pltpu.VMEM((tq + 2 * w, wkv), BF16), pltpu.VMEM((tq + 2 * w, wkv), BF16)],
        compiler_params=pltpu.CompilerParams(
            dimension_semantics=("parallel", "arbitrary"), vmem_limit_bytes=V7X_VMEM_LIMIT),
        name=name,
    )(*args)


def _rope_tables(seq):
    t = jnp.arange(seq)
    npair = HEAD_DIM // 4
    freqs = ROPE_THETA ** (-jnp.arange(npair, dtype=F32) / npair)

    def cs(pos):
        ang = pos.astype(F32)[:, None] * freqs[None, :]
        return jnp.cos(ang), jnp.sin(ang)

    cr, sr = cs(t // GRID_W)
    cc, sc = cs(t % GRID_W)
    cos = jnp.concatenate([cr, cr, cc, cc], axis=-1)
    sin = jnp.concatenate([-sr, sr, -sc, sc], axis=-1)
    return jnp.tile(cos, (1, 2)), jnp.tile(sin, (1, 2))


def _cprep_kernel(q_ref, k_ref, v_ref, cos_ref, sin_ref, qg_ref, kg_ref, avg_ref, qo_ref, ko_ref, ve_ref, vo_ref):
    lane = lax.broadcasted_iota(jnp.int32, (1, LANES), 1)
    v = v_ref[...]
    ones = jnp.ones(v.shape, v.dtype)
    ve_ref[...] = jnp.where(lane < HEAD_DIM, v, ones)
    vo_ref[...] = jnp.where(lane < HEAD_DIM, ones, v)
    first = (lane & (HEAD_DIM // 4)) == 0
    cos = cos_ref[...]
    sin = sin_ref[...]
    avg = avg_ref[...]

    def norm_rope(x, gain):
        x = x.astype(F32)
        sq = x * x
        hi = sq.astype(BF16)
        r1 = sq - hi.astype(F32)
        mid = r1.astype(BF16)
        lo = (r1 - mid.astype(F32)).astype(BF16)
        ms = (jnp.dot(hi, avg, preferred_element_type=F32) + jnp.dot(mid, avg, preferred_element_type=F32)
              + jnp.dot(lo, avg, preferred_element_type=F32))
        y = x * lax.rsqrt(ms + RMS_EPS) * gain
        quarter = HEAD_DIM // 4
        partner = jnp.where(first, pltpu.roll(y, LANES - quarter, 1), pltpu.roll(y, quarter, 1))
        return y * cos + partner * sin

    for c in range(q_ref.shape[1] // LANES):
        cols = slice(c * LANES, (c + 1) * LANES)
        qo_ref[:, cols] = norm_rope(q_ref[:, cols], qg_ref[...]).astype(qo_ref.dtype)
    ko_ref[...] = norm_rope(k_ref[...], kg_ref[...]).astype(ko_ref.dtype)


def _cprep(p, cos, sin, q_gain, k_gain, batch, seq, tm=2048):
    t = batch * seq
    nt = seq // tm
    avg = np.zeros((LANES, LANES), np.float32)
    avg[:HEAD_DIM, :HEAD_DIM] = 1.0 / HEAD_DIM
    avg[HEAD_DIM:, HEAD_DIM:] = 1.0 / HEAD_DIM
    gw = C_QW // GA_KV_HEADS
    qg = jnp.tile(q_gain.astype(F32) * (HEAD_DIM ** -0.5 * np.log2(np.e)), 2).reshape(1, LANES)
    kg = jnp.tile(k_gain.astype(F32), 2).reshape(1, LANES)
    return pl.pallas_call(
        _cprep_kernel,
        grid=(t // tm, GA_KV_HEADS),
        in_specs=[pl.BlockSpec((tm, gw), lambda i, g: (i, COL_CQ // gw + g)),
                  pl.BlockSpec((tm, LANES), lambda i, g: (i, COL_CK // LANES + g)),
                  pl.BlockSpec((tm, LANES), lambda i, g: (i, COL_CV // LANES + g)),
                  pl.BlockSpec((tm, LANES), lambda i, g: (i % nt, 0)),
                  pl.BlockSpec((tm, LANES), lambda i, g: (i % nt, 0)),
                  pl.BlockSpec((1, LANES), lambda i, g: (0, 0)),
                  pl.BlockSpec((1, LANES), lambda i, g: (0, 0)),
                  pl.BlockSpec((LANES, LANES), lambda i, g: (0, 0))],
        out_specs=[pl.BlockSpec((tm, gw), lambda i, g: (i, g))] + [pl.BlockSpec((tm, LANES), lambda i, g: (i, g))] * 3,
        out_shape=[jax.ShapeDtypeStruct((t, C_QW), BF16)] + [jax.ShapeDtypeStruct((t, GA_KV_HEADS * LANES), BF16)] * 3,
        compiler_params=pltpu.CompilerParams(
            dimension_semantics=("parallel", "arbitrary"), vmem_limit_bytes=V7X_VMEM_LIMIT),
        name="cprep",
    )(p, p, p, cos, sin, qg, kg, jnp.asarray(avg, BF16))


def _flash_kernel(q_ref, k_ref, ve_ref, vo_ref, o_ref, qs_ref, m_ref, acc_ref, *, tk):
    nkv = k_ref.shape[0] // tk
    nheads = qs_ref.shape[0]
    lane, hm = _half_masks()
    for h in range(nheads):
        qs_ref[h] = q_ref[:, (h // 2) * LANES:(h // 2 + 1) * LANES] * hm[h % 2]
    m_ref[...] = jnp.full(m_ref.shape, NEG_INF, F32)
    acc_ref[...] = jnp.zeros(acc_ref.shape, F32)

    def body(t, carry):
        rows = pl.ds(pl.multiple_of(t * tk, tk), tk)
        k = k_ref[rows, :]
        vs = (ve_ref[rows, :], vo_ref[rows, :])
        for h in range(nheads):
            s = _dot_nt(qs_ref[h], k)
            m_old = m_ref[h]
            m_new = jnp.maximum(m_old, jnp.max(s, axis=-1, keepdims=True))
            alpha = jnp.exp2(m_old - m_new)
            p = jnp.concatenate([jnp.exp2(s[:, c * LANES:(c + 1) * LANES] - m_new) for c in range(tk // LANES)],
                                axis=1)
            acc_ref[h] = alpha * acc_ref[h] + jnp.dot(p.astype(BF16), vs[h % 2], preferred_element_type=F32)
            m_ref[h] = m_new
        return carry

    lax.fori_loop(0, nkv, body, 0)
    for pi in range(nheads // 2):
        even, odd = acc_ref[2 * pi], acc_ref[2 * pi + 1]
        even = even / pltpu.roll(even, HEAD_DIM, 1)
        odd = odd / pltpu.roll(odd, HEAD_DIM, 1)
        o_ref[:, pi * LANES:(pi + 1) * LANES] = jnp.where(lane < HEAD_DIM, even, odd).astype(o_ref.dtype)


def _flash(q, k, ve, vo, batch, seq, tq=2048, tk=1024):
    t = batch * seq
    nq = seq // tq
    gw = C_QW // GA_KV_HEADS
    nheads = GA_Q_HEADS // GA_KV_HEADS
    kv_spec = pl.BlockSpec((seq, LANES), lambda b, g, i: (b, g))
    return pl.pallas_call(
        functools.partial(_flash_kernel, tk=tk),
        grid=(batch, GA_KV_HEADS, nq),
        in_specs=[pl.BlockSpec((tq, gw), lambda b, g, i: (b * nq + i, g)), kv_spec, kv_spec, kv_spec],
        out_specs=pl.BlockSpec((tq, gw), lambda b, g, i: (b * nq + i, g)),
        out_shape=jax.ShapeDtypeStruct((t, C_QW), BF16),
        scratch_shapes=[pltpu.VMEM((nheads, tq, LANES), BF16), pltpu.VMEM((nheads, tq, LANES), F32),
                        pltpu.VMEM((nheads, tq, LANES), F32)],
        compiler_params=pltpu.CompilerParams(
            dimension_semantics=("parallel", "parallel", "arbitrary"), vmem_limit_bytes=V7X_VMEM_LIMIT),
        name="flash",
    )(q, k, ve, vo)


def _merge_kernel(x_ref, ya_ref, ob0_ref, ob1_ref, ob2_ref, ls0_ref, ls1_ref, ls2_ref, yc_ref, yd_ref,
                  ga_ref, gb_ref, gc_ref, gd_ref, wa_ref, wb_ref, wc_ref, wd_ref, wo_ref, gn_ref,
                  xo_ref, ho_ref, stage):
    tm = x_ref.shape[0]
    slots = iter(range(stage.shape[0]))

    def token_order(ref):
        dil = ref.shape[0]
        if dil == 1:
            return ref[0]
        slot = next(slots)
        for r in range(dil):
            for s in range(B_OUT // LANES):
                stage[slot, s, pl.ds(r, tm // dil, stride=dil), :] = ref[r, :, s * LANES:(s + 1) * LANES]
        return jnp.concatenate([stage[slot, s] for s in range(B_OUT // LANES)], axis=1)

    l0, l1, l2 = token_order(ls0_ref), token_order(ls1_ref), token_order(ls2_ref)
    mx = jnp.maximum(jnp.maximum(l0, l1), l2)
    e0, e1, e2 = jnp.exp2(l0 - mx), jnp.exp2(l1 - mx), jnp.exp2(l2 - mx)
    yb = (e0 * token_order(ob0_ref) + e1 * token_order(ob1_ref) + e2 * token_order(ob2_ref)) / (e0 + e1 + e2)

    def branch(g_ref, y, w_ref):
        return g_ref[...].astype(F32) * jnp.dot(y, w_ref[...], preferred_element_type=F32)

    merged = (branch(ga_ref, ya_ref[...], wa_ref) + branch(gb_ref, yb.astype(BF16), wb_ref)
              + branch(gc_ref, yc_ref[...], wc_ref) + branch(gd_ref, yd_ref[...], wd_ref))
    out = x_ref[...] + jnp.dot(merged.astype(BF16), wo_ref[...], preferred_element_type=F32)
    xo_ref[...] = out
    ho_ref[...] = _rms_rows(out, gn_ref[...]).astype(ho_ref.dtype)


def _merge(x, ya, obs, lss, yc, yd, gates, wa, wb, wc, wd, wo, gain, seq, tm=256):
    t, d = x.shape
    nb = seq // tm

    def rows(width, col=0):
        return pl.BlockSpec((tm, width), lambda i: (i, col))

    def classes(arr):
        dil = arr.shape[1]
        return pl.BlockSpec((None, dil, tm // dil, B_OUT), lambda i: (i // nb, 0, i % nb, 0))

    def const(arr):
        return pl.BlockSpec(arr.shape, lambda i: (0, 0), pipeline_mode=pl.Buffered(1))

    gain = gain.reshape(1, d)
    return pl.pallas_call(
        _merge_kernel,
        grid=(t // tm,),
        in_specs=[rows(d), rows(A_W)] + [classes(a) for a in obs] + [classes(a) for a in lss]
                 + [rows(C_QW), rows(D_QW), rows(d, 0), rows(d, 1), rows(d, 2), rows(d, 3),
                    const(wa), const(wb), const(wc), const(wd), const(wo), const(gain)],
        out_specs=[rows(d), rows(d)],
        out_shape=[jax.ShapeDtypeStruct((t, d), F32), jax.ShapeDtypeStruct((t, d), BF16)],
        scratch_shapes=[pltpu.VMEM((2 * len(DILATED), B_OUT // LANES, tm, LANES), F32)],
        compiler_params=pltpu.CompilerParams(dimension_semantics=("parallel",), vmem_limit_bytes=V7X_VMEM_LIMIT),
        name="merge",
    )(x, ya, *obs, *lss, yc, yd, gates, gates, gates, gates, wa, wb, wc, wd, wo, gain)


def _mlp_kernel(h_ref, x_ref, wu_ref, wd_ref, gn_ref, *rest, final, row_chunk):
    if final:
        yo_ref, acc_ref = rest
    else:
        xo_ref, ho_ref, acc_ref = rest
    f = pl.program_id(1)

    @pl.when(f == 0)
    def _():
        acc_ref[...] = x_ref[...]

    for r in range(h_ref.shape[0] // row_chunk):
        rows = pl.ds(r * row_chunk, row_chunk)
        u = jnp.dot(h_ref[rows, :], wu_ref[...], preferred_element_type=F32)
        a = jnp.square(jnp.maximum(u, 0.0)).astype(BF16)
        acc_ref[rows, :] += jnp.dot(a, wd_ref[...], preferred_element_type=F32)

    @pl.when(f == pl.num_programs(1) - 1)
    def _():
        out = acc_ref[...]
        normed = _rms_rows(out, gn_ref[...])
        if final:
            yo_ref[...] = normed
        else:
            xo_ref[...] = out
            ho_ref[...] = normed.astype(ho_ref.dtype)


def _mlp(h, x, wu, wd, gain, *, final, tm=512, tf=1024):
    t, d = x.shape
    ff = wu.shape[1]
    rows = pl.BlockSpec((tm, d), lambda i, f: (i, 0))
    if final:
        out_specs = [rows]
        out_shape = [jax.ShapeDtypeStruct((t, d), F32)]
    else:
        out_specs = [rows, rows]
        out_shape = [jax.ShapeDtypeStruct((t, d), F32), jax.ShapeDtypeStruct((t, d), BF16)]
    return pl.pallas_call(
        functools.partial(_mlp_kernel, final=final, row_chunk=256),
        grid=(t // tm, ff // tf),
        in_specs=[rows, rows,
                  pl.BlockSpec((d, tf), lambda i, f: (0, f)),
                  pl.BlockSpec((tf, d), lambda i, f: (f, 0)),
                  pl.BlockSpec((1, d), lambda i, f: (0, 0))],
        out_specs=out_specs,
        out_shape=out_shape,
        scratch_shapes=[pltpu.VMEM((tm, d), F32)],
        compiler_params=pltpu.CompilerParams(
            dimension_semantics=("parallel", "arbitrary"), vmem_limit_bytes=V7X_VMEM_LIMIT),
        name="mlp_final" if final else "mlp",
    )(h, x, wu, wd, gain.reshape(1, d))


def _prep_in_weights(w_in):
    offs = np.cumsum((0,) + SPLIT_SIZES)
    colscale = np.ones((offs[-1],), np.float32)
    for s in (0, 3, 9):
        colscale[offs[s]:offs[s + 1]] = HEAD_DIM ** -0.5 * LOG2E
    w_all = (w_in * colscale).astype(BF16)
    aq, ak, av, bq, bk, bv, cq, ck, cv, dq, dk, dv, gl = [w_all[..., offs[s]:offs[s + 1]] for s in range(13)]

    def dup(t):
        k0, k1 = t[..., :HEAD_DIM], t[..., HEAD_DIM:]
        return jnp.concatenate([k0, k0, k1, k1], axis=-1)

    b_groups = []
    for g in range(len(DIL_PATTERNS)):
        sl = slice(g * B_OUT, (g + 1) * B_OUT)
        b_groups += [bq[..., sl], bk[..., sl], bv[..., sl]]
    pad = jnp.zeros(w_in.shape[:-1] + (P_WIDTH - COL_BD - 2 * BD_WIDTH,), BF16)
    w_mix = jnp.concatenate([aq, ak, av, cq, dq] + b_groups[:3]
                            + [dup(ck), dup(cv), dup(dk), dup(dv)] + b_groups[3:] + [pad], axis=-1)
    return w_mix, gl


def _trunk(x, wts, tables):
    batch, seq, d = x.shape
    t = batch * seq
    depth = wts["w_mix"].shape[0]
    cos, sin = _rope_tables(seq)
    x2 = x.reshape(t, d)
    h = _rmsnorm(x2, wts["norm_mix"][0])
    b_slopes = _alibi_slopes(DIL_HEADS)
    d_table = _band_table(_alibi_slopes(SW_Q_HEADS), 1, SW_HALF_WINDOW, SW_KV_HEADS)
    for l in range(depth):
        p, *folded = _proj_mix(h, wts["w_mix"][l], batch, seq)
        gates = _proj(h, wts["w_gate"][l], sigmoid=True, name="proj_gate")
        ya = _natten(p, tables["natten"][l], batch, seq)
        obs, lss = [], []
        hg = DIL_HEADS_PER_GROUP
        p3 = p.reshape(batch, seq, P_WIDTH)
        for g, (win, dil) in enumerate(DIL_PATTERNS):
            if dil == 1:
                arr, col = p3, COL_B0
            else:
                arr, col = folded[DILATED.index(dil)].reshape(batch * dil, seq // dil, BD_WIDTH), 0
            ob, ls = _banded(arr, w=win // (2 * dil), col_q=col, col_k=col + B_OUT, col_v=col + 2 * B_OUT,
                             wq=B_OUT, wkv=B_OUT, kvmap=(0, 1),
                             table=_band_table(b_slopes[g * hg:(g + 1) * hg], dil, win // (2 * dil), 2),
                             sink=None, out_dtype=F32, has_lse=True, name="banded_d%d" % dil)
            obs.append(ob.reshape(batch, dil, seq // dil, B_OUT))
            lss.append(ls.reshape(batch, dil, seq // dil, B_OUT))
        qr, kr, ve, vo = _cprep(p, cos, sin, wts["q_gain_c"][l], wts["k_gain_c"][l], batch, seq)
        yc = _flash(qr, kr, ve, vo, batch, seq)
        (yd,) = _banded(p3, w=SW_HALF_WINDOW, col_q=COL_DQ, col_k=COL_DK, col_v=COL_DV, wq=D_QW, wkv=2 * D_KVW,
                        kvmap=(0, 1), table=d_table, sink=wts["sink_d"][l], out_dtype=BF16, has_lse=False,
                        name="banded_sink")
        yd = yd.reshape(t, D_QW)
        x2, h = _merge(x2, ya, obs, lss, yc, yd, gates, wts["w_a"][l], wts["w_b"][l], wts["w_c"][l],
                       wts["w_d"][l], wts["w_out"][l], wts["norm_mlp"][l], seq)
        if l + 1 < depth:
            x2, h = _mlp(h, x2, wts["w_up"][l], wts["w_down"][l], wts["norm_mix"][l + 1], final=False)
        else:
            (y,) = _mlp(h, x2, wts["w_up"][l], wts["w_down"][l], wts["norm_final"], final=True)
    return y.reshape(batch, seq, d)


def kernel(x_prompt, x_sample, norm_mix, w_in, rel_bias_a, q_gain_c, k_gain_c, sink_d, w_branch_a, w_branch_b,
           w_branch_c, w_branch_d, w_out, norm_mlp, w_up, w_down, norm_final):
    w_mix, w_gate = _prep_in_weights(w_in)
    wts = dict(norm_mix=norm_mix, w_mix=w_mix, w_gate=w_gate, q_gain_c=q_gain_c, k_gain_c=k_gain_c, sink_d=sink_d,
               w_a=w_branch_a.astype(BF16), w_b=w_branch_b.astype(BF16), w_c=w_branch_c.astype(BF16),
               w_d=w_branch_d.astype(BF16), w_out=w_out.astype(BF16), norm_mlp=norm_mlp,
               w_up=w_up.astype(BF16), w_down=w_down.astype(BF16), norm_final=norm_final)
    tables = dict(natten=jax.vmap(_natten_bias_table)(rel_bias_a))
    return (_trunk(x_prompt, wts, tables), _trunk(x_sample, wts, tables))
```

```python
import functools

import numpy as np
import jax
import jax.numpy as jnp
from jax import lax
from jax.experimental import pallas as pl
from jax.experimental.pallas import tpu as pltpu

F32 = jnp.float32
BF16 = jnp.bfloat16

D_MODEL = 2048
HEAD_DIM = 64
GRID_W = 64
NA_HEADS = 8
NA_ROWS = 8
NA_COLS = 16
DIL_PATTERNS = ((128, 1), (512, 4), (2048, 16))
DIL_HEADS_PER_GROUP = 4
DIL_HEADS = 12
DIL_BLOCK = 64
GA_Q_HEADS = 8
GA_KV_HEADS = 2
ROPE_THETA = 10000.0
SW_Q_HEADS = 8
SW_KV_HEADS = 2
SW_HALF_WINDOW = 128
N_BRANCHES = 4
D_FF = 4 * D_MODEL
RMS_EPS = 1e-6
NEG_INF = -1e30
LOG2E = float(np.log2(np.e))

A_W = NA_HEADS * HEAD_DIM
B_W = DIL_HEADS * HEAD_DIM
B_OUT = DIL_HEADS_PER_GROUP * HEAD_DIM
C_QW = GA_Q_HEADS * HEAD_DIM
C_KVW = GA_KV_HEADS * HEAD_DIM
D_QW = SW_Q_HEADS * HEAD_DIM
D_KVW = SW_KV_HEADS * HEAD_DIM
GATE_W = N_BRANCHES * D_MODEL
SPLIT_SIZES = (A_W, A_W, A_W, B_W, B_W, B_W, C_QW, C_KVW, C_KVW, D_QW, D_KVW, D_KVW, GATE_W)

LANES = 128
V7X_VMEM_LIMIT = 56 * 1024 * 1024

COL_AQ = 0
COL_AK = 512
COL_AV = 1024
COL_DK = 1536
COL_DV = 1664
COL_CQ = 1792
COL_B0Q = 2304
COL_DQ = 2560
COL_B0K = 3072
COL_B0V = 3328
COL_CK = 3584
COL_CV = 3712
COL_BD = 3840
P_WIDTH = 5376
MIX_TN = 1792
GATE_TN = 2048
BD_TILE = COL_BD // MIX_TN
BD_LOCAL = COL_BD % MIX_TN
BD_WIDTH = 3 * B_OUT
DILATED = tuple(d for _, d in DIL_PATTERNS if d > 1)


def _mix_segments():
    offs = dict(zip("aq ak av bq bk bv cq ck cv dq dk dv".split(), np.cumsum((0,) + SPLIT_SIZES)[:12]))
    b = lambda name, g: offs[name] + g * B_OUT
    segs = [(offs["aq"], A_W, True), (offs["ak"], A_W, False), (offs["av"], A_W, False),
            (offs["dk"], D_KVW, False), (offs["dv"], D_KVW, False),
            (offs["cq"], C_QW, False), (b("bq", 0), B_OUT, True), (offs["dq"], D_QW, True),
            (b("bk", 0), B_OUT, False), (b("bv", 0), B_OUT, False),
            (offs["ck"], C_KVW, False), (offs["cv"], C_KVW, False)]
    for g in (1, 2):
        segs += [(b("bq", g), B_OUT, True), (b("bk", g), B_OUT, False), (b("bv", g), B_OUT, False)]
    assert sum(s[1] for s in segs) == P_WIDTH
    return segs


def _alibi_slopes(n):
    return (2.0 ** (-8.0 * np.arange(1, n + 1) / n)).astype(np.float32)


def _half_masks():
    lane = lax.broadcasted_iota(jnp.int32, (1, LANES), 1)
    lo = jnp.where(lane < HEAD_DIM, 1.0, 0.0).astype(BF16)
    hi = jnp.where(lane >= HEAD_DIM, 1.0, 0.0).astype(BF16)
    return lane, (lo, hi)


def _dot_nt(a, b):
    return lax.dot_general(a, b, (((1,), (1,)), ((), ())), preferred_element_type=F32)


def _rms_rows(x, gain):
    ms = jnp.mean(x * x, axis=-1, keepdims=True)
    return x * lax.rsqrt(ms + RMS_EPS) * gain


def _rmsnorm_kernel(x_ref, g_ref, o_ref):
    o_ref[...] = _rms_rows(x_ref[...], g_ref[...]).astype(o_ref.dtype)


def _rmsnorm(x, gain, tm=512):
    t, d = x.shape
    return pl.pallas_call(
        _rmsnorm_kernel,
        grid=(t // tm,),
        in_specs=[pl.BlockSpec((tm, d), lambda i: (i, 0)), pl.BlockSpec((1, d), lambda i: (0, 0))],
        out_specs=pl.BlockSpec((tm, d), lambda i: (i, 0)),
        out_shape=jax.ShapeDtypeStruct((t, d), BF16),
        compiler_params=pltpu.CompilerParams(dimension_semantics=("parallel",), vmem_limit_bytes=V7X_VMEM_LIMIT),
        name="rmsnorm",
    )(x, gain.reshape(1, d))


def _proj_kernel(h_ref, w_ref, o_ref, *, row_chunk, sigmoid):
    for r in range(h_ref.shape[0] // row_chunk):
        rows = pl.ds(r * row_chunk, row_chunk)
        acc = jnp.dot(h_ref[rows, :], w_ref[...], preferred_element_type=F32)
        if sigmoid:
            acc = 1.0 / (1.0 + jnp.exp(-acc))
        o_ref[rows, :] = acc.astype(o_ref.dtype)


def _proj(h, w, *, sigmoid, tm=1024, tn=GATE_TN, name):
    t, k = h.shape
    n = w.shape[1]
    return pl.pallas_call(
        functools.partial(_proj_kernel, row_chunk=256, sigmoid=sigmoid),
        grid=(t // tm, n // tn),
        in_specs=[pl.BlockSpec((tm, k), lambda i, j: (i, 0)), pl.BlockSpec((k, tn), lambda i, j: (0, j))],
        out_specs=pl.BlockSpec((tm, tn), lambda i, j: (i, j)),
        out_shape=jax.ShapeDtypeStruct((t, n), BF16),
        compiler_params=pltpu.CompilerParams(
            dimension_semantics=("parallel", "arbitrary"), vmem_limit_bytes=V7X_VMEM_LIMIT),
        name=name,
    )(h, w)


def _proj_mix_kernel(h_ref, w_ref, o_ref, *rest, row_chunk):
    (f4_ref, f16_ref), (stage, stage4) = rest[:2], rest[2:]
    j = pl.program_id(1)
    tm = h_ref.shape[0]
    slabs = BD_WIDTH // LANES
    quarter = tm // 4
    for r in range(tm // row_chunk):
        rows = pl.ds(r * row_chunk, row_chunk)
        acc = jnp.dot(h_ref[rows, :], w_ref[...], preferred_element_type=F32)
        o_ref[rows, :] = acc.astype(o_ref.dtype)
        for s in range(2 * slabs):
            stage[s, rows, :] = acc[:, BD_LOCAL + s * LANES:BD_LOCAL + (s + 1) * LANES]

    @pl.when(j == BD_TILE)
    def _():
        for s in range(slabs):
            for r4 in range(4):
                f4_ref[r4, :, s * LANES:(s + 1) * LANES] = stage[s, pl.ds(r4, quarter, stride=4), :].astype(
                    f4_ref.dtype)
                stage4[s, r4 * quarter:(r4 + 1) * quarter, :] = stage[slabs + s, pl.ds(r4, quarter, stride=4), :]
        for s in range(slabs):
            for r4 in range(4):
                for r4b in range(4):
                    part = stage4[s, pl.ds(r4 * quarter + r4b, quarter // 4, stride=4), :]
                    f16_ref[4 * r4b + r4, :, s * LANES:(s + 1) * LANES] = part.astype(f16_ref.dtype)


def _proj_mix(h, w, batch, seq, tm=1024):
    assert DILATED == (4, 16), "the fold in _proj_mix_kernel is written for dilations 4 and 16"
    t, k = h.shape
    nb = seq // tm
    fold_specs = [pl.BlockSpec((None, dil, tm // dil, BD_WIDTH), lambda i, j: (i // nb, 0, i % nb, 0))
                  for dil in DILATED]
    fold_shapes = [jax.ShapeDtypeStruct((batch, dil, seq // dil, BD_WIDTH), BF16) for dil in DILATED]
    return pl.pallas_call(
        functools.partial(_proj_mix_kernel, row_chunk=256),
        grid=(t // tm, P_WIDTH // MIX_TN),
        in_specs=[pl.BlockSpec((tm, k), lambda i, j: (i, 0)), pl.BlockSpec((k, MIX_TN), lambda i, j: (0, j))],
        out_specs=[pl.BlockSpec((tm, MIX_TN), lambda i, j: (i, j))] + fold_specs,
        out_shape=[jax.ShapeDtypeStruct((t, P_WIDTH), BF16)] + fold_shapes,
        scratch_shapes=[pltpu.VMEM((2 * BD_WIDTH // LANES, tm, LANES), F32),
                        pltpu.VMEM((BD_WIDTH // LANES, tm, LANES), F32)],
        compiler_params=pltpu.CompilerParams(
            dimension_semantics=("parallel", "arbitrary"), vmem_limit_bytes=V7X_VMEM_LIMIT),
        name="proj_mix",
    )(h, w)


NA_STEP_ROWS = 8
NA_STEP_TOK = NA_STEP_ROWS * GRID_W
NA_KEYS = NA_ROWS * GRID_W


def _natten_bias_table(rel_bias):
    qc = np.arange(GRID_W)[:, None]
    kc = np.arange(GRID_W)[None, :]
    c0 = np.clip(qc - NA_COLS // 2, 0, GRID_W - NA_COLS)
    valid = (kc >= c0) & (kc < c0 + NA_COLS)
    dc = np.clip(kc - qc, -(NA_COLS - 1), NA_COLS - 1) + NA_COLS - 1
    shift = np.arange(NA_ROWS)[:, None]
    krow = np.arange(NA_ROWS)[None, :]
    dr = krow - shift + NA_ROWS - 1
    tbl = rel_bias[:, dr][:, :, :, dc]
    tbl = jnp.where(valid[None, None, None], tbl * LOG2E, NEG_INF)
    tbl = jnp.transpose(tbl, (1, 0, 3, 2, 4))
    return tbl.reshape(NA_ROWS, NA_HEADS // 2, 2 * GRID_W, NA_KEYS).astype(F32)


def _natten_kernel(q_ref, kp_ref, kc_ref, kn_ref, vp_ref, vc_ref, vn_ref, tbl_ref, o_ref, qs, kbuf, vbuf, *, n_rows):
    i = pl.program_id(1)
    for s, (kr, vr) in enumerate(((kp_ref, vp_ref), (kc_ref, vc_ref), (kn_ref, vn_ref))):
        kbuf[s * NA_STEP_TOK:(s + 1) * NA_STEP_TOK, :] = kr[...]
        vbuf[s * NA_STEP_TOK:(s + 1) * NA_STEP_TOK, :] = vr[...]
    lane, hm = _half_masks()
    for j in range(NA_STEP_ROWS):
        for pi in range(NA_HEADS // 2):
            qpair = q_ref[j * GRID_W:(j + 1) * GRID_W, pi * LANES:(pi + 1) * LANES]
            for e in range(2):
                qs[j, pi, e * GRID_W:(e + 1) * GRID_W, :] = qpair * hm[e]

    shifts, offs = [], []
    for j in range(NA_STEP_ROWS):
        r = i * NA_STEP_ROWS + j
        r0 = jnp.clip(r - NA_ROWS // 2, 0, n_rows - NA_ROWS)
        shifts.append(r - r0)
        offs.append(pl.multiple_of((r0 - (i - 1) * NA_STEP_ROWS) * GRID_W, GRID_W))
    for pi in range(NA_HEADS // 2):
        cols = slice(pi * LANES, (pi + 1) * LANES)
        s = jnp.concatenate([_dot_nt(qs[j, pi], kbuf[pl.ds(offs[j], NA_KEYS), cols]) + tbl_ref[shifts[j], pi]
                             for j in range(NA_STEP_ROWS)], axis=0)
        m = jnp.broadcast_to(jnp.max(s, axis=-1, keepdims=True), (s.shape[0], LANES))
        p = jnp.concatenate([jnp.exp2(s[:, k * LANES:(k + 1) * LANES] - m) for k in range(NA_KEYS // LANES)], axis=1)
        inv = 1.0 / jnp.broadcast_to(jnp.sum(p, axis=-1, keepdims=True), (s.shape[0], LANES))
        p = p.astype(BF16)
        for j in range(NA_STEP_ROWS):
            rows = slice(j * 2 * GRID_W, (j + 1) * 2 * GRID_W)
            o = jnp.dot(p[rows], vbuf[pl.ds(offs[j], NA_KEYS), cols], preferred_element_type=F32) * inv[rows]
            o_ref[j * GRID_W:(j + 1) * GRID_W, cols] = jnp.where(lane < HEAD_DIM, o[:GRID_W], o[GRID_W:]).astype(
                o_ref.dtype)


def _natten(p, tbl, batch, seq):
    t = batch * seq
    nb = seq // NA_STEP_TOK
    tok = NA_STEP_TOK

    def spec(col, which):
        cb = col // A_W
        if which == 0:
            return pl.BlockSpec((tok, A_W), lambda b, i: (b * nb + i, cb))
        if which < 0:
            return pl.BlockSpec((tok, A_W), lambda b, i: (b * nb + jnp.maximum(i - 1, 0), cb))
        return pl.BlockSpec((tok, A_W), lambda b, i: (b * nb + jnp.minimum(i + 1, nb - 1), cb))

    return pl.pallas_call(
        functools.partial(_natten_kernel, n_rows=seq // GRID_W),
        grid=(batch, nb),
        in_specs=[spec(COL_AQ, 0),
                  spec(COL_AK, -1), spec(COL_AK, 0), spec(COL_AK, 1),
                  spec(COL_AV, -1), spec(COL_AV, 0), spec(COL_AV, 1),
                  pl.BlockSpec(tbl.shape, lambda b, i: (0, 0, 0, 0))],
        out_specs=pl.BlockSpec((tok, A_W), lambda b, i: (b * nb + i, 0)),
        out_shape=jax.ShapeDtypeStruct((t, A_W), BF16),
        scratch_shapes=[pltpu.VMEM((NA_STEP_ROWS, NA_HEADS // 2, 2 * GRID_W, LANES), BF16),
                        pltpu.VMEM((3 * tok, A_W), BF16), pltpu.VMEM((3 * tok, A_W), BF16)],
        compiler_params=pltpu.CompilerParams(
            dimension_semantics=("parallel", "arbitrary"), vmem_limit_bytes=V7X_VMEM_LIMIT),
        name="natten",
    )(p, p, p, p, p, p, p, tbl)


BAND_SUB = 128


def _band_table(slopes, dist_scale, w, chains):
    delta = np.arange(BAND_SUB + 2 * w)[None, :] - w - np.arange(BAND_SUB)[:, None]
    bias = -slopes[:, None, None] * (dist_scale * np.abs(delta)).astype(np.float32)[None]
    tbl = np.where(np.abs(delta)[None] <= w, bias * LOG2E, NEG_INF).astype(np.float32)
    return tbl.reshape(chains, -1, BAND_SUB + 2 * w)


def _banded_kernel(*refs, w, tq, n, kvmap, has_sink, has_lse):
    refs = list(refs)
    sink_ref = refs.pop(0) if has_sink else None
    q_ref, kp_ref, kc_ref, kn_ref, vp_ref, vc_ref, vn_ref, tbl_ref, o_ref = refs[:9]
    lse_ref = refs[9] if has_lse else None
    qs, kbuf, vbuf = refs[-3:]
    sub = BAND_SUB
    nchains = len(kvmap)
    blocks = q_ref.shape[1] // LANES // nchains
    i = pl.program_id(1)
    lane, hm = _half_masks()

    def fill(buf, lo, hi, ref):
        if buf.shape[1] == ref.shape[1]:
            buf[lo:hi] = ref[...]
        else:
            x = ref[...].astype(F32)
            swapped = pltpu.roll(x, HEAD_DIM, 1)
            buf[lo:hi, 0:LANES] = jnp.where(lane < HEAD_DIM, x, swapped).astype(buf.dtype)
            buf[lo:hi, LANES:2 * LANES] = jnp.where(lane < HEAD_DIM, swapped, x).astype(buf.dtype)

    for buf, (prev_ref, cur_ref, next_ref) in ((kbuf, (kp_ref, kc_ref, kn_ref)), (vbuf, (vp_ref, vc_ref, vn_ref))):
        fill(buf, 0, w, prev_ref)
        fill(buf, w, w + tq, cur_ref)
        fill(buf, w + tq, 2 * w + tq, next_ref)
    for j in range(tq // sub):
        for pi in range(nchains * blocks):
            qpair = q_ref[j * sub:(j + 1) * sub, pi * LANES:(pi + 1) * LANES]
            for e in range(2):
                qs[j, (2 * pi + e) * sub:(2 * pi + e + 1) * sub, :] = qpair * hm[e]
    nsub = tq // sub
    rows = 2 * blocks * sub
    nkeys = sub + 2 * w
    range_bias = {}
    for j in {0, nsub - 1}:
        kpos = i * tq + j * sub - w + lax.broadcasted_iota(jnp.int32, (1, nkeys), 1)
        range_bias[j] = jnp.where(kpos < 0, NEG_INF, jnp.where(kpos >= n, NEG_INF, 0.0))
    for c, kv in enumerate(kvmap):
        kvcols = slice(kv * LANES, (kv + 1) * LANES)
        tbl = tbl_ref[c]
        parts = []
        for j in range(nsub):
            sj = _dot_nt(qs[j, c * rows:(c + 1) * rows, :], kbuf[j * sub:j * sub + nkeys, kvcols]) + tbl
            parts.append(sj + range_bias[j] if j in range_bias else sj)
        s = jnp.concatenate(parts, axis=0)
        m = jnp.max(s, axis=-1, keepdims=True)
        if has_sink:
            sink = jnp.concatenate([jnp.full((sub, LANES), sink_ref[c * 2 * blocks + h] * LOG2E, F32)
                                    for h in range(2 * blocks)] * nsub, axis=0)
            m = jnp.maximum(m, sink)
        else:
            m = jnp.broadcast_to(m, (m.shape[0], LANES))
        p = jnp.concatenate([jnp.exp2(s[:, k * LANES:(k + 1) * LANES] - m) for k in range(nkeys // LANES)], axis=1)
        l = jnp.sum(p, axis=-1, keepdims=True)
        if has_sink:
            l = l + jnp.exp2(sink - m)
        else:
            l = jnp.broadcast_to(l, (l.shape[0], LANES))
        inv = 1.0 / l
        lse = m + jnp.log2(l)
        p = p.astype(BF16)
        for j in range(nsub):
            jr = slice(j * rows, (j + 1) * rows)
            o = jnp.dot(p[jr], vbuf[j * sub:j * sub + nkeys, kvcols], preferred_element_type=F32) * inv[jr]
            for x in range(blocks):
                cols = slice((c * blocks + x) * LANES, (c * blocks + x + 1) * LANES)
                even, odd = slice(2 * x * sub, (2 * x + 1) * sub), slice((2 * x + 1) * sub, (2 * x + 2) * sub)
                o_ref[j * sub:(j + 1) * sub, cols] = jnp.where(lane < HEAD_DIM, o[even], o[odd]).astype(o_ref.dtype)
                if has_lse:
                    lse_j = lse[jr]
                    lse_ref[j * sub:(j + 1) * sub, cols] = jnp.where(lane < HEAD_DIM, lse_j[even], lse_j[odd])


def _banded(arr, *, w, col_q, col_k, col_v, wq, wkv, kvmap, table, sink, out_dtype, has_lse, name):
    assert w <= BAND_SUB
    classes, n, _ = arr.shape
    tq = min(512, n)
    sub = tq // w
    nblk = n // w

    def qspec(col, width):
        return pl.BlockSpec((None, tq, width), lambda c, i: (c, i, col // width))

    def prev_spec(col, width):
        return pl.BlockSpec((None, w, width), lambda c, i: (c, jnp.maximum(i * sub - 1, 0), col // width))

    def next_spec(col, width):
        return pl.BlockSpec((None, w, width), lambda c, i: (c, jnp.minimum((i + 1) * sub, nblk - 1), col // width))

    in_specs = [qspec(col_q, wq),
                prev_spec(col_k, wkv), qspec(col_k, wkv), next_spec(col_k, wkv),
                prev_spec(col_v, wkv), qspec(col_v, wkv), next_spec(col_v, wkv),
                pl.BlockSpec(table.shape, lambda c, i: (0, 0, 0))]
    args = [arr] * 7 + [jnp.asarray(table)]
    if sink is not None:
        in_specs = [pl.BlockSpec(memory_space=pltpu.SMEM)] + in_specs
        args = [sink.astype(F32)] + args
    out_block = pl.BlockSpec((None, tq, wq), lambda c, i: (c, i, 0))
    out_shape = [jax.ShapeDtypeStruct((classes, n, wq), out_dtype)]
    out_specs = [out_block]
    if has_lse:
        out_shape.append(jax.ShapeDtypeStruct((classes, n, wq), F32))
        out_specs.append(out_block)
    return pl.pallas_call(
        functools.partial(_banded_kernel, w=w, tq=tq, n=n, kvmap=kvmap, has_sink=sink is not None, has_lse=has_lse),
        grid=(classes, n // tq),
        in_specs=in_specs,
        out_specs=out_specs,
        out_shape=out_shape,
        scratch_shapes=[pltpu.VMEM((tq // BAND_SUB, 2 * (wq // LANES) * BAND_SUB, LANES), BF16),
                        pltpu.VMEM((tq + 2 * w, len(kvmap) * LANES), BF16),
                        pltpu.VMEM((tq + 2 * w, len(kvmap) * LANES), BF16)],
        compiler_params=pltpu.CompilerParams(
            dimension_semantics=("parallel", "arbitrary"), vmem_limit_bytes=V7X_VMEM_LIMIT),
        name=name,
    )(*args)


def _rope_tables(seq):
    t = jnp.arange(seq)
    npair = HEAD_DIM // 4
    freqs = ROPE_THETA ** (-jnp.arange(npair, dtype=F32) / npair)

    def cs(pos):
        ang = pos.astype(F32)[:, None] * freqs[None, :]
        return jnp.cos(ang), jnp.sin(ang)

    cr, sr = cs(t // GRID_W)
    cc, sc = cs(t % GRID_W)
    cos = jnp.concatenate([cr, cr, cc, cc], axis=-1)
    sin = jnp.concatenate([-sr, sr, -sc, sc], axis=-1)
    return jnp.tile(cos, (1, 2)), jnp.tile(sin, (1, 2))


def _cprep_kernel(q_ref, k_ref, v_ref, cos_ref, sin_ref, qg_ref, kg_ref, avg_ref, qo_ref, ko_ref, ve_ref, vo_ref):
    lane = lax.broadcasted_iota(jnp.int32, (1, LANES), 1)
    own_half = (lane // HEAD_DIM) == pl.program_id(1)

    def group_pair(x):
        x = x.astype(F32)
        return jnp.where(own_half, x, pltpu.roll(x, HEAD_DIM, 1))

    v = group_pair(v_ref[...])
    ve_ref[...] = jnp.where(lane < HEAD_DIM, v, 1.0).astype(ve_ref.dtype)
    vo_ref[...] = jnp.where(lane < HEAD_DIM, 1.0, v).astype(vo_ref.dtype)
    first = (lane & (HEAD_DIM // 4)) == 0
    cos = cos_ref[...]
    sin = sin_ref[...]
    avg = avg_ref[...]

    def norm_rope(x, gain):
        x = x.astype(F32)
        sq = x * x
        hi = sq.astype(BF16)
        r1 = sq - hi.astype(F32)
        mid = r1.astype(BF16)
        lo = (r1 - mid.astype(F32)).astype(BF16)
        ms = (jnp.dot(hi, avg, preferred_element_type=F32) + jnp.dot(mid, avg, preferred_element_type=F32)
              + jnp.dot(lo, avg, preferred_element_type=F32))
        y = x * lax.rsqrt(ms + RMS_EPS) * gain
        quarter = HEAD_DIM // 4
        partner = jnp.where(first, pltpu.roll(y, LANES - quarter, 1), pltpu.roll(y, quarter, 1))
        return y * cos + partner * sin

    for c in range(q_ref.shape[1] // LANES):
        cols = slice(c * LANES, (c + 1) * LANES)
        qo_ref[:, cols] = norm_rope(q_ref[:, cols], qg_ref[...]).astype(qo_ref.dtype)
    ko_ref[...] = norm_rope(group_pair(k_ref[...]), kg_ref[...]).astype(ko_ref.dtype)


def _cprep(p, cos, sin, q_gain, k_gain, batch, seq, tm=2048):
    t = batch * seq
    nt = seq // tm
    avg = np.zeros((LANES, LANES), np.float32)
    avg[:HEAD_DIM, :HEAD_DIM] = 1.0 / HEAD_DIM
    avg[HEAD_DIM:, HEAD_DIM:] = 1.0 / HEAD_DIM
    gw = C_QW // GA_KV_HEADS
    qg = jnp.tile(q_gain.astype(F32) * (HEAD_DIM ** -0.5 * np.log2(np.e)), 2).reshape(1, LANES)
    kg = jnp.tile(k_gain.astype(F32), 2).reshape(1, LANES)
    return pl.pallas_call(
        _cprep_kernel,
        grid=(t // tm, GA_KV_HEADS),
        in_specs=[pl.BlockSpec((tm, gw), lambda i, g: (i, COL_CQ // gw + g)),
                  pl.BlockSpec((tm, LANES), lambda i, g: (i, COL_CK // LANES)),
                  pl.BlockSpec((tm, LANES), lambda i, g: (i, COL_CV // LANES)),
                  pl.BlockSpec((tm, LANES), lambda i, g: (i % nt, 0)),
                  pl.BlockSpec((tm, LANES), lambda i, g: (i % nt, 0)),
                  pl.BlockSpec((1, LANES), lambda i, g: (0, 0)),
                  pl.BlockSpec((1, LANES), lambda i, g: (0, 0)),
                  pl.BlockSpec((LANES, LANES), lambda i, g: (0, 0))],
        out_specs=[pl.BlockSpec((tm, gw), lambda i, g: (i, g))] + [pl.BlockSpec((tm, LANES), lambda i, g: (i, g))] * 3,
        out_shape=[jax.ShapeDtypeStruct((t, C_QW), BF16)] + [jax.ShapeDtypeStruct((t, GA_KV_HEADS * LANES), BF16)] * 3,
        compiler_params=pltpu.CompilerParams(
            dimension_semantics=("parallel", "arbitrary"), vmem_limit_bytes=V7X_VMEM_LIMIT),
        name="cprep",
    )(p, p, p, cos, sin, qg, kg, jnp.asarray(avg, BF16))


def _flash_kernel(q_ref, k_ref, ve_ref, vo_ref, o_ref, qs_ref, m_ref, acc_ref, *, tk):
    nkv = k_ref.shape[0] // tk
    nheads = qs_ref.shape[0]
    lane, hm = _half_masks()
    for h in range(nheads):
        qs_ref[h] = q_ref[:, (h // 2) * LANES:(h // 2 + 1) * LANES] * hm[h % 2]
    m_ref[...] = jnp.full(m_ref.shape, NEG_INF, F32)
    acc_ref[...] = jnp.zeros(acc_ref.shape, F32)

    def body(t, carry):
        rows = pl.ds(pl.multiple_of(t * tk, tk), tk)
        k = k_ref[rows, :]
        vs = (ve_ref[rows, :], vo_ref[rows, :])
        for h in range(nheads):
            s = _dot_nt(qs_ref[h], k)
            m_old = m_ref[h]
            m_new = jnp.maximum(m_old, jnp.max(s, axis=-1, keepdims=True))
            alpha = jnp.exp2(m_old - m_new)
            p = jnp.concatenate([jnp.exp2(s[:, c * LANES:(c + 1) * LANES] - m_new) for c in range(tk // LANES)],
                                axis=1)
            acc_ref[h] = alpha * acc_ref[h] + jnp.dot(p.astype(BF16), vs[h % 2], preferred_element_type=F32)
            m_ref[h] = m_new
        return carry

    lax.fori_loop(0, nkv, body, 0)
    for pi in range(nheads // 2):
        even, odd = acc_ref[2 * pi], acc_ref[2 * pi + 1]
        even = even / pltpu.roll(even, HEAD_DIM, 1)
        odd = odd / pltpu.roll(odd, HEAD_DIM, 1)
        o_ref[:, pi * LANES:(pi + 1) * LANES] = jnp.where(lane < HEAD_DIM, even, odd).astype(o_ref.dtype)


def _flash(q, k, ve, vo, batch, seq, tq=2048, tk=1024):
    t = batch * seq
    nq = seq // tq
    gw = C_QW // GA_KV_HEADS
    nheads = GA_Q_HEADS // GA_KV_HEADS
    kv_spec = pl.BlockSpec((seq, LANES), lambda b, g, i: (b, g))
    return pl.pallas_call(
        functools.partial(_flash_kernel, tk=tk),
        grid=(batch, GA_KV_HEADS, nq),
        in_specs=[pl.BlockSpec((tq, gw), lambda b, g, i: (b * nq + i, g)), kv_spec, kv_spec, kv_spec],
        out_specs=pl.BlockSpec((tq, gw), lambda b, g, i: (b * nq + i, g)),
        out_shape=jax.ShapeDtypeStruct((t, C_QW), BF16),
        scratch_shapes=[pltpu.VMEM((nheads, tq, LANES), BF16), pltpu.VMEM((nheads, tq, LANES), F32),
                        pltpu.VMEM((nheads, tq, LANES), F32)],
        compiler_params=pltpu.CompilerParams(
            dimension_semantics=("parallel", "parallel", "arbitrary"), vmem_limit_bytes=V7X_VMEM_LIMIT),
        name="flash",
    )(q, k, ve, vo)


def _merge_kernel(x_ref, ya_ref, ob0_ref, ob1_ref, ob2_ref, ls0_ref, ls1_ref, ls2_ref, yc_ref, yd_ref,
                  ga_ref, gb_ref, gc_ref, gd_ref, wa_ref, wb_ref, wc_ref, wd_ref, wo_ref, gn_ref,
                  xo_ref, ho_ref, stage):
    tm = x_ref.shape[0]
    slots = iter(range(stage.shape[0]))

    def token_order(ref):
        dil = ref.shape[0]
        if dil == 1:
            return ref[0]
        slot = next(slots)
        for r in range(dil):
            for s in range(B_OUT // LANES):
                stage[slot, s, pl.ds(r, tm // dil, stride=dil), :] = ref[r, :, s * LANES:(s + 1) * LANES]
        return jnp.concatenate([stage[slot, s] for s in range(B_OUT // LANES)], axis=1)

    l0, l1, l2 = token_order(ls0_ref), token_order(ls1_ref), token_order(ls2_ref)
    mx = jnp.maximum(jnp.maximum(l0, l1), l2)
    e0, e1, e2 = jnp.exp2(l0 - mx), jnp.exp2(l1 - mx), jnp.exp2(l2 - mx)
    yb = (e0 * token_order(ob0_ref) + e1 * token_order(ob1_ref) + e2 * token_order(ob2_ref)) / (e0 + e1 + e2)

    def branch(g_ref, y, w_ref):
        return g_ref[...].astype(F32) * jnp.dot(y, w_ref[...], preferred_element_type=F32)

    merged = (branch(ga_ref, ya_ref[...], wa_ref) + branch(gb_ref, yb.astype(BF16), wb_ref)
              + branch(gc_ref, yc_ref[...], wc_ref) + branch(gd_ref, yd_ref[...], wd_ref))
    out = x_ref[...] + jnp.dot(merged.astype(BF16), wo_ref[...], preferred_element_type=F32)
    xo_ref[...] = out
    ho_ref[...] = _rms_rows(out, gn_ref[...]).astype(ho_ref.dtype)


def _merge(x, ya, obs, lss, yc, yd, gates, wa, wb, wc, wd, wo, gain, seq, tm=256):
    t, d = x.shape
    nb = seq // tm

    def rows(width, col=0):
        return pl.BlockSpec((tm, width), lambda i: (i, col))

    def classes(arr):
        dil = arr.shape[1]
        return pl.BlockSpec((None, dil, tm // dil, B_OUT), lambda i: (i // nb, 0, i % nb, 0))

    def const(arr):
        return pl.BlockSpec(arr.shape, lambda i: (0, 0), pipeline_mode=pl.Buffered(1))

    gain = gain.reshape(1, d)
    return pl.pallas_call(
        _merge_kernel,
        grid=(t // tm,),
        in_specs=[rows(d), rows(A_W)] + [classes(a) for a in obs] + [classes(a) for a in lss]
                 + [rows(C_QW), rows(D_QW), rows(d, 0), rows(d, 1), rows(d, 2), rows(d, 3),
                    const(wa), const(wb), const(wc), const(wd), const(wo), const(gain)],
        out_specs=[rows(d), rows(d)],
        out_shape=[jax.ShapeDtypeStruct((t, d), F32), jax.ShapeDtypeStruct((t, d), BF16)],
        scratch_shapes=[pltpu.VMEM((2 * len(DILATED), B_OUT // LANES, tm, LANES), F32)],
        compiler_params=pltpu.CompilerParams(dimension_semantics=("parallel",), vmem_limit_bytes=V7X_VMEM_LIMIT),
        name="merge",
    )(x, ya, *obs, *lss, yc, yd, gates, gates, gates, gates, wa, wb, wc, wd, wo, gain)


def _mlp_kernel(h_ref, x_ref, wu_ref, wd_ref, gn_ref, *rest, final, row_chunk):
    if final:
        yo_ref, acc_ref = rest
    else:
        xo_ref, ho_ref, acc_ref = rest
    f = pl.program_id(1)

    @pl.when(f == 0)
    def _():
        acc_ref[...] = x_ref[...]

    for r in range(h_ref.shape[0] // row_chunk):
        rows = pl.ds(r * row_chunk, row_chunk)
        u = jnp.dot(h_ref[rows, :], wu_ref[...], preferred_element_type=F32)
        a = jnp.square(jnp.maximum(u, 0.0)).astype(BF16)
        acc_ref[rows, :] += jnp.dot(a, wd_ref[...], preferred_element_type=F32)

    @pl.when(f == pl.num_programs(1) - 1)
    def _():
        out = acc_ref[...]
        normed = _rms_rows(out, gn_ref[...])
        if final:
            yo_ref[...] = normed
        else:
            xo_ref[...] = out
            ho_ref[...] = normed.astype(ho_ref.dtype)


def _mlp(h, x, wu, wd, gain, *, final, tm=512, tf=1024):
    t, d = x.shape
    ff = wu.shape[1]
    rows = pl.BlockSpec((tm, d), lambda i, f: (i, 0))
    if final:
        out_specs = [rows]
        out_shape = [jax.ShapeDtypeStruct((t, d), F32)]
    else:
        out_specs = [rows, rows]
        out_shape = [jax.ShapeDtypeStruct((t, d), F32), jax.ShapeDtypeStruct((t, d), BF16)]
    return pl.pallas_call(
        functools.partial(_mlp_kernel, final=final, row_chunk=256),
        grid=(t // tm, ff // tf),
        in_specs=[rows, rows,
                  pl.BlockSpec((d, tf), lambda i, f: (0, f)),
                  pl.BlockSpec((tf, d), lambda i, f: (f, 0)),
                  pl.BlockSpec((1, d), lambda i, f: (0, 0))],
        out_specs=out_specs,
        out_shape=out_shape,
        scratch_shapes=[pltpu.VMEM((tm, d), F32)],
        compiler_params=pltpu.CompilerParams(
            dimension_semantics=("parallel", "arbitrary"), vmem_limit_bytes=V7X_VMEM_LIMIT),
        name="mlp_final" if final else "mlp",
    )(h, x, wu, wd, gain.reshape(1, d))


def _prep_in_weights(w_in):
    src, scale = [], []
    for col, width, is_query in _mix_segments():
        src += [col // LANES + c for c in range(width // LANES)]
        scale += [HEAD_DIM ** -0.5 * LOG2E if is_query else 1.0] * width
    w_mix = _relayout_cols(w_in, np.asarray(src, np.int32), np.asarray(scale, np.float32), LANES)
    gate_blk = 256
    gate0 = sum(SPLIT_SIZES[:-1]) // gate_blk
    w_gate = _relayout_cols(w_in, gate0 + np.arange(GATE_W // gate_blk, dtype=np.int32),
                            np.ones((GATE_W,), np.float32), gate_blk)
    return w_mix, w_gate


def _relayout_kernel(src_ref, w_ref, s_ref, o_ref):
    o_ref[...] = (w_ref[...] * s_ref[...]).astype(o_ref.dtype)


def _relayout_cols(w, src_blocks, scale, blk):
    depth, k, _ = w.shape
    nout = len(src_blocks)
    return pl.pallas_call(
        _relayout_kernel,
        grid_spec=pltpu.PrefetchScalarGridSpec(
            num_scalar_prefetch=1,
            grid=(depth, nout),
            in_specs=[pl.BlockSpec((None, k, blk), lambda l, c, src: (l, 0, src[c])),
                      pl.BlockSpec((1, blk), lambda l, c, src: (0, c))],
            out_specs=pl.BlockSpec((None, k, blk), lambda l, c, src: (l, 0, c))),
        out_shape=jax.ShapeDtypeStruct((depth, k, nout * blk), BF16),
        compiler_params=pltpu.CompilerParams(
            dimension_semantics=("parallel", "arbitrary"), vmem_limit_bytes=V7X_VMEM_LIMIT),
        name="relayout_cols",
    )(jnp.asarray(src_blocks), w, jnp.asarray(scale).reshape(1, nout * blk))


def _trunk(x, wts, tables):
    batch, seq, d = x.shape
    t = batch * seq
    depth = wts["w_mix"].shape[0]
    cos, sin = _rope_tables(seq)
    x2 = x.reshape(t, d)
    h = _rmsnorm(x2, wts["norm_mix"][0])
    b_slopes = _alibi_slopes(DIL_HEADS)
    d_table = _band_table(_alibi_slopes(SW_Q_HEADS), 1, SW_HALF_WINDOW, SW_KV_HEADS)
    for l in range(depth):
        p, *folded = _proj_mix(h, wts["w_mix"][l], batch, seq)
        gates = _proj(h, wts["w_gate"][l], sigmoid=True, name="proj_gate")
        ya = _natten(p, tables["natten"][l], batch, seq)
        obs, lss = [], []
        hg = DIL_HEADS_PER_GROUP
        p3 = p.reshape(batch, seq, P_WIDTH)
        for g, (win, dil) in enumerate(DIL_PATTERNS):
            if dil == 1:
                arr, cols = p3, (COL_B0Q, COL_B0K, COL_B0V)
            else:
                arr = folded[DILATED.index(dil)].reshape(batch * dil, seq // dil, BD_WIDTH)
                cols = (0, B_OUT, 2 * B_OUT)
            ob, ls = _banded(arr, w=win // (2 * dil), col_q=cols[0], col_k=cols[1], col_v=cols[2],
                             wq=B_OUT, wkv=B_OUT, kvmap=(0, 1),
                             table=_band_table(b_slopes[g * hg:(g + 1) * hg], dil, win // (2 * dil), 2),
                             sink=None, out_dtype=F32, has_lse=True, name="banded_d%d" % dil)
            obs.append(ob.reshape(batch, dil, seq // dil, B_OUT))
            lss.append(ls.reshape(batch, dil, seq // dil, B_OUT))
        qr, kr, ve, vo = _cprep(p, cos, sin, wts["q_gain_c"][l], wts["k_gain_c"][l], batch, seq)
        yc = _flash(qr, kr, ve, vo, batch, seq)
        (yd,) = _banded(p3, w=SW_HALF_WINDOW, col_q=COL_DQ, col_k=COL_DK, col_v=COL_DV, wq=D_QW, wkv=D_KVW,
                        kvmap=(0, 1), table=d_table, sink=wts["sink_d"][l], out_dtype=BF16, has_lse=False,
                        name="banded_sink")
        yd = yd.reshape(t, D_QW)
        x2, h = _merge(x2, ya, obs, lss, yc, yd, gates, wts["w_a"][l], wts["w_b"][l], wts["w_c"][l],
                       wts["w_d"][l], wts["w_out"][l], wts["norm_mlp"][l], seq)
        if l + 1 < depth:
            x2, h = _mlp(h, x2, wts["w_up"][l], wts["w_down"][l], wts["norm_mix"][l + 1], final=False)
        else:
            (y,) = _mlp(h, x2, wts["w_up"][l], wts["w_down"][l], wts["norm_final"], final=True)
    return y.reshape(batch, seq, d)


def kernel(x_prompt, x_sample, norm_mix, w_in, rel_bias_a, q_gain_c, k_gain_c, sink_d, w_branch_a, w_branch_b,
           w_branch_c, w_branch_d, w_out, norm_mlp, w_up, w_down, norm_final):
    w_mix, w_gate = _prep_in_weights(w_in)
    wts = dict(norm_mix=norm_mix, w_mix=w_mix, w_gate=w_gate, q_gain_c=q_gain_c, k_gain_c=k_gain_c, sink_d=sink_d,
               w_a=w_branch_a.astype(BF16), w_b=w_branch_b.astype(BF16), w_c=w_branch_c.astype(BF16),
               w_d=w_branch_d.astype(BF16), w_out=w_out.astype(BF16), norm_mlp=norm_mlp,
               w_up=w_up.astype(BF16), w_down=w_down.astype(BF16), norm_final=norm_final)
    tables = dict(natten=jax.vmap(_natten_bias_table)(rel_bias_a))
    return (_trunk(x_prompt, wts, tables), _trunk(x_sample, wts, tables))
```

```python
import functools

import numpy as np
import jax
import jax.numpy as jnp
from jax import lax
from jax.experimental import pallas as pl
from jax.experimental.pallas import tpu as pltpu

F32 = jnp.float32
BF16 = jnp.bfloat16

D_MODEL = 2048
HEAD_DIM = 64
GRID_W = 64
NA_HEADS = 8
NA_ROWS = 8
NA_COLS = 16
DIL_PATTERNS = ((128, 1), (512, 4), (2048, 16))
DIL_HEADS_PER_GROUP = 4
DIL_HEADS = 12
DIL_BLOCK = 64
GA_Q_HEADS = 8
GA_KV_HEADS = 2
ROPE_THETA = 10000.0
SW_Q_HEADS = 8
SW_KV_HEADS = 2
SW_HALF_WINDOW = 128
N_BRANCHES = 4
D_FF = 4 * D_MODEL
RMS_EPS = 1e-6
NEG_INF = -1e30
LOG2E = float(np.log2(np.e))

A_W = NA_HEADS * HEAD_DIM
B_W = DIL_HEADS * HEAD_DIM
B_OUT = DIL_HEADS_PER_GROUP * HEAD_DIM
C_QW = GA_Q_HEADS * HEAD_DIM
C_KVW = GA_KV_HEADS * HEAD_DIM
D_QW = SW_Q_HEADS * HEAD_DIM
D_KVW = SW_KV_HEADS * HEAD_DIM
GATE_W = N_BRANCHES * D_MODEL
SPLIT_SIZES = (A_W, A_W, A_W, B_W, B_W, B_W, C_QW, C_KVW, C_KVW, D_QW, D_KVW, D_KVW, GATE_W)

LANES = 128
V7X_VMEM_LIMIT = 56 * 1024 * 1024

COL_AQ = 0
COL_AK = 512
COL_AV = 1024
COL_DK = 1536
COL_DV = 1664
COL_CQ = 1792
COL_B0Q = 2304
COL_DQ = 2560
COL_B0K = 3072
COL_B0V = 3328
COL_CK = 3584
COL_CV = 3712
COL_BD = 3840
P_WIDTH = 5376
MIX_TN = 1792
GATE_TN = 2048
BD_TILE = COL_BD // MIX_TN
BD_LOCAL = COL_BD % MIX_TN
BD_WIDTH = 3 * B_OUT
DILATED = tuple(d for _, d in DIL_PATTERNS if d > 1)


def _mix_segments():
    offs = dict(zip("aq ak av bq bk bv cq ck cv dq dk dv".split(), np.cumsum((0,) + SPLIT_SIZES)[:12]))
    b = lambda name, g: offs[name] + g * B_OUT
    segs = [(offs["aq"], A_W, True), (offs["ak"], A_W, False), (offs["av"], A_W, False),
            (offs["dk"], D_KVW, False), (offs["dv"], D_KVW, False),
            (offs["cq"], C_QW, False), (b("bq", 0), B_OUT, True), (offs["dq"], D_QW, True),
            (b("bk", 0), B_OUT, False), (b("bv", 0), B_OUT, False),
            (offs["ck"], C_KVW, False), (offs["cv"], C_KVW, False)]
    for g in (1, 2):
        segs += [(b("bq", g), B_OUT, True), (b("bk", g), B_OUT, False), (b("bv", g), B_OUT, False)]
    assert sum(s[1] for s in segs) == P_WIDTH
    return segs


def _alibi_slopes(n):
    return (2.0 ** (-8.0 * np.arange(1, n + 1) / n)).astype(np.float32)


def _half_masks():
    lane = lax.broadcasted_iota(jnp.int32, (1, LANES), 1)
    lo = jnp.where(lane < HEAD_DIM, 1.0, 0.0).astype(BF16)
    hi = jnp.where(lane >= HEAD_DIM, 1.0, 0.0).astype(BF16)
    return lane, (lo, hi)


def _dot_nt(a, b):
    return lax.dot_general(a, b, (((1,), (1,)), ((), ())), preferred_element_type=F32)


def _rms_rows(x, gain):
    ms = jnp.mean(x * x, axis=-1, keepdims=True)
    return x * lax.rsqrt(ms + RMS_EPS) * gain


def _rmsnorm_kernel(x_ref, g_ref, o_ref):
    o_ref[...] = _rms_rows(x_ref[...], g_ref[...]).astype(o_ref.dtype)


def _rmsnorm(x, gain, tm=512):
    t, d = x.shape
    return pl.pallas_call(
        _rmsnorm_kernel,
        grid=(t // tm,),
        in_specs=[pl.BlockSpec((tm, d), lambda i: (i, 0)), pl.BlockSpec((1, d), lambda i: (0, 0))],
        out_specs=pl.BlockSpec((tm, d), lambda i: (i, 0)),
        out_shape=jax.ShapeDtypeStruct((t, d), BF16),
        compiler_params=pltpu.CompilerParams(dimension_semantics=("parallel",), vmem_limit_bytes=V7X_VMEM_LIMIT),
        name="rmsnorm",
    )(x, gain.reshape(1, d))


def _proj_kernel(h_ref, w_ref, o_ref, *, row_chunk, sigmoid):
    for r in range(h_ref.shape[0] // row_chunk):
        rows = pl.ds(r * row_chunk, row_chunk)
        acc = jnp.dot(h_ref[rows, :], w_ref[...], preferred_element_type=F32)
        if sigmoid:
            acc = 1.0 / (1.0 + jnp.exp(-acc))
        o_ref[rows, :] = acc.astype(o_ref.dtype)


def _proj(h, w, layer, *, sigmoid, tm=1024, tn=GATE_TN, name):
    t, k = h.shape
    n = w.shape[2]
    return pl.pallas_call(
        functools.partial(_proj_kernel, row_chunk=256, sigmoid=sigmoid),
        grid=(t // tm, n // tn),
        in_specs=[pl.BlockSpec((tm, k), lambda i, j: (i, 0)),
                  pl.BlockSpec((None, k, tn), lambda i, j: (layer, 0, j))],
        out_specs=pl.BlockSpec((tm, tn), lambda i, j: (i, j)),
        out_shape=jax.ShapeDtypeStruct((t, n), BF16),
        compiler_params=pltpu.CompilerParams(
            dimension_semantics=("parallel", "arbitrary"), vmem_limit_bytes=V7X_VMEM_LIMIT),
        name=name,
    )(h, w)


def _proj_mix_kernel(h_ref, w_ref, o_ref, *rest, row_chunk):
    (f4_ref, f16_ref), (stage, stage4) = rest[:2], rest[2:]
    j = pl.program_id(1)
    tm = h_ref.shape[0]
    slabs = BD_WIDTH // LANES
    quarter = tm // 4
    for r in range(tm // row_chunk):
        rows = pl.ds(r * row_chunk, row_chunk)
        acc = jnp.dot(h_ref[rows, :], w_ref[...], preferred_element_type=F32)
        o_ref[rows, :] = acc.astype(o_ref.dtype)
        for s in range(2 * slabs):
            stage[s, rows, :] = acc[:, BD_LOCAL + s * LANES:BD_LOCAL + (s + 1) * LANES]

    @pl.when(j == BD_TILE)
    def _():
        for s in range(slabs):
            for r4 in range(4):
                f4_ref[r4, :, s * LANES:(s + 1) * LANES] = stage[s, pl.ds(r4, quarter, stride=4), :].astype(
                    f4_ref.dtype)
                stage4[s, r4 * quarter:(r4 + 1) * quarter, :] = stage[slabs + s, pl.ds(r4, quarter, stride=4), :]
        for s in range(slabs):
            for r4 in range(4):
                for r4b in range(4):
                    part = stage4[s, pl.ds(r4 * quarter + r4b, quarter // 4, stride=4), :]
                    f16_ref[4 * r4b + r4, :, s * LANES:(s + 1) * LANES] = part.astype(f16_ref.dtype)


def _proj_mix(h, w, layer, batch, seq, tm=1024):
    assert DILATED == (4, 16), "the fold in _proj_mix_kernel is written for dilations 4 and 16"
    t, k = h.shape
    nb = seq // tm
    fold_specs = [pl.BlockSpec((None, dil, tm // dil, BD_WIDTH), lambda i, j: (i // nb, 0, i % nb, 0))
                  for dil in DILATED]
    fold_shapes = [jax.ShapeDtypeStruct((batch, dil, seq // dil, BD_WIDTH), BF16) for dil in DILATED]
    return pl.pallas_call(
        functools.partial(_proj_mix_kernel, row_chunk=256),
        grid=(t // tm, P_WIDTH // MIX_TN),
        in_specs=[pl.BlockSpec((tm, k), lambda i, j: (i, 0)),
                  pl.BlockSpec((None, k, MIX_TN), lambda i, j: (layer, 0, j))],
        out_specs=[pl.BlockSpec((tm, MIX_TN), lambda i, j: (i, j))] + fold_specs,
        out_shape=[jax.ShapeDtypeStruct((t, P_WIDTH), BF16)] + fold_shapes,
        scratch_shapes=[pltpu.VMEM((2 * BD_WIDTH // LANES, tm, LANES), F32),
                        pltpu.VMEM((BD_WIDTH // LANES, tm, LANES), F32)],
        compiler_params=pltpu.CompilerParams(
            dimension_semantics=("parallel", "arbitrary"), vmem_limit_bytes=V7X_VMEM_LIMIT),
        name="proj_mix",
    )(h, w)


NA_STEP_ROWS = 8
NA_STEP_TOK = NA_STEP_ROWS * GRID_W
NA_KEYS = NA_ROWS * GRID_W


def _natten_bias_table(rel_bias):
    qc = np.arange(GRID_W)[:, None]
    kc = np.arange(GRID_W)[None, :]
    c0 = np.clip(qc - NA_COLS // 2, 0, GRID_W - NA_COLS)
    valid = (kc >= c0) & (kc < c0 + NA_COLS)
    heads, ndr, ndc = rel_bias.shape
    span = 2 * GRID_W
    left = GRID_W - NA_COLS
    padded = jnp.pad(rel_bias.astype(F32), ((0, 0), (0, 0), (left, span - ndc - left)))
    skew = jnp.tile(padded, (1, 1, GRID_W))[..., :GRID_W * (span - 1)].reshape(heads, ndr, GRID_W, span - 1)
    t = skew[..., GRID_W - 1:2 * GRID_W - 1]
    t = jnp.where(valid, t * LOG2E, NEG_INF).transpose(0, 2, 1, 3)
    tbl = jnp.stack([t[:, :, NA_ROWS - 1 - s:2 * NA_ROWS - 1 - s, :] for s in range(NA_ROWS)])
    return tbl.reshape(NA_ROWS, NA_HEADS // 2, 2 * GRID_W, NA_KEYS)


def _natten_kernel(q_ref, kp_ref, kc_ref, kn_ref, vp_ref, vc_ref, vn_ref, tbl_ref, o_ref, qs, kbuf, vbuf, *, n_rows):
    i = pl.program_id(1)
    for s, (kr, vr) in enumerate(((kp_ref, vp_ref), (kc_ref, vc_ref), (kn_ref, vn_ref))):
        kbuf[s * NA_STEP_TOK:(s + 1) * NA_STEP_TOK, :] = kr[...]
        vbuf[s * NA_STEP_TOK:(s + 1) * NA_STEP_TOK, :] = vr[...]
    lane, hm = _half_masks()
    for j in range(NA_STEP_ROWS):
        for pi in range(NA_HEADS // 2):
            qpair = q_ref[j * GRID_W:(j + 1) * GRID_W, pi * LANES:(pi + 1) * LANES]
            for e in range(2):
                qs[j, pi, e * GRID_W:(e + 1) * GRID_W, :] = qpair * hm[e]

    shifts, offs = [], []
    for j in range(NA_STEP_ROWS):
        r = i * NA_STEP_ROWS + j
        r0 = jnp.clip(r - NA_ROWS // 2, 0, n_rows - NA_ROWS)
        shifts.append(r - r0)
        offs.append(pl.multiple_of((r0 - (i - 1) * NA_STEP_ROWS) * GRID_W, GRID_W))
    for pi in range(NA_HEADS // 2):
        cols = slice(pi * LANES, (pi + 1) * LANES)
        s = jnp.concatenate([_dot_nt(qs[j, pi], kbuf[pl.ds(offs[j], NA_KEYS), cols]) + tbl_ref[shifts[j], pi]
                             for j in range(NA_STEP_ROWS)], axis=0)
        m = jnp.broadcast_to(jnp.max(s, axis=-1, keepdims=True), (s.shape[0], LANES))
        p = jnp.concatenate([jnp.exp2(s[:, k * LANES:(k + 1) * LANES] - m) for k in range(NA_KEYS // LANES)], axis=1)
        inv = 1.0 / jnp.broadcast_to(jnp.sum(p, axis=-1, keepdims=True), (s.shape[0], LANES))
        p = p.astype(BF16)
        for j in range(NA_STEP_ROWS):
            rows = slice(j * 2 * GRID_W, (j + 1) * 2 * GRID_W)
            o = jnp.dot(p[rows], vbuf[pl.ds(offs[j], NA_KEYS), cols], preferred_element_type=F32) * inv[rows]
            o_ref[j * GRID_W:(j + 1) * GRID_W, cols] = jnp.where(lane < HEAD_DIM, o[:GRID_W], o[GRID_W:]).astype(
                o_ref.dtype)


def _natten(p, tbl, layer, batch, seq):
    t = batch * seq
    nb = seq // NA_STEP_TOK
    tok = NA_STEP_TOK

    def spec(col, which):
        cb = col // A_W
        if which == 0:
            return pl.BlockSpec((tok, A_W), lambda b, i: (b * nb + i, cb))
        if which < 0:
            return pl.BlockSpec((tok, A_W), lambda b, i: (b * nb + jnp.maximum(i - 1, 0), cb))
        return pl.BlockSpec((tok, A_W), lambda b, i: (b * nb + jnp.minimum(i + 1, nb - 1), cb))

    return pl.pallas_call(
        functools.partial(_natten_kernel, n_rows=seq // GRID_W),
        grid=(batch, nb),
        in_specs=[spec(COL_AQ, 0),
                  spec(COL_AK, -1), spec(COL_AK, 0), spec(COL_AK, 1),
                  spec(COL_AV, -1), spec(COL_AV, 0), spec(COL_AV, 1),
                  pl.BlockSpec((None,) + tbl.shape[1:], lambda b, i: (layer, 0, 0, 0, 0))],
        out_specs=pl.BlockSpec((tok, A_W), lambda b, i: (b * nb + i, 0)),
        out_shape=jax.ShapeDtypeStruct((t, A_W), BF16),
        scratch_shapes=[pltpu.VMEM((NA_STEP_ROWS, NA_HEADS // 2, 2 * GRID_W, LANES), BF16),
                        pltpu.VMEM((3 * tok, A_W), BF16), pltpu.VMEM((3 * tok, A_W), BF16)],
        compiler_params=pltpu.CompilerParams(
            dimension_semantics=("parallel", "arbitrary"), vmem_limit_bytes=V7X_VMEM_LIMIT),
        name="natten",
    )(p, p, p, p, p, p, p, tbl)


BAND_SUB = 128


def _band_table(slopes, dist_scale, w, chains):
    delta = np.arange(BAND_SUB + 2 * w)[None, :] - w - np.arange(BAND_SUB)[:, None]
    bias = -slopes[:, None, None] * (dist_scale * np.abs(delta)).astype(np.float32)[None]
    tbl = np.where(np.abs(delta)[None] <= w, bias * LOG2E, NEG_INF).astype(np.float32)
    return tbl.reshape(chains, -1, BAND_SUB + 2 * w)


def _banded_kernel(*refs, w, tq, n, kvmap, has_sink, has_lse):
    refs = list(refs)
    sink_ref = refs.pop(0) if has_sink else None
    q_ref, kp_ref, kc_ref, kn_ref, vp_ref, vc_ref, vn_ref, tbl_ref, o_ref = refs[:9]
    lse_ref = refs[9] if has_lse else None
    qs, kbuf, vbuf = refs[-3:]
    sub = BAND_SUB
    nchains = len(kvmap)
    blocks = q_ref.shape[1] // LANES // nchains
    i = pl.program_id(1)
    lane, hm = _half_masks()

    def fill(buf, lo, hi, ref):
        if buf.shape[1] == ref.shape[1]:
            buf[lo:hi] = ref[...]
        else:
            x = ref[...].astype(F32)
            swapped = pltpu.roll(x, HEAD_DIM, 1)
            buf[lo:hi, 0:LANES] = jnp.where(lane < HEAD_DIM, x, swapped).astype(buf.dtype)
            buf[lo:hi, LANES:2 * LANES] = jnp.where(lane < HEAD_DIM, swapped, x).astype(buf.dtype)

    for buf, (prev_ref, cur_ref, next_ref) in ((kbuf, (kp_ref, kc_ref, kn_ref)), (vbuf, (vp_ref, vc_ref, vn_ref))):
        fill(buf, 0, w, prev_ref)
        fill(buf, w, w + tq, cur_ref)
        fill(buf, w + tq, 2 * w + tq, next_ref)
    for j in range(tq // sub):
        for pi in range(nchains * blocks):
            qpair = q_ref[j * sub:(j + 1) * sub, pi * LANES:(pi + 1) * LANES]
            for e in range(2):
                qs[j, (2 * pi + e) * sub:(2 * pi + e + 1) * sub, :] = qpair * hm[e]
    nsub = tq // sub
    rows = 2 * blocks * sub
    nkeys = sub + 2 * w
    range_bias = {}
    for j in {0, nsub - 1}:
        kpos = i * tq + j * sub - w + lax.broadcasted_iota(jnp.int32, (1, nkeys), 1)
        range_bias[j] = jnp.where(kpos < 0, NEG_INF, jnp.where(kpos >= n, NEG_INF, 0.0))
    for c, kv in enumerate(kvmap):
        kvcols = slice(kv * LANES, (kv + 1) * LANES)
        tbl = tbl_ref[c]
        parts = []
        for j in range(nsub):
            sj = _dot_nt(qs[j, c * rows:(c + 1) * rows, :], kbuf[j * sub:j * sub + nkeys, kvcols]) + tbl
            parts.append(sj + range_bias[j] if j in range_bias else sj)
        s = jnp.concatenate(parts, axis=0)
        m = jnp.max(s, axis=-1, keepdims=True)
        if has_sink:
            sink = jnp.concatenate([jnp.full((sub, LANES), sink_ref[c * 2 * blocks + h] * LOG2E, F32)
                                    for h in range(2 * blocks)] * nsub, axis=0)
            m = jnp.maximum(m, sink)
        else:
            m = jnp.broadcast_to(m, (m.shape[0], LANES))
        p = jnp.concatenate([jnp.exp2(s[:, k * LANES:(k + 1) * LANES] - m) for k in range(nkeys // LANES)], axis=1)
        l = jnp.sum(p, axis=-1, keepdims=True)
        if has_sink:
            l = l + jnp.exp2(sink - m)
        else:
            l = jnp.broadcast_to(l, (l.shape[0], LANES))
        inv = 1.0 / l
        lse = m + jnp.log2(l)
        p = p.astype(BF16)
        for j in range(nsub):
            jr = slice(j * rows, (j + 1) * rows)
            o = jnp.dot(p[jr], vbuf[j * sub:j * sub + nkeys, kvcols], preferred_element_type=F32) * inv[jr]
            for x in range(blocks):
                cols = slice((c * blocks + x) * LANES, (c * blocks + x + 1) * LANES)
                even, odd = slice(2 * x * sub, (2 * x + 1) * sub), slice((2 * x + 1) * sub, (2 * x + 2) * sub)
                o_ref[j * sub:(j + 1) * sub, cols] = jnp.where(lane < HEAD_DIM, o[even], o[odd]).astype(o_ref.dtype)
                if has_lse:
                    lse_j = lse[jr]
                    lse_ref[j * sub:(j + 1) * sub, cols] = jnp.where(lane < HEAD_DIM, lse_j[even], lse_j[odd])


def _banded(arr, *, w, col_q, col_k, col_v, wq, wkv, kvmap, table, sink, out_dtype, has_lse, name):
    assert w <= BAND_SUB
    classes, n, _ = arr.shape
    tq = min(512, n)
    sub = tq // w
    nblk = n // w

    def qspec(col, width):
        return pl.BlockSpec((None, tq, width), lambda c, i: (c, i, col // width))

    def prev_spec(col, width):
        return pl.BlockSpec((None, w, width), lambda c, i: (c, jnp.maximum(i * sub - 1, 0), col // width))

    def next_spec(col, width):
        return pl.BlockSpec((None, w, width), lambda c, i: (c, jnp.minimum((i + 1) * sub, nblk - 1), col // width))

    in_specs = [qspec(col_q, wq),
                prev_spec(col_k, wkv), qspec(col_k, wkv), next_spec(col_k, wkv),
                prev_spec(col_v, wkv), qspec(col_v, wkv), next_spec(col_v, wkv),
                pl.BlockSpec(table.shape, lambda c, i: (0, 0, 0))]
    args = [arr] * 7 + [jnp.asarray(table)]
    if sink is not None:
        in_specs = [pl.BlockSpec(memory_space=pltpu.SMEM)] + in_specs
        args = [sink.astype(F32)] + args
    out_block = pl.BlockSpec((None, tq, wq), lambda c, i: (c, i, 0))
    out_shape = [jax.ShapeDtypeStruct((classes, n, wq), out_dtype)]
    out_specs = [out_block]
    if has_lse:
        out_shape.append(jax.ShapeDtypeStruct((classes, n, wq), F32))
        out_specs.append(out_block)
    return pl.pallas_call(
        functools.partial(_banded_kernel, w=w, tq=tq, n=n, kvmap=kvmap, has_sink=sink is not None, has_lse=has_lse),
        grid=(classes, n // tq),
        in_specs=in_specs,
        out_specs=out_specs,
        out_shape=out_shape,
        scratch_shapes=[pltpu.VMEM((tq // BAND_SUB, 2 * (wq // LANES) * BAND_SUB, LANES), BF16),
                        pltpu.VMEM((tq + 2 * w, len(kvmap) * LANES), BF16),
                        pltpu.VMEM((tq + 2 * w, len(kvmap) * LANES), BF16)],
        compiler_params=pltpu.CompilerParams(
            dimension_semantics=("parallel", "arbitrary"), vmem_limit_bytes=V7X_VMEM_LIMIT),
        name=name,
    )(*args)


def _rope_tables(seq):
    t = jnp.arange(seq)
    npair = HEAD_DIM // 4
    freqs = ROPE_THETA ** (-jnp.arange(npair, dtype=F32) / npair)

    def cs(pos):
        ang = pos.astype(F32)[:, None] * freqs[None, :]
        return jnp.cos(ang), jnp.sin(ang)

    cr, sr = cs(t // GRID_W)
    cc, sc = cs(t % GRID_W)
    cos = jnp.concatenate([cr, cr, cc, cc], axis=-1)
    sin = jnp.concatenate([-sr, sr, -sc, sc], axis=-1)
    return jnp.tile(cos, (1, 2)), jnp.tile(sin, (1, 2))


def _cprep_kernel(q_ref, k_ref, v_ref, cos_ref, sin_ref, qg_ref, kg_ref, avg_ref, qo_ref, ko_ref, ve_ref, vo_ref):
    lane = lax.broadcasted_iota(jnp.int32, (1, LANES), 1)
    own_half = (lane // HEAD_DIM) == pl.program_id(1)

    def group_pair(x):
        x = x.astype(F32)
        return jnp.where(own_half, x, pltpu.roll(x, HEAD_DIM, 1))

    v = group_pair(v_ref[...])
    ve_ref[...] = jnp.where(lane < HEAD_DIM, v, 1.0).astype(ve_ref.dtype)
    vo_ref[...] = jnp.where(lane < HEAD_DIM, 1.0, v).astype(vo_ref.dtype)
    first = (lane & (HEAD_DIM // 4)) == 0
    cos = cos_ref[...]
    sin = sin_ref[...]
    avg = avg_ref[...]

    def norm_rope(x, gain):
        x = x.astype(F32)
        sq = x * x
        hi = sq.astype(BF16)
        r1 = sq - hi.astype(F32)
        mid = r1.astype(BF16)
        lo = (r1 - mid.astype(F32)).astype(BF16)
        ms = (jnp.dot(hi, avg, preferred_element_type=F32) + jnp.dot(mid, avg, preferred_element_type=F32)
              + jnp.dot(lo, avg, preferred_element_type=F32))
        y = x * lax.rsqrt(ms + RMS_EPS) * gain
        quarter = HEAD_DIM // 4
        partner = jnp.where(first, pltpu.roll(y, LANES - quarter, 1), pltpu.roll(y, quarter, 1))
        return y * cos + partner * sin

    for c in range(q_ref.shape[1] // LANES):
        cols = slice(c * LANES, (c + 1) * LANES)
        qo_ref[:, cols] = norm_rope(q_ref[:, cols], qg_ref[...]).astype(qo_ref.dtype)
    ko_ref[...] = norm_rope(group_pair(k_ref[...]), kg_ref[...]).astype(ko_ref.dtype)


def _cprep(p, cos, sin, q_gain, k_gain, batch, seq, tm=2048):
    t = batch * seq
    nt = seq // tm
    avg = np.zeros((LANES, LANES), np.float32)
    avg[:HEAD_DIM, :HEAD_DIM] = 1.0 / HEAD_DIM
    avg[HEAD_DIM:, HEAD_DIM:] = 1.0 / HEAD_DIM
    gw = C_QW // GA_KV_HEADS
    qg = jnp.tile(q_gain.astype(F32) * (HEAD_DIM ** -0.5 * np.log2(np.e)), 2).reshape(1, LANES)
    kg = jnp.tile(k_gain.astype(F32), 2).reshape(1, LANES)
    return pl.pallas_call(
        _cprep_kernel,
        grid=(t // tm, GA_KV_HEADS),
        in_specs=[pl.BlockSpec((tm, gw), lambda i, g: (i, COL_CQ // gw + g)),
                  pl.BlockSpec((tm, LANES), lambda i, g: (i, COL_CK // LANES)),
                  pl.BlockSpec((tm, LANES), lambda i, g: (i, COL_CV // LANES)),
                  pl.BlockSpec((tm, LANES), lambda i, g: (i % nt, 0)),
                  pl.BlockSpec((tm, LANES), lambda i, g: (i % nt, 0)),
                  pl.BlockSpec((1, LANES), lambda i, g: (0, 0)),
                  pl.BlockSpec((1, LANES), lambda i, g: (0, 0)),
                  pl.BlockSpec((LANES, LANES), lambda i, g: (0, 0))],
        out_specs=[pl.BlockSpec((tm, gw), lambda i, g: (i, g))] + [pl.BlockSpec((tm, LANES), lambda i, g: (i, g))] * 3,
        out_shape=[jax.ShapeDtypeStruct((t, C_QW), BF16)] + [jax.ShapeDtypeStruct((t, GA_KV_HEADS * LANES), BF16)] * 3,
        compiler_params=pltpu.CompilerParams(
            dimension_semantics=("parallel", "arbitrary"), vmem_limit_bytes=V7X_VMEM_LIMIT),
        name="cprep",
    )(p, p, p, cos, sin, qg, kg, jnp.asarray(avg, BF16))


def _flash_kernel(q_ref, k_ref, ve_ref, vo_ref, o_ref, qs_ref, m_ref, acc_ref, *, tk):
    nkv = k_ref.shape[0] // tk
    nheads = qs_ref.shape[0]
    lane, hm = _half_masks()
    for h in range(nheads):
        qs_ref[h] = q_ref[:, (h // 2) * LANES:(h // 2 + 1) * LANES] * hm[h % 2]
    m_ref[...] = jnp.full(m_ref.shape, NEG_INF, F32)
    acc_ref[...] = jnp.zeros(acc_ref.shape, F32)

    def body(t, carry):
        rows = pl.ds(pl.multiple_of(t * tk, tk), tk)
        k = k_ref[rows, :]
        vs = (ve_ref[rows, :], vo_ref[rows, :])
        for h in range(nheads):
            s = _dot_nt(qs_ref[h], k)
            m_old = m_ref[h]
            m_new = jnp.maximum(m_old, jnp.max(s, axis=-1, keepdims=True))
            alpha = jnp.exp2(m_old - m_new)
            p = jnp.concatenate([jnp.exp2(s[:, c * LANES:(c + 1) * LANES] - m_new) for c in range(tk // LANES)],
                                axis=1)
            acc_ref[h] = alpha * acc_ref[h] + jnp.dot(p.astype(BF16), vs[h % 2], preferred_element_type=F32)
            m_ref[h] = m_new
        return carry

    lax.fori_loop(0, nkv, body, 0)
    for pi in range(nheads // 2):
        even, odd = acc_ref[2 * pi], acc_ref[2 * pi + 1]
        even = even / pltpu.roll(even, HEAD_DIM, 1)
        odd = odd / pltpu.roll(odd, HEAD_DIM, 1)
        o_ref[:, pi * LANES:(pi + 1) * LANES] = jnp.where(lane < HEAD_DIM, even, odd).astype(o_ref.dtype)


def _flash(q, k, ve, vo, batch, seq, tq=2048, tk=1024):
    t = batch * seq
    nq = seq // tq
    gw = C_QW // GA_KV_HEADS
    nheads = GA_Q_HEADS // GA_KV_HEADS
    kv_spec = pl.BlockSpec((seq, LANES), lambda b, g, i: (b, g))
    return pl.pallas_call(
        functools.partial(_flash_kernel, tk=tk),
        grid=(batch, GA_KV_HEADS, nq),
        in_specs=[pl.BlockSpec((tq, gw), lambda b, g, i: (b * nq + i, g)), kv_spec, kv_spec, kv_spec],
        out_specs=pl.BlockSpec((tq, gw), lambda b, g, i: (b * nq + i, g)),
        out_shape=jax.ShapeDtypeStruct((t, C_QW), BF16),
        scratch_shapes=[pltpu.VMEM((nheads, tq, LANES), BF16), pltpu.VMEM((nheads, tq, LANES), F32),
                        pltpu.VMEM((nheads, tq, LANES), F32)],
        compiler_params=pltpu.CompilerParams(
            dimension_semantics=("parallel", "parallel", "arbitrary"), vmem_limit_bytes=V7X_VMEM_LIMIT),
        name="flash",
    )(q, k, ve, vo)


def _merge_kernel(x_ref, ya_ref, ob0_ref, ob1_ref, ob2_ref, ls0_ref, ls1_ref, ls2_ref, yc_ref, yd_ref,
                  ga_ref, gb_ref, gc_ref, gd_ref, wa_ref, wb_ref, wc_ref, wd_ref, wo_ref, gn_ref,
                  xo_ref, ho_ref, stage):
    tm = x_ref.shape[0]
    slots = iter(range(stage.shape[0]))

    def token_order(ref):
        dil = ref.shape[0]
        if dil == 1:
            return ref[0]
        slot = next(slots)
        for r in range(dil):
            for s in range(B_OUT // LANES):
                stage[slot, s, pl.ds(r, tm // dil, stride=dil), :] = ref[r, :, s * LANES:(s + 1) * LANES]
        return jnp.concatenate([stage[slot, s] for s in range(B_OUT // LANES)], axis=1)

    l0, l1, l2 = token_order(ls0_ref), token_order(ls1_ref), token_order(ls2_ref)
    mx = jnp.maximum(jnp.maximum(l0, l1), l2)
    e0, e1, e2 = jnp.exp2(l0 - mx), jnp.exp2(l1 - mx), jnp.exp2(l2 - mx)
    yb = (e0 * token_order(ob0_ref) + e1 * token_order(ob1_ref) + e2 * token_order(ob2_ref)) / (e0 + e1 + e2)

    def branch(g_ref, y, w_ref):
        return g_ref[...].astype(F32) * jnp.dot(y, w_ref[...], preferred_element_type=F32)

    merged = (branch(ga_ref, ya_ref[...], wa_ref) + branch(gb_ref, yb.astype(BF16), wb_ref)
              + branch(gc_ref, yc_ref[...], wc_ref) + branch(gd_ref, yd_ref[...], wd_ref))
    out = x_ref[...] + jnp.dot(merged.astype(BF16), wo_ref[...], preferred_element_type=F32)
    xo_ref[...] = out
    ho_ref[...] = _rms_rows(out, gn_ref[...]).astype(ho_ref.dtype)


def _merge(x, ya, obs, lss, yc, yd, gates, wa, wb, wc, wd, wo, gain, layer, seq, tm=256):
    t, d = x.shape
    nb = seq // tm

    def rows(width, col=0):
        return pl.BlockSpec((tm, width), lambda i: (i, col))

    def classes(arr):
        dil = arr.shape[1]
        return pl.BlockSpec((None, dil, tm // dil, B_OUT), lambda i: (i // nb, 0, i % nb, 0))

    def const(arr):
        return pl.BlockSpec((None,) + arr.shape[1:], lambda i: (layer, 0, 0), pipeline_mode=pl.Buffered(1))

    gain = gain.reshape(1, d)
    return pl.pallas_call(
        _merge_kernel,
        grid=(t // tm,),
        in_specs=[rows(d), rows(A_W)] + [classes(a) for a in obs] + [classes(a) for a in lss]
                 + [rows(C_QW), rows(D_QW), rows(d, 0), rows(d, 1), rows(d, 2), rows(d, 3),
                    const(wa), const(wb), const(wc), const(wd), const(wo),
                    pl.BlockSpec((1, d), lambda i: (0, 0), pipeline_mode=pl.Buffered(1))],
        out_specs=[rows(d), rows(d)],
        out_shape=[jax.ShapeDtypeStruct((t, d), F32), jax.ShapeDtypeStruct((t, d), BF16)],
        scratch_shapes=[pltpu.VMEM((2 * len(DILATED), B_OUT // LANES, tm, LANES), F32)],
        compiler_params=pltpu.CompilerParams(dimension_semantics=("parallel",), vmem_limit_bytes=V7X_VMEM_LIMIT),
        name="merge",
    )(x, ya, *obs, *lss, yc, yd, gates, gates, gates, gates, wa, wb, wc, wd, wo, gain)


def _mlp_kernel(h_ref, x_ref, wu_ref, wd_ref, gn_ref, *rest, final, row_chunk):
    if final:
        yo_ref, acc_ref = rest
    else:
        xo_ref, ho_ref, acc_ref = rest
    f = pl.program_id(1)

    @pl.when(f == 0)
    def _():
        acc_ref[...] = x_ref[...]

    for r in range(h_ref.shape[0] // row_chunk):
        rows = pl.ds(r * row_chunk, row_chunk)
        u = jnp.dot(h_ref[rows, :], wu_ref[...], preferred_element_type=F32)
        a = jnp.square(jnp.maximum(u, 0.0)).astype(BF16)
        acc_ref[rows, :] += jnp.dot(a, wd_ref[...], preferred_element_type=F32)

    @pl.when(f == pl.num_programs(1) - 1)
    def _():
        out = acc_ref[...]
        normed = _rms_rows(out, gn_ref[...])
        if final:
            yo_ref[...] = normed
        else:
            xo_ref[...] = out
            ho_ref[...] = normed.astype(ho_ref.dtype)


def _mlp(h, x, wu, wd, gain, layer, *, final, tm=512, tf=1024):
    t, d = x.shape
    ff = wu.shape[2]
    rows = pl.BlockSpec((tm, d), lambda i, f: (i, 0))
    if final:
        out_specs = [rows]
        out_shape = [jax.ShapeDtypeStruct((t, d), F32)]
    else:
        out_specs = [rows, rows]
        out_shape = [jax.ShapeDtypeStruct((t, d), F32), jax.ShapeDtypeStruct((t, d), BF16)]
    return pl.pallas_call(
        functools.partial(_mlp_kernel, final=final, row_chunk=256),
        grid=(t // tm, ff // tf),
        in_specs=[rows, rows,
                  pl.BlockSpec((None, d, tf), lambda i, f: (layer, 0, f)),
                  pl.BlockSpec((None, tf, d), lambda i, f: (layer, f, 0)),
                  pl.BlockSpec((1, d), lambda i, f: (0, 0))],
        out_specs=out_specs,
        out_shape=out_shape,
        scratch_shapes=[pltpu.VMEM((tm, d), F32)],
        compiler_params=pltpu.CompilerParams(
            dimension_semantics=("parallel", "arbitrary"), vmem_limit_bytes=V7X_VMEM_LIMIT),
        name="mlp_final" if final else "mlp",
    )(h, x, wu, wd, gain.reshape(1, d))


def _prep_in_weights(w_in):
    src, scale = [], []
    for col, width, is_query in _mix_segments():
        src += [col // LANES + c for c in range(width // LANES)]
        scale += [HEAD_DIM ** -0.5 * LOG2E if is_query else 1.0] * width
    w_mix = _relayout_cols(w_in, np.asarray(src, np.int32), np.asarray(scale, np.float32), LANES)
    gate_blk = 256
    gate0 = sum(SPLIT_SIZES[:-1]) // gate_blk
    w_gate = _relayout_cols(w_in, gate0 + np.arange(GATE_W // gate_blk, dtype=np.int32),
                            np.ones((GATE_W,), np.float32), gate_blk)
    return w_mix, w_gate


def _relayout_kernel(src_ref, w_ref, s_ref, o_ref):
    o_ref[...] = (w_ref[...] * s_ref[...]).astype(o_ref.dtype)


def _relayout_cols(w, src_blocks, scale, blk):
    depth, k, _ = w.shape
    nout = len(src_blocks)
    return pl.pallas_call(
        _relayout_kernel,
        grid_spec=pltpu.PrefetchScalarGridSpec(
            num_scalar_prefetch=1,
            grid=(depth, nout),
            in_specs=[pl.BlockSpec((None, k, blk), lambda l, c, src: (l, 0, src[c])),
                      pl.BlockSpec((1, blk), lambda l, c, src: (0, c))],
            out_specs=pl.BlockSpec((None, k, blk), lambda l, c, src: (l, 0, c))),
        out_shape=jax.ShapeDtypeStruct((depth, k, nout * blk), BF16),
        compiler_params=pltpu.CompilerParams(
            dimension_semantics=("parallel", "arbitrary"), vmem_limit_bytes=V7X_VMEM_LIMIT),
        name="relayout_cols",
    )(jnp.asarray(src_blocks), w, jnp.asarray(scale).reshape(1, nout * blk))


def _trunk(x, wts, tables):
    batch, seq, d = x.shape
    t = batch * seq
    depth = wts["w_mix"].shape[0]
    cos, sin = _rope_tables(seq)
    x2 = x.reshape(t, d)
    h = _rmsnorm(x2, wts["norm_mix"][0])
    b_slopes = _alibi_slopes(DIL_HEADS)
    d_table = _band_table(_alibi_slopes(SW_Q_HEADS), 1, SW_HALF_WINDOW, SW_KV_HEADS)
    for l in range(depth):
        p, *folded = _proj_mix(h, wts["w_mix"], l, batch, seq)
        gates = _proj(h, wts["w_gate"], l, sigmoid=True, name="proj_gate")
        ya = _natten(p, tables["natten"], l, batch, seq)
        obs, lss = [], []
        hg = DIL_HEADS_PER_GROUP
        p3 = p.reshape(batch, seq, P_WIDTH)
        for g, (win, dil) in enumerate(DIL_PATTERNS):
            if dil == 1:
                arr, cols = p3, (COL_B0Q, COL_B0K, COL_B0V)
            else:
                arr = folded[DILATED.index(dil)].reshape(batch * dil, seq // dil, BD_WIDTH)
                cols = (0, B_OUT, 2 * B_OUT)
            ob, ls = _banded(arr, w=win // (2 * dil), col_q=cols[0], col_k=cols[1], col_v=cols[2],
                             wq=B_OUT, wkv=B_OUT, kvmap=(0, 1),
                             table=_band_table(b_slopes[g * hg:(g + 1) * hg], dil, win // (2 * dil), 2),
                             sink=None, out_dtype=F32, has_lse=True, name="banded_d%d" % dil)
            obs.append(ob.reshape(batch, dil, seq // dil, B_OUT))
            lss.append(ls.reshape(batch, dil, seq // dil, B_OUT))
        qr, kr, ve, vo = _cprep(p, cos, sin, wts["q_gain_c"][l], wts["k_gain_c"][l], batch, seq)
        yc = _flash(qr, kr, ve, vo, batch, seq)
        (yd,) = _banded(p3, w=SW_HALF_WINDOW, col_q=COL_DQ, col_k=COL_DK, col_v=COL_DV, wq=D_QW, wkv=D_KVW,
                        kvmap=(0, 1), table=d_table, sink=wts["sink_d"][l], out_dtype=BF16, has_lse=False,
                        name="banded_sink")
        yd = yd.reshape(t, D_QW)
        x2, h = _merge(x2, ya, obs, lss, yc, yd, gates, wts["w_a"], wts["w_b"], wts["w_c"], wts["w_d"],
                       wts["w_out"], wts["norm_mlp"][l], l, seq)
        if l + 1 < depth:
            x2, h = _mlp(h, x2, wts["w_up"], wts["w_down"], wts["norm_mix"][l + 1], l, final=False)
        else:
            (y,) = _mlp(h, x2, wts["w_up"], wts["w_down"], wts["norm_final"], l, final=True)
    return y.reshape(batch, seq, d)


def kernel(x_prompt, x_sample, norm_mix, w_in, rel_bias_a, q_gain_c, k_gain_c, sink_d, w_branch_a, w_branch_b,
           w_branch_c, w_branch_d, w_out, norm_mlp, w_up, w_down, norm_final):
    w_mix, w_gate = _prep_in_weights(w_in)
    wts = dict(norm_mix=norm_mix, w_mix=w_mix, w_gate=w_gate, q_gain_c=q_gain_c, k_gain_c=k_gain_c, sink_d=sink_d,
               w_a=w_branch_a.astype(BF16), w_b=w_branch_b.astype(BF16), w_c=w_branch_c.astype(BF16),
               w_d=w_branch_d.astype(BF16), w_out=w_out.astype(BF16), norm_mlp=norm_mlp,
               w_up=w_up.astype(BF16), w_down=w_down.astype(BF16), norm_final=norm_final)
    tables = dict(natten=jax.vmap(_natten_bias_table)(rel_bias_a))
    return (_trunk(x_prompt, wts, tables), _trunk(x_sample, wts, tables))
```

```python
import functools

import numpy as np
import jax
import jax.numpy as jnp
from jax import lax
from jax.experimental import pallas as pl
from jax.experimental.pallas import tpu as pltpu

F32 = jnp.float32
BF16 = jnp.bfloat16

D_MODEL = 2048
HEAD_DIM = 64
GRID_W = 64
NA_HEADS = 8
NA_ROWS = 8
NA_COLS = 16
DIL_PATTERNS = ((128, 1), (512, 4), (2048, 16))
DIL_HEADS_PER_GROUP = 4
DIL_HEADS = 12
DIL_BLOCK = 64
GA_Q_HEADS = 8
GA_KV_HEADS = 2
ROPE_THETA = 10000.0
SW_Q_HEADS = 8
SW_KV_HEADS = 2
SW_HALF_WINDOW = 128
N_BRANCHES = 4
D_FF = 4 * D_MODEL
RMS_EPS = 1e-6
NEG_INF = -1e30
LOG2E = float(np.log2(np.e))

A_W = NA_HEADS * HEAD_DIM
B_W = DIL_HEADS * HEAD_DIM
B_OUT = DIL_HEADS_PER_GROUP * HEAD_DIM
C_QW = GA_Q_HEADS * HEAD_DIM
C_KVW = GA_KV_HEADS * HEAD_DIM
D_QW = SW_Q_HEADS * HEAD_DIM
D_KVW = SW_KV_HEADS * HEAD_DIM
GATE_W = N_BRANCHES * D_MODEL
SPLIT_SIZES = (A_W, A_W, A_W, B_W, B_W, B_W, C_QW, C_KVW, C_KVW, D_QW, D_KVW, D_KVW, GATE_W)

LANES = 128
V7X_VMEM_LIMIT = 56 * 1024 * 1024

COL_AQ = 0
COL_AK = 512
COL_AV = 1024
COL_DK = 1536
COL_DV = 1664
COL_CQ = 1792
COL_B0Q = 2304
COL_DQ = 2560
COL_B0K = 3072
COL_B0V = 3328
COL_CK = 3584
COL_CV = 3712
COL_BD = 3840
P_WIDTH = 5376
MIX_TN = 1792
GATE_TN = 2048
BD_TILE = COL_BD // MIX_TN
BD_LOCAL = COL_BD % MIX_TN
BD_WIDTH = 3 * B_OUT
DILATED = tuple(d for _, d in DIL_PATTERNS if d > 1)


def _mix_segments():
    offs = dict(zip("aq ak av bq bk bv cq ck cv dq dk dv".split(), np.cumsum((0,) + SPLIT_SIZES)[:12]))
    b = lambda name, g: offs[name] + g * B_OUT
    segs = [(offs["aq"], A_W, True), (offs["ak"], A_W, False), (offs["av"], A_W, False),
            (offs["dk"], D_KVW, False), (offs["dv"], D_KVW, False),
            (offs["cq"], C_QW, False), (b("bq", 0), B_OUT, True), (offs["dq"], D_QW, True),
            (b("bk", 0), B_OUT, False), (b("bv", 0), B_OUT, False),
            (offs["ck"], C_KVW, False), (offs["cv"], C_KVW, False)]
    for g in (1, 2):
        segs += [(b("bq", g), B_OUT, True), (b("bk", g), B_OUT, False), (b("bv", g), B_OUT, False)]
    assert sum(s[1] for s in segs) == P_WIDTH
    return segs


def _alibi_slopes(n):
    return (2.0 ** (-8.0 * np.arange(1, n + 1) / n)).astype(np.float32)


def _half_masks():
    lane = lax.broadcasted_iota(jnp.int32, (1, LANES), 1)
    lo = jnp.where(lane < HEAD_DIM, 1.0, 0.0).astype(BF16)
    hi = jnp.where(lane >= HEAD_DIM, 1.0, 0.0).astype(BF16)
    return lane, (lo, hi)


def _dot_nt(a, b):
    return lax.dot_general(a, b, (((1,), (1,)), ((), ())), preferred_element_type=F32)


def _rms_rows(x, gain):
    ms = jnp.mean(x * x, axis=-1, keepdims=True)
    return x * lax.rsqrt(ms + RMS_EPS) * gain


def _rmsnorm_kernel(x_ref, g_ref, o_ref):
    o_ref[...] = _rms_rows(x_ref[...], g_ref[...]).astype(o_ref.dtype)


def _rmsnorm(x, gain, tm=512):
    t, d = x.shape
    return pl.pallas_call(
        _rmsnorm_kernel,
        grid=(t // tm,),
        in_specs=[pl.BlockSpec((tm, d), lambda i: (i, 0)), pl.BlockSpec((1, d), lambda i: (0, 0))],
        out_specs=pl.BlockSpec((tm, d), lambda i: (i, 0)),
        out_shape=jax.ShapeDtypeStruct((t, d), BF16),
        compiler_params=pltpu.CompilerParams(dimension_semantics=("parallel",), vmem_limit_bytes=V7X_VMEM_LIMIT),
        name="rmsnorm",
    )(x, gain.reshape(1, d))


def _proj_kernel(h_ref, w_ref, o_ref, *, row_chunk, sigmoid):
    for r in range(h_ref.shape[0] // row_chunk):
        rows = pl.ds(r * row_chunk, row_chunk)
        acc = jnp.dot(h_ref[rows, :], w_ref[...], preferred_element_type=F32)
        if sigmoid:
            acc = 1.0 / (1.0 + jnp.exp(-acc))
        o_ref[rows, :] = acc.astype(o_ref.dtype)


def _proj(h, w, layer, *, sigmoid, tm=1024, tn=GATE_TN, name):
    t, k = h.shape
    n = w.shape[2]
    return pl.pallas_call(
        functools.partial(_proj_kernel, row_chunk=256, sigmoid=sigmoid),
        grid=(t // tm, n // tn),
        in_specs=[pl.BlockSpec((tm, k), lambda i, j: (i, 0)),
                  pl.BlockSpec((None, k, tn), lambda i, j: (layer, 0, j))],
        out_specs=pl.BlockSpec((tm, tn), lambda i, j: (i, j)),
        out_shape=jax.ShapeDtypeStruct((t, n), BF16),
        compiler_params=pltpu.CompilerParams(
            dimension_semantics=("parallel", "arbitrary"), vmem_limit_bytes=V7X_VMEM_LIMIT),
        name=name,
    )(h, w)


def _proj_mix_kernel(h_ref, w_ref, o_ref, *rest, row_chunk):
    (f4_ref, f16_ref), (stage, stage4) = rest[:2], rest[2:]
    j = pl.program_id(1)
    tm = h_ref.shape[0]
    slabs = BD_WIDTH // LANES
    quarter = tm // 4
    for r in range(tm // row_chunk):
        rows = pl.ds(r * row_chunk, row_chunk)
        acc = jnp.dot(h_ref[rows, :], w_ref[...], preferred_element_type=F32)
        o_ref[rows, :] = acc.astype(o_ref.dtype)
        for s in range(2 * slabs):
            stage[s, rows, :] = acc[:, BD_LOCAL + s * LANES:BD_LOCAL + (s + 1) * LANES]

    @pl.when(j == BD_TILE)
    def _():
        for s in range(slabs):
            for r4 in range(4):
                f4_ref[r4, :, s * LANES:(s + 1) * LANES] = stage[s, pl.ds(r4, quarter, stride=4), :].astype(
                    f4_ref.dtype)
                stage4[s, r4 * quarter:(r4 + 1) * quarter, :] = stage[slabs + s, pl.ds(r4, quarter, stride=4), :]
        for s in range(slabs):
            for r4 in range(4):
                for r4b in range(4):
                    part = stage4[s, pl.ds(r4 * quarter + r4b, quarter // 4, stride=4), :]
                    f16_ref[4 * r4b + r4, :, s * LANES:(s + 1) * LANES] = part.astype(f16_ref.dtype)


def _proj_mix(h, w, layer, batch, seq, tm=1024):
    assert DILATED == (4, 16), "the fold in _proj_mix_kernel is written for dilations 4 and 16"
    t, k = h.shape
    nb = seq // tm
    fold_specs = [pl.BlockSpec((None, dil, tm // dil, BD_WIDTH), lambda i, j: (i // nb, 0, i % nb, 0))
                  for dil in DILATED]
    fold_shapes = [jax.ShapeDtypeStruct((batch, dil, seq // dil, BD_WIDTH), BF16) for dil in DILATED]
    return pl.pallas_call(
        functools.partial(_proj_mix_kernel, row_chunk=256),
        grid=(t // tm, P_WIDTH // MIX_TN),
        in_specs=[pl.BlockSpec((tm, k), lambda i, j: (i, 0)),
                  pl.BlockSpec((None, k, MIX_TN), lambda i, j: (layer, 0, j))],
        out_specs=[pl.BlockSpec((tm, MIX_TN), lambda i, j: (i, j))] + fold_specs,
        out_shape=[jax.ShapeDtypeStruct((t, P_WIDTH), BF16)] + fold_shapes,
        scratch_shapes=[pltpu.VMEM((2 * BD_WIDTH // LANES, tm, LANES), F32),
                        pltpu.VMEM((BD_WIDTH // LANES, tm, LANES), F32)],
        compiler_params=pltpu.CompilerParams(
            dimension_semantics=("parallel", "arbitrary"), vmem_limit_bytes=V7X_VMEM_LIMIT),
        name="proj_mix",
    )(h, w)


NA_STEP_ROWS = 8
NA_STEP_TOK = NA_STEP_ROWS * GRID_W
NA_KEYS = NA_ROWS * GRID_W


def _natten_bias_table(rel_bias):
    qc = np.arange(GRID_W)[:, None]
    kc = np.arange(GRID_W)[None, :]
    c0 = np.clip(qc - NA_COLS // 2, 0, GRID_W - NA_COLS)
    valid = (kc >= c0) & (kc < c0 + NA_COLS)
    heads, ndr, ndc = rel_bias.shape
    span = 2 * GRID_W
    left = GRID_W - NA_COLS
    padded = jnp.pad(rel_bias.astype(F32), ((0, 0), (0, 0), (left, span - ndc - left)))
    skew = jnp.tile(padded, (1, 1, GRID_W))[..., :GRID_W * (span - 1)].reshape(heads, ndr, GRID_W, span - 1)
    t = skew[..., GRID_W - 1:2 * GRID_W - 1]
    t = jnp.where(valid, t * LOG2E, NEG_INF).transpose(0, 2, 1, 3)
    tbl = jnp.stack([t[:, :, NA_ROWS - 1 - s:2 * NA_ROWS - 1 - s, :] for s in range(NA_ROWS)])
    return tbl.reshape(NA_ROWS, NA_HEADS // 2, 2 * GRID_W, NA_KEYS)


def _natten_kernel(q_ref, kp_ref, kc_ref, kn_ref, vp_ref, vc_ref, vn_ref, tbl_ref, o_ref, qs, kbuf, vbuf, *, n_rows):
    i = pl.program_id(1)
    for s, (kr, vr) in enumerate(((kp_ref, vp_ref), (kc_ref, vc_ref), (kn_ref, vn_ref))):
        kbuf[s * NA_STEP_TOK:(s + 1) * NA_STEP_TOK, :] = kr[...]
        vbuf[s * NA_STEP_TOK:(s + 1) * NA_STEP_TOK, :] = vr[...]
    lane, hm = _half_masks()
    for j in range(NA_STEP_ROWS):
        for pi in range(NA_HEADS // 2):
            qpair = q_ref[j * GRID_W:(j + 1) * GRID_W, pi * LANES:(pi + 1) * LANES]
            for e in range(2):
                qs[j, pi, e * GRID_W:(e + 1) * GRID_W, :] = qpair * hm[e]

    shifts, offs = [], []
    for j in range(NA_STEP_ROWS):
        r = i * NA_STEP_ROWS + j
        r0 = jnp.clip(r - NA_ROWS // 2, 0, n_rows - NA_ROWS)
        shifts.append(r - r0)
        offs.append(pl.multiple_of((r0 - (i - 1) * NA_STEP_ROWS) * GRID_W, GRID_W))
    for pi in range(NA_HEADS // 2):
        cols = slice(pi * LANES, (pi + 1) * LANES)
        s = jnp.concatenate([_dot_nt(qs[j, pi], kbuf[pl.ds(offs[j], NA_KEYS), cols]) + tbl_ref[shifts[j], pi]
                             for j in range(NA_STEP_ROWS)], axis=0)
        m = jnp.broadcast_to(jnp.max(s, axis=-1, keepdims=True), (s.shape[0], LANES))
        p = jnp.concatenate([jnp.exp2(s[:, k * LANES:(k + 1) * LANES] - m) for k in range(NA_KEYS // LANES)], axis=1)
        inv = 1.0 / jnp.broadcast_to(jnp.sum(p, axis=-1, keepdims=True), (s.shape[0], LANES))
        p = p.astype(BF16)
        for j in range(NA_STEP_ROWS):
            rows = slice(j * 2 * GRID_W, (j + 1) * 2 * GRID_W)
            o = jnp.dot(p[rows], vbuf[pl.ds(offs[j], NA_KEYS), cols], preferred_element_type=F32) * inv[rows]
            o_ref[j * GRID_W:(j + 1) * GRID_W, cols] = jnp.where(lane < HEAD_DIM, o[:GRID_W], o[GRID_W:]).astype(
                o_ref.dtype)


def _natten(p, tbl, layer, batch, seq):
    t = batch * seq
    nb = seq // NA_STEP_TOK
    tok = NA_STEP_TOK

    def spec(col, which):
        cb = col // A_W
        if which == 0:
            return pl.BlockSpec((tok, A_W), lambda b, i: (b * nb + i, cb))
        if which < 0:
            return pl.BlockSpec((tok, A_W), lambda b, i: (b * nb + jnp.maximum(i - 1, 0), cb))
        return pl.BlockSpec((tok, A_W), lambda b, i: (b * nb + jnp.minimum(i + 1, nb - 1), cb))

    return pl.pallas_call(
        functools.partial(_natten_kernel, n_rows=seq // GRID_W),
        grid=(batch, nb),
        in_specs=[spec(COL_AQ, 0),
                  spec(COL_AK, -1), spec(COL_AK, 0), spec(COL_AK, 1),
                  spec(COL_AV, -1), spec(COL_AV, 0), spec(COL_AV, 1),
                  pl.BlockSpec((None,) + tbl.shape[1:], lambda b, i: (layer, 0, 0, 0, 0))],
        out_specs=pl.BlockSpec((tok, A_W), lambda b, i: (b * nb + i, 0)),
        out_shape=jax.ShapeDtypeStruct((t, A_W), BF16),
        scratch_shapes=[pltpu.VMEM((NA_STEP_ROWS, NA_HEADS // 2, 2 * GRID_W, LANES), BF16),
                        pltpu.VMEM((3 * tok, A_W), BF16), pltpu.VMEM((3 * tok, A_W), BF16)],
        compiler_params=pltpu.CompilerParams(
            dimension_semantics=("parallel", "arbitrary"), vmem_limit_bytes=V7X_VMEM_LIMIT),
        name="natten",
    )(p, p, p, p, p, p, p, tbl)


BAND_SUB = 128


def _band_table(slopes, dist_scale, w, chains):
    delta = np.arange(BAND_SUB + 2 * w)[None, :] - w - np.arange(BAND_SUB)[:, None]
    bias = -slopes[:, None, None] * (dist_scale * np.abs(delta)).astype(np.float32)[None]
    tbl = np.where(np.abs(delta)[None] <= w, bias * LOG2E, NEG_INF).astype(np.float32)
    return tbl.reshape(chains, -1, BAND_SUB + 2 * w)


def _banded_kernel(*refs, w, tq, n, kvmap, has_sink, has_lse):
    refs = list(refs)
    sink_ref = refs.pop(0) if has_sink else None
    q_ref, kp_ref, kc_ref, kn_ref, vp_ref, vc_ref, vn_ref, tbl_ref, o_ref = refs[:9]
    lse_ref = refs[9] if has_lse else None
    qs, kbuf, vbuf = refs[-3:]
    sub = BAND_SUB
    nchains = len(kvmap)
    blocks = q_ref.shape[1] // LANES // nchains
    i = pl.program_id(1)
    lane, hm = _half_masks()

    def fill(buf, lo, hi, ref):
        if buf.shape[1] == ref.shape[1]:
            buf[lo:hi] = ref[...]
        else:
            x = ref[...].astype(F32)
            swapped = pltpu.roll(x, HEAD_DIM, 1)
            buf[lo:hi, 0:LANES] = jnp.where(lane < HEAD_DIM, x, swapped).astype(buf.dtype)
            buf[lo:hi, LANES:2 * LANES] = jnp.where(lane < HEAD_DIM, swapped, x).astype(buf.dtype)

    for buf, (prev_ref, cur_ref, next_ref) in ((kbuf, (kp_ref, kc_ref, kn_ref)), (vbuf, (vp_ref, vc_ref, vn_ref))):
        fill(buf, 0, w, prev_ref)
        fill(buf, w, w + tq, cur_ref)
        fill(buf, w + tq, 2 * w + tq, next_ref)
    for j in range(tq // sub):
        for pi in range(nchains * blocks):
            qpair = q_ref[j * sub:(j + 1) * sub, pi * LANES:(pi + 1) * LANES]
            for e in range(2):
                qs[j, (2 * pi + e) * sub:(2 * pi + e + 1) * sub, :] = qpair * hm[e]
    nsub = tq // sub
    rows = 2 * blocks * sub
    nkeys = sub + 2 * w
    range_bias = {}
    for j in {0, nsub - 1}:
        kpos = i * tq + j * sub - w + lax.broadcasted_iota(jnp.int32, (1, nkeys), 1)
        range_bias[j] = jnp.where(kpos < 0, NEG_INF, jnp.where(kpos >= n, NEG_INF, 0.0))
    for c, kv in enumerate(kvmap):
        kvcols = slice(kv * LANES, (kv + 1) * LANES)
        tbl = tbl_ref[c]
        parts = []
        for j in range(nsub):
            sj = _dot_nt(qs[j, c * rows:(c + 1) * rows, :], kbuf[j * sub:j * sub + nkeys, kvcols]) + tbl
            parts.append(sj + range_bias[j] if j in range_bias else sj)
        s = jnp.concatenate(parts, axis=0)
        m = jnp.max(s, axis=-1, keepdims=True)
        if has_sink:
            sink = jnp.concatenate([jnp.full((sub, LANES), sink_ref[c * 2 * blocks + h] * LOG2E, F32)
                                    for h in range(2 * blocks)] * nsub, axis=0)
            m = jnp.maximum(m, sink)
        else:
            m = jnp.broadcast_to(m, (m.shape[0], LANES))
        p = jnp.concatenate([jnp.exp2(s[:, k * LANES:(k + 1) * LANES] - m) for k in range(nkeys // LANES)], axis=1)
        l = jnp.sum(p, axis=-1, keepdims=True)
        if has_sink:
            l = l + jnp.exp2(sink - m)
        else:
            l = jnp.broadcast_to(l, (l.shape[0], LANES))
        inv = 1.0 / l
        lse = m + jnp.log2(l)
        p = p.astype(BF16)
        for j in range(nsub):
            jr = slice(j * rows, (j + 1) * rows)
            o = jnp.dot(p[jr], vbuf[j * sub:j * sub + nkeys, kvcols], preferred_element_type=F32) * inv[jr]
            for x in range(blocks):
                cols = slice((c * blocks + x) * LANES, (c * blocks + x + 1) * LANES)
                even, odd = slice(2 * x * sub, (2 * x + 1) * sub), slice((2 * x + 1) * sub, (2 * x + 2) * sub)
                o_ref[j * sub:(j + 1) * sub, cols] = jnp.where(lane < HEAD_DIM, o[even], o[odd]).astype(o_ref.dtype)
                if has_lse:
                    lse_j = lse[jr]
                    lse_ref[j * sub:(j + 1) * sub, cols] = jnp.where(lane < HEAD_DIM, lse_j[even], lse_j[odd])


def _banded(arr, *, w, col_q, col_k, col_v, wq, wkv, kvmap, table, sink, out_dtype, has_lse, name):
    assert w <= BAND_SUB
    classes, n, _ = arr.shape
    tq = min(512, n)
    sub = tq // w
    nblk = n // w

    def qspec(col, width):
        return pl.BlockSpec((None, tq, width), lambda c, i: (c, i, col // width))

    def prev_spec(col, width):
        return pl.BlockSpec((None, w, width), lambda c, i: (c, jnp.maximum(i * sub - 1, 0), col // width))

    def next_spec(col, width):
        return pl.BlockSpec((None, w, width), lambda c, i: (c, jnp.minimum((i + 1) * sub, nblk - 1), col // width))

    in_specs = [qspec(col_q, wq),
                prev_spec(col_k, wkv), qspec(col_k, wkv), next_spec(col_k, wkv),
                prev_spec(col_v, wkv), qspec(col_v, wkv), next_spec(col_v, wkv),
                pl.BlockSpec(table.shape, lambda c, i: (0, 0, 0))]
    args = [arr] * 7 + [jnp.asarray(table)]
    if sink is not None:
        in_specs = [pl.BlockSpec(memory_space=pltpu.SMEM)] + in_specs
        args = [sink.astype(F32)] + args
    out_block = pl.BlockSpec((None, tq, wq), lambda c, i: (c, i, 0))
    out_shape = [jax.ShapeDtypeStruct((classes, n, wq), out_dtype)]
    out_specs = [out_block]
    if has_lse:
        out_shape.append(jax.ShapeDtypeStruct((classes, n, wq), F32))
        out_specs.append(out_block)
    return pl.pallas_call(
        functools.partial(_banded_kernel, w=w, tq=tq, n=n, kvmap=kvmap, has_sink=sink is not None, has_lse=has_lse),
        grid=(classes, n // tq),
        in_specs=in_specs,
        out_specs=out_specs,
        out_shape=out_shape,
        scratch_shapes=[pltpu.VMEM((tq // BAND_SUB, 2 * (wq // LANES) * BAND_SUB, LANES), BF16),
                        pltpu.VMEM((tq + 2 * w, len(kvmap) * LANES), BF16),
                        pltpu.VMEM((tq + 2 * w, len(kvmap) * LANES), BF16)],
        compiler_params=pltpu.CompilerParams(
            dimension_semantics=("parallel", "arbitrary"), vmem_limit_bytes=V7X_VMEM_LIMIT),
        name=name,
    )(*args)


def _rope_tables(seq):
    t = jnp.arange(seq)
    npair = HEAD_DIM // 4
    freqs = ROPE_THETA ** (-jnp.arange(npair, dtype=F32) / npair)

    def cs(pos):
        ang = pos.astype(F32)[:, None] * freqs[None, :]
        return jnp.cos(ang), jnp.sin(ang)

    cr, sr = cs(t // GRID_W)
    cc, sc = cs(t % GRID_W)
    cos = jnp.concatenate([cr, cr, cc, cc], axis=-1)
    sin = jnp.concatenate([-sr, sr, -sc, sc], axis=-1)
    return jnp.tile(cos, (1, 2)), jnp.tile(sin, (1, 2))


def _cprep_kernel(q_ref, k_ref, v_ref, cos_ref, sin_ref, qg_ref, kg_ref, avg_ref, qo_ref, ko_ref, ve_ref, vo_ref):
    lane = lax.broadcasted_iota(jnp.int32, (1, LANES), 1)
    own_half = (lane // HEAD_DIM) == pl.program_id(1)

    def group_pair(x):
        x = x.astype(F32)
        return jnp.where(own_half, x, pltpu.roll(x, HEAD_DIM, 1))

    v = group_pair(v_ref[...])
    ve_ref[...] = jnp.where(lane < HEAD_DIM, v, 1.0).astype(ve_ref.dtype)
    vo_ref[...] = jnp.where(lane < HEAD_DIM, 1.0, v).astype(vo_ref.dtype)
    first = (lane & (HEAD_DIM // 4)) == 0
    cos = cos_ref[...]
    sin = sin_ref[...]
    avg = avg_ref[...]

    def norm_rope(x, gain):
        x = x.astype(F32)
        sq = x * x
        hi = sq.astype(BF16)
        r1 = sq - hi.astype(F32)
        mid = r1.astype(BF16)
        lo = (r1 - mid.astype(F32)).astype(BF16)
        ms = (jnp.dot(hi, avg, preferred_element_type=F32) + jnp.dot(mid, avg, preferred_element_type=F32)
              + jnp.dot(lo, avg, preferred_element_type=F32))
        y = x * lax.rsqrt(ms + RMS_EPS) * gain
        quarter = HEAD_DIM // 4
        partner = jnp.where(first, pltpu.roll(y, LANES - quarter, 1), pltpu.roll(y, quarter, 1))
        return y * cos + partner * sin

    for c in range(q_ref.shape[1] // LANES):
        cols = slice(c * LANES, (c + 1) * LANES)
        qo_ref[:, cols] = norm_rope(q_ref[:, cols], qg_ref[...]).astype(qo_ref.dtype)
    ko_ref[...] = norm_rope(group_pair(k_ref[...]), kg_ref[...]).astype(ko_ref.dtype)


def _cprep(p, cos, sin, q_gain, k_gain, batch, seq, tm=2048):
    t = batch * seq
    nt = seq // tm
    avg = np.zeros((LANES, LANES), np.float32)
    avg[:HEAD_DIM, :HEAD_DIM] = 1.0 / HEAD_DIM
    avg[HEAD_DIM:, HEAD_DIM:] = 1.0 / HEAD_DIM
    gw = C_QW // GA_KV_HEADS
    qg = jnp.tile(q_gain.astype(F32) * (HEAD_DIM ** -0.5 * np.log2(np.e)), 2).reshape(1, LANES)
    kg = jnp.tile(k_gain.astype(F32), 2).reshape(1, LANES)
    return pl.pallas_call(
        _cprep_kernel,
        grid=(t // tm, GA_KV_HEADS),
        in_specs=[pl.BlockSpec((tm, gw), lambda i, g: (i, COL_CQ // gw + g)),
                  pl.BlockSpec((tm, LANES), lambda i, g: (i, COL_CK // LANES)),
                  pl.BlockSpec((tm, LANES), lambda i, g: (i, COL_CV // LANES)),
                  pl.BlockSpec((tm, LANES), lambda i, g: (i % nt, 0)),
                  pl.BlockSpec((tm, LANES), lambda i, g: (i % nt, 0)),
                  pl.BlockSpec((1, LANES), lambda i, g: (0, 0)),
                  pl.BlockSpec((1, LANES), lambda i, g: (0, 0)),
                  pl.BlockSpec((LANES, LANES), lambda i, g: (0, 0))],
        out_specs=[pl.BlockSpec((tm, gw), lambda i, g: (i, g))] + [pl.BlockSpec((tm, LANES), lambda i, g: (i, g))] * 3,
        out_shape=[jax.ShapeDtypeStruct((t, C_QW), BF16)] + [jax.ShapeDtypeStruct((t, GA_KV_HEADS * LANES), BF16)] * 3,
        compiler_params=pltpu.CompilerParams(
            dimension_semantics=("parallel", "arbitrary"), vmem_limit_bytes=V7X_VMEM_LIMIT),
        name="cprep",
    )(p, p, p, cos, sin, qg, kg, jnp.asarray(avg, BF16))


def _flash_kernel(q_ref, k_ref, ve_ref, vo_ref, o_ref, qs_ref, m_ref, acc_ref, *, tk):
    nkv = k_ref.shape[0] // tk
    nheads = qs_ref.shape[0]
    lane, hm = _half_masks()
    for h in range(nheads):
        qs_ref[h] = q_ref[:, (h // 2) * LANES:(h // 2 + 1) * LANES] * hm[h % 2]
    m_ref[...] = jnp.full(m_ref.shape, NEG_INF, F32)
    acc_ref[...] = jnp.zeros(acc_ref.shape, F32)

    def body(t, carry):
        rows = pl.ds(pl.multiple_of(t * tk, tk), tk)
        k = k_ref[rows, :]
        vs = (ve_ref[rows, :], vo_ref[rows, :])
        for h in range(nheads):
            s = _dot_nt(qs_ref[h], k)
            m_old = m_ref[h]
            m_new = jnp.maximum(m_old, jnp.max(s, axis=-1, keepdims=True))
            alpha = jnp.exp2(m_old - m_new)
            p = jnp.concatenate([jnp.exp2(s[:, c * LANES:(c + 1) * LANES] - m_new) for c in range(tk // LANES)],
                                axis=1)
            acc_ref[h] = alpha * acc_ref[h] + jnp.dot(p.astype(BF16), vs[h % 2], preferred_element_type=F32)
            m_ref[h] = m_new
        return carry

    lax.fori_loop(0, nkv, body, 0)
    for pi in range(nheads // 2):
        even, odd = acc_ref[2 * pi], acc_ref[2 * pi + 1]
        even = even / pltpu.roll(even, HEAD_DIM, 1)
        odd = odd / pltpu.roll(odd, HEAD_DIM, 1)
        o_ref[:, pi * LANES:(pi + 1) * LANES] = jnp.where(lane < HEAD_DIM, even, odd).astype(o_ref.dtype)


def _flash(q, k, ve, vo, batch, seq, tq=2048, tk=1024):
    t = batch * seq
    nq = seq // tq
    gw = C_QW // GA_KV_HEADS
    nheads = GA_Q_HEADS // GA_KV_HEADS
    kv_spec = pl.BlockSpec((seq, LANES), lambda b, g, i: (b, g))
    return pl.pallas_call(
        functools.partial(_flash_kernel, tk=tk),
        grid=(batch, GA_KV_HEADS, nq),
        in_specs=[pl.BlockSpec((tq, gw), lambda b, g, i: (b * nq + i, g)), kv_spec, kv_spec, kv_spec],
        out_specs=pl.BlockSpec((tq, gw), lambda b, g, i: (b * nq + i, g)),
        out_shape=jax.ShapeDtypeStruct((t, C_QW), BF16),
        scratch_shapes=[pltpu.VMEM((nheads, tq, LANES), BF16), pltpu.VMEM((nheads, tq, LANES), F32),
                        pltpu.VMEM((nheads, tq, LANES), F32)],
        compiler_params=pltpu.CompilerParams(
            dimension_semantics=("parallel", "parallel", "arbitrary"), vmem_limit_bytes=V7X_VMEM_LIMIT),
        name="flash",
    )(q, k, ve, vo)


def _merge_kernel(x_ref, ya_ref, ob0_ref, ob1_ref, ob2_ref, ls0_ref, ls1_ref, ls2_ref, yc_ref, yd_ref,
                  ga_ref, gb_ref, gc_ref, gd_ref, wa_ref, wb_ref, wc_ref, wd_ref, wo_ref, gn_ref,
                  xo_ref, ho_ref, stage):
    tm = x_ref.shape[0]
    slots = iter(range(stage.shape[0]))

    def token_order(ref):
        dil = ref.shape[0]
        if dil == 1:
            return ref[0]
        slot = next(slots)
        for r in range(dil):
            for s in range(B_OUT // LANES):
                stage[slot, s, pl.ds(r, tm // dil, stride=dil), :] = ref[r, :, s * LANES:(s + 1) * LANES]
        return jnp.concatenate([stage[slot, s] for s in range(B_OUT // LANES)], axis=1)

    l0, l1, l2 = token_order(ls0_ref), token_order(ls1_ref), token_order(ls2_ref)
    mx = jnp.maximum(jnp.maximum(l0, l1), l2)
    e0, e1, e2 = jnp.exp2(l0 - mx), jnp.exp2(l1 - mx), jnp.exp2(l2 - mx)
    yb = (e0 * token_order(ob0_ref) + e1 * token_order(ob1_ref) + e2 * token_order(ob2_ref)) / (e0 + e1 + e2)

    def branch(g_ref, y, w_ref):
        return g_ref[...].astype(F32) * jnp.dot(y, w_ref[...], preferred_element_type=F32)

    merged = (branch(ga_ref, ya_ref[...], wa_ref) + branch(gb_ref, yb.astype(BF16), wb_ref)
              + branch(gc_ref, yc_ref[...], wc_ref) + branch(gd_ref, yd_ref[...], wd_ref))
    out = x_ref[...] + jnp.dot(merged.astype(BF16), wo_ref[...], preferred_element_type=F32)
    xo_ref[...] = out
    ho_ref[...] = _rms_rows(out, gn_ref[...]).astype(ho_ref.dtype)


def _merge(x, ya, obs, lss, yc, yd, gates, wa, wb, wc, wd, wo, gain, layer, seq, tm=256):
    t, d = x.shape
    nb = seq // tm

    def rows(width, col=0):
        return pl.BlockSpec((tm, width), lambda i: (i, col))

    def classes(arr):
        dil = arr.shape[1]
        return pl.BlockSpec((None, dil, tm // dil, B_OUT), lambda i: (i // nb, 0, i % nb, 0))

    def const(arr):
        return pl.BlockSpec((None,) + arr.shape[1:], lambda i: (layer, 0, 0), pipeline_mode=pl.Buffered(1))

    gain = gain.reshape(1, d)
    return pl.pallas_call(
        _merge_kernel,
        grid=(t // tm,),
        in_specs=[rows(d), rows(A_W)] + [classes(a) for a in obs] + [classes(a) for a in lss]
                 + [rows(C_QW), rows(D_QW), rows(d, 0), rows(d, 1), rows(d, 2), rows(d, 3),
                    const(wa), const(wb), const(wc), const(wd), const(wo),
                    pl.BlockSpec((1, d), lambda i: (0, 0), pipeline_mode=pl.Buffered(1))],
        out_specs=[rows(d), rows(d)],
        out_shape=[jax.ShapeDtypeStruct((t, d), F32), jax.ShapeDtypeStruct((t, d), BF16)],
        scratch_shapes=[pltpu.VMEM((2 * len(DILATED), B_OUT // LANES, tm, LANES), F32)],
        compiler_params=pltpu.CompilerParams(dimension_semantics=("parallel",), vmem_limit_bytes=V7X_VMEM_LIMIT),
        name="merge",
    )(x, ya, *obs, *lss, yc, yd, gates, gates, gates, gates, wa, wb, wc, wd, wo, gain)


def _mlp_kernel(h_ref, x_ref, wu_hbm, wd_hbm, gn_ref, *rest, final, row_chunk, layer, tf):
    if final:
        yo_ref, wu_buf, wd_buf, sem, acc_ref = rest
    else:
        xo_ref, ho_ref, wu_buf, wd_buf, sem, acc_ref = rest
    i = pl.program_id(0)
    nf = wu_hbm.shape[2] // tf

    def tile_copies(f, slot):
        return (pltpu.make_async_copy(wu_hbm.at[layer, :, pl.ds(f * tf, tf)], wu_buf.at[slot], sem.at[0, slot]),
                pltpu.make_async_copy(wd_hbm.at[layer, pl.ds(f * tf, tf), :], wd_buf.at[slot], sem.at[1, slot]))

    def start(f, slot):
        for cp in tile_copies(f, slot):
            cp.start()

    @pl.when(i == 0)
    def _():
        start(0, 0)

    acc_ref[...] = x_ref[...]
    for f in range(nf):
        slot = f % 2
        if f + 1 < nf:
            start(f + 1, 1 - slot)
        else:
            @pl.when(i + 1 < pl.num_programs(0))
            def _():
                start(0, 1 - slot)
        for cp in tile_copies(f, slot):
            cp.wait()
        for r in range(h_ref.shape[0] // row_chunk):
            rows = pl.ds(r * row_chunk, row_chunk)
            u = jnp.dot(h_ref[rows, :], wu_buf[slot], preferred_element_type=F32)
            a = jnp.square(jnp.maximum(u, 0.0)).astype(BF16)
            acc_ref[rows, :] += jnp.dot(a, wd_buf[slot], preferred_element_type=F32)

    out = acc_ref[...]
    normed = _rms_rows(out, gn_ref[...])
    if final:
        yo_ref[...] = normed
    else:
        xo_ref[...] = out
        ho_ref[...] = normed.astype(ho_ref.dtype)


def _mlp(h, x, wu, wd, gain, layer, *, final, tm=512, tf=1024):
    t, d = x.shape
    ff = wu.shape[2]
    assert (ff // tf) % 2 == 0, "the two-slot weight stream needs an even number of hidden tiles"
    rows = pl.BlockSpec((tm, d), lambda i: (i, 0))
    if final:
        out_specs = [rows]
        out_shape = [jax.ShapeDtypeStruct((t, d), F32)]
    else:
        out_specs = [rows, rows]
        out_shape = [jax.ShapeDtypeStruct((t, d), F32), jax.ShapeDtypeStruct((t, d), BF16)]
    return pl.pallas_call(
        functools.partial(_mlp_kernel, final=final, row_chunk=256, layer=layer, tf=tf),
        grid=(t // tm,),
        in_specs=[rows, rows, pl.BlockSpec(memory_space=pl.ANY), pl.BlockSpec(memory_space=pl.ANY),
                  pl.BlockSpec((1, d), lambda i: (0, 0))],
        out_specs=out_specs,
        out_shape=out_shape,
        scratch_shapes=[pltpu.VMEM((2, d, tf), BF16), pltpu.VMEM((2, tf, d), BF16),
                        pltpu.SemaphoreType.DMA((2, 2)), pltpu.VMEM((tm, d), F32)],
        compiler_params=pltpu.CompilerParams(dimension_semantics=("arbitrary",), vmem_limit_bytes=V7X_VMEM_LIMIT),
        name="mlp_final" if final else "mlp",
    )(h, x, wu, wd, gain.reshape(1, d))


def _prep_in_weights(w_in):
    src, scale = [], []
    for col, width, is_query in _mix_segments():
        src += [col // LANES + c for c in range(width // LANES)]
        scale += [HEAD_DIM ** -0.5 * LOG2E if is_query else 1.0] * width
    w_mix = _relayout_cols(w_in, np.asarray(src, np.int32), np.asarray(scale, np.float32), LANES)
    gate_blk = 256
    gate0 = sum(SPLIT_SIZES[:-1]) // gate_blk
    w_gate = _relayout_cols(w_in, gate0 + np.arange(GATE_W // gate_blk, dtype=np.int32),
                            np.ones((GATE_W,), np.float32), gate_blk)
    return w_mix, w_gate


def _relayout_kernel(src_ref, w_ref, s_ref, o_ref):
    o_ref[...] = (w_ref[...] * s_ref[...]).astype(o_ref.dtype)


def _relayout_cols(w, src_blocks, scale, blk):
    depth, k, _ = w.shape
    nout = len(src_blocks)
    return pl.pallas_call(
        _relayout_kernel,
        grid_spec=pltpu.PrefetchScalarGridSpec(
            num_scalar_prefetch=1,
            grid=(depth, nout),
            in_specs=[pl.BlockSpec((None, k, blk), lambda l, c, src: (l, 0, src[c])),
                      pl.BlockSpec((1, blk), lambda l, c, src: (0, c))],
            out_specs=pl.BlockSpec((None, k, blk), lambda l, c, src: (l, 0, c))),
        out_shape=jax.ShapeDtypeStruct((depth, k, nout * blk), BF16),
        compiler_params=pltpu.CompilerParams(
            dimension_semantics=("parallel", "arbitrary"), vmem_limit_bytes=V7X_VMEM_LIMIT),
        name="relayout_cols",
    )(jnp.asarray(src_blocks), w, jnp.asarray(scale).reshape(1, nout * blk))


def _trunk(x, wts, tables):
    batch, seq, d = x.shape
    t = batch * seq
    depth = wts["w_mix"].shape[0]
    cos, sin = _rope_tables(seq)
    x2 = x.reshape(t, d)
    h = _rmsnorm(x2, wts["norm_mix"][0])
    b_slopes = _alibi_slopes(DIL_HEADS)
    d_table = _band_table(_alibi_slopes(SW_Q_HEADS), 1, SW_HALF_WINDOW, SW_KV_HEADS)
    for l in range(depth):
        p, *folded = _proj_mix(h, wts["w_mix"], l, batch, seq)
        gates = _proj(h, wts["w_gate"], l, sigmoid=True, tm=2048, name="proj_gate")
        ya = _natten(p, tables["natten"], l, batch, seq)
        obs, lss = [], []
        hg = DIL_HEADS_PER_GROUP
        p3 = p.reshape(batch, seq, P_WIDTH)
        for g, (win, dil) in enumerate(DIL_PATTERNS):
            if dil == 1:
                arr, cols = p3, (COL_B0Q, COL_B0K, COL_B0V)
            else:
                arr = folded[DILATED.index(dil)].reshape(batch * dil, seq // dil, BD_WIDTH)
                cols = (0, B_OUT, 2 * B_OUT)
            ob, ls = _banded(arr, w=win // (2 * dil), col_q=cols[0], col_k=cols[1], col_v=cols[2],
                             wq=B_OUT, wkv=B_OUT, kvmap=(0, 1),
                             table=_band_table(b_slopes[g * hg:(g + 1) * hg], dil, win // (2 * dil), 2),
                             sink=None, out_dtype=F32, has_lse=True, name="banded_d%d" % dil)
            obs.append(ob.reshape(batch, dil, seq // dil, B_OUT))
            lss.append(ls.reshape(batch, dil, seq // dil, B_OUT))
        qr, kr, ve, vo = _cprep(p, cos, sin, wts["q_gain_c"][l], wts["k_gain_c"][l], batch, seq)
        yc = _flash(qr, kr, ve, vo, batch, seq)
        (yd,) = _banded(p3, w=SW_HALF_WINDOW, col_q=COL_DQ, col_k=COL_DK, col_v=COL_DV, wq=D_QW, wkv=D_KVW,
                        kvmap=(0, 1), table=d_table, sink=wts["sink_d"][l], out_dtype=BF16, has_lse=False,
                        name="banded_sink")
        yd = yd.reshape(t, D_QW)
        x2, h = _merge(x2, ya, obs, lss, yc, yd, gates, wts["w_a"], wts["w_b"], wts["w_c"], wts["w_d"],
                       wts["w_out"], wts["norm_mlp"][l], l, seq)
        if l + 1 < depth:
            x2, h = _mlp(h, x2, wts["w_up"], wts["w_down"], wts["norm_mix"][l + 1], l, final=False)
        else:
            (y,) = _mlp(h, x2, wts["w_up"], wts["w_down"], wts["norm_final"], l, final=True)
    return y.reshape(batch, seq, d)


def kernel(x_prompt, x_sample, norm_mix, w_in, rel_bias_a, q_gain_c, k_gain_c, sink_d, w_branch_a, w_branch_b,
           w_branch_c, w_branch_d, w_out, norm_mlp, w_up, w_down, norm_final):
    w_mix, w_gate = _prep_in_weights(w_in)
    wts = dict(norm_mix=norm_mix, w_mix=w_mix, w_gate=w_gate, q_gain_c=q_gain_c, k_gain_c=k_gain_c, sink_d=sink_d,
               w_a=w_branch_a.astype(BF16), w_b=w_branch_b.astype(BF16), w_c=w_branch_c.astype(BF16),
               w_d=w_branch_d.astype(BF16), w_out=w_out.astype(BF16), norm_mlp=norm_mlp,
               w_up=w_up.astype(BF16), w_down=w_down.astype(BF16), norm_final=norm_final)
    tables = dict(natten=jax.vmap(_natten_bias_table)(rel_bias_a))
    return (_trunk(x_prompt, wts, tables), _trunk(x_sample, wts, tables))
```

```python
import functools

import numpy as np
import jax
import jax.numpy as jnp
from jax import lax
from jax.experimental import pallas as pl
from jax.experimental.pallas import tpu as pltpu

F32 = jnp.float32
BF16 = jnp.bfloat16

D_MODEL = 2048
HEAD_DIM = 64
GRID_W = 64
NA_HEADS = 8
NA_ROWS = 8
NA_COLS = 16
DIL_PATTERNS = ((128, 1), (512, 4), (2048, 16))
DIL_HEADS_PER_GROUP = 4
DIL_HEADS = 12
DIL_BLOCK = 64
GA_Q_HEADS = 8
GA_KV_HEADS = 2
ROPE_THETA = 10000.0
SW_Q_HEADS = 8
SW_KV_HEADS = 2
SW_HALF_WINDOW = 128
N_BRANCHES = 4
D_FF = 4 * D_MODEL
RMS_EPS = 1e-6
NEG_INF = -1e30
LOG2E = float(np.log2(np.e))

A_W = NA_HEADS * HEAD_DIM
B_W = DIL_HEADS * HEAD_DIM
B_OUT = DIL_HEADS_PER_GROUP * HEAD_DIM
C_QW = GA_Q_HEADS * HEAD_DIM
C_KVW = GA_KV_HEADS * HEAD_DIM
D_QW = SW_Q_HEADS * HEAD_DIM
D_KVW = SW_KV_HEADS * HEAD_DIM
GATE_W = N_BRANCHES * D_MODEL
SPLIT_SIZES = (A_W, A_W, A_W, B_W, B_W, B_W, C_QW, C_KVW, C_KVW, D_QW, D_KVW, D_KVW, GATE_W)

LANES = 128
V7X_VMEM_LIMIT = 56 * 1024 * 1024

COL_AQ = 0
COL_AK = 512
COL_AV = 1024
COL_DK = 1536
COL_DV = 1664
COL_CQ = 1792
COL_B0Q = 2304
COL_DQ = 2560
COL_B0K = 3072
COL_B0V = 3328
COL_CK = 3584
COL_CV = 3712
COL_BD = 3840
P_WIDTH = 5376
MIX_TN = 1792
GATE_TN = 2048
BD_TILE = COL_BD // MIX_TN
BD_LOCAL = COL_BD % MIX_TN
BD_WIDTH = 3 * B_OUT
DILATED = tuple(d for _, d in DIL_PATTERNS if d > 1)


def _mix_segments():
    offs = dict(zip("aq ak av bq bk bv cq ck cv dq dk dv".split(), np.cumsum((0,) + SPLIT_SIZES)[:12]))
    b = lambda name, g: offs[name] + g * B_OUT
    segs = [(offs["aq"], A_W, True), (offs["ak"], A_W, False), (offs["av"], A_W, False),
            (offs["dk"], D_KVW, False), (offs["dv"], D_KVW, False),
            (offs["cq"], C_QW, False), (b("bq", 0), B_OUT, True), (offs["dq"], D_QW, True),
            (b("bk", 0), B_OUT, False), (b("bv", 0), B_OUT, False),
            (offs["ck"], C_KVW, False), (offs["cv"], C_KVW, False)]
    for g in (1, 2):
        segs += [(b("bq", g), B_OUT, True), (b("bk", g), B_OUT, False), (b("bv", g), B_OUT, False)]
    assert sum(s[1] for s in segs) == P_WIDTH
    return segs


def _alibi_slopes(n):
    return (2.0 ** (-8.0 * np.arange(1, n + 1) / n)).astype(np.float32)


def _half_masks():
    lane = lax.broadcasted_iota(jnp.int32, (1, LANES), 1)
    lo = jnp.where(lane < HEAD_DIM, 1.0, 0.0).astype(BF16)
    hi = jnp.where(lane >= HEAD_DIM, 1.0, 0.0).astype(BF16)
    return lane, (lo, hi)


def _dot_nt(a, b):
    return lax.dot_general(a, b, (((1,), (1,)), ((), ())), preferred_element_type=F32)


def _rms_rows(x, gain):
    ms = jnp.mean(x * x, axis=-1, keepdims=True)
    return x * lax.rsqrt(ms + RMS_EPS) * gain


def _rmsnorm_kernel(x_ref, g_ref, o_ref):
    o_ref[...] = _rms_rows(x_ref[...], g_ref[...]).astype(o_ref.dtype)


def _rmsnorm(x, gain, tm=512):
    t, d = x.shape
    return pl.pallas_call(
        _rmsnorm_kernel,
        grid=(t // tm,),
        in_specs=[pl.BlockSpec((tm, d), lambda i: (i, 0)), pl.BlockSpec((1, d), lambda i: (0, 0))],
        out_specs=pl.BlockSpec((tm, d), lambda i: (i, 0)),
        out_shape=jax.ShapeDtypeStruct((t, d), BF16),
        compiler_params=pltpu.CompilerParams(dimension_semantics=("parallel",), vmem_limit_bytes=V7X_VMEM_LIMIT),
        name="rmsnorm",
    )(x, gain.reshape(1, d))


def _proj_kernel(h_ref, w_ref, o_ref, *, row_chunk, sigmoid):
    for r in range(h_ref.shape[0] // row_chunk):
        rows = pl.ds(r * row_chunk, row_chunk)
        acc = jnp.dot(h_ref[rows, :], w_ref[...], preferred_element_type=F32)
        if sigmoid:
            acc = 1.0 / (1.0 + jnp.exp(-acc))
        o_ref[rows, :] = acc.astype(o_ref.dtype)


def _proj(h, w, layer, *, sigmoid, tm=1024, tn=GATE_TN, name):
    t, k = h.shape
    n = w.shape[2]
    return pl.pallas_call(
        functools.partial(_proj_kernel, row_chunk=256, sigmoid=sigmoid),
        grid=(t // tm, n // tn),
        in_specs=[pl.BlockSpec((tm, k), lambda i, j: (i, 0)),
                  pl.BlockSpec((None, k, tn), lambda i, j: (layer, 0, j))],
        out_specs=pl.BlockSpec((tm, tn), lambda i, j: (i, j)),
        out_shape=jax.ShapeDtypeStruct((t, n), BF16),
        compiler_params=pltpu.CompilerParams(
            dimension_semantics=("parallel", "arbitrary"), vmem_limit_bytes=V7X_VMEM_LIMIT),
        name=name,
    )(h, w)


def _proj_mix_kernel(h_ref, w_ref, o_ref, *rest, row_chunk):
    (f4_ref, f16_ref), (stage, stage4) = rest[:2], rest[2:]
    j = pl.program_id(1)
    tm = h_ref.shape[0]
    slabs = BD_WIDTH // LANES
    quarter = tm // 4
    for r in range(tm // row_chunk):
        rows = pl.ds(r * row_chunk, row_chunk)
        acc = jnp.dot(h_ref[rows, :], w_ref[...], preferred_element_type=F32)
        o_ref[rows, :] = acc.astype(o_ref.dtype)
        for s in range(2 * slabs):
            stage[s, rows, :] = acc[:, BD_LOCAL + s * LANES:BD_LOCAL + (s + 1) * LANES]

    @pl.when(j == BD_TILE)
    def _():
        for s in range(slabs):
            for r4 in range(4):
                f4_ref[r4, :, s * LANES:(s + 1) * LANES] = stage[s, pl.ds(r4, quarter, stride=4), :].astype(
                    f4_ref.dtype)
                stage4[s, r4 * quarter:(r4 + 1) * quarter, :] = stage[slabs + s, pl.ds(r4, quarter, stride=4), :]
        for s in range(slabs):
            for r4 in range(4):
                for r4b in range(4):
                    part = stage4[s, pl.ds(r4 * quarter + r4b, quarter // 4, stride=4), :]
                    f16_ref[4 * r4b + r4, :, s * LANES:(s + 1) * LANES] = part.astype(f16_ref.dtype)


def _proj_mix(h, w, layer, batch, seq, tm=1024):
    assert DILATED == (4, 16), "the fold in _proj_mix_kernel is written for dilations 4 and 16"
    t, k = h.shape
    nb = seq // tm
    fold_specs = [pl.BlockSpec((None, dil, tm // dil, BD_WIDTH), lambda i, j: (i // nb, 0, i % nb, 0))
                  for dil in DILATED]
    fold_shapes = [jax.ShapeDtypeStruct((batch, dil, seq // dil, BD_WIDTH), BF16) for dil in DILATED]
    return pl.pallas_call(
        functools.partial(_proj_mix_kernel, row_chunk=256),
        grid=(t // tm, P_WIDTH // MIX_TN),
        in_specs=[pl.BlockSpec((tm, k), lambda i, j: (i, 0)),
                  pl.BlockSpec((None, k, MIX_TN), lambda i, j: (layer, 0, j))],
        out_specs=[pl.BlockSpec((tm, MIX_TN), lambda i, j: (i, j))] + fold_specs,
        out_shape=[jax.ShapeDtypeStruct((t, P_WIDTH), BF16)] + fold_shapes,
        scratch_shapes=[pltpu.VMEM((2 * BD_WIDTH // LANES, tm, LANES), F32),
                        pltpu.VMEM((BD_WIDTH // LANES, tm, LANES), F32)],
        compiler_params=pltpu.CompilerParams(
            dimension_semantics=("parallel", "arbitrary"), vmem_limit_bytes=V7X_VMEM_LIMIT),
        name="proj_mix",
    )(h, w)


NA_STEP_ROWS = 8
NA_STEP_TOK = NA_STEP_ROWS * GRID_W
NA_KEYS = NA_ROWS * GRID_W


def _natten_bias_table(rel_bias):
    qc = np.arange(GRID_W)[:, None]
    kc = np.arange(GRID_W)[None, :]
    c0 = np.clip(qc - NA_COLS // 2, 0, GRID_W - NA_COLS)
    valid = (kc >= c0) & (kc < c0 + NA_COLS)
    heads, ndr, ndc = rel_bias.shape
    span = 2 * GRID_W
    left = GRID_W - NA_COLS
    padded = jnp.pad(rel_bias.astype(F32), ((0, 0), (0, 0), (left, span - ndc - left)))
    skew = jnp.tile(padded, (1, 1, GRID_W))[..., :GRID_W * (span - 1)].reshape(heads, ndr, GRID_W, span - 1)
    t = skew[..., GRID_W - 1:2 * GRID_W - 1]
    t = jnp.where(valid, t * LOG2E, NEG_INF).transpose(0, 2, 1, 3)
    tbl = jnp.stack([t[:, :, NA_ROWS - 1 - s:2 * NA_ROWS - 1 - s, :] for s in range(NA_ROWS)])
    return tbl.reshape(NA_ROWS, NA_HEADS // 2, 2 * GRID_W, NA_KEYS)


def _natten_kernel(q_ref, kp_ref, kc_ref, kn_ref, vp_ref, vc_ref, vn_ref, tbl_ref, o_ref, qs, kbuf, vbuf, *, n_rows):
    i = pl.program_id(1)
    for s, (kr, vr) in enumerate(((kp_ref, vp_ref), (kc_ref, vc_ref), (kn_ref, vn_ref))):
        kbuf[s * NA_STEP_TOK:(s + 1) * NA_STEP_TOK, :] = kr[...]
        vbuf[s * NA_STEP_TOK:(s + 1) * NA_STEP_TOK, :] = vr[...]
    lane, hm = _half_masks()
    for j in range(NA_STEP_ROWS):
        for pi in range(NA_HEADS // 2):
            qpair = q_ref[j * GRID_W:(j + 1) * GRID_W, pi * LANES:(pi + 1) * LANES]
            for e in range(2):
                qs[j, pi, e * GRID_W:(e + 1) * GRID_W, :] = qpair * hm[e]

    shifts, offs = [], []
    for j in range(NA_STEP_ROWS):
        r = i * NA_STEP_ROWS + j
        r0 = jnp.clip(r - NA_ROWS // 2, 0, n_rows - NA_ROWS)
        shifts.append(r - r0)
        offs.append(pl.multiple_of((r0 - (i - 1) * NA_STEP_ROWS) * GRID_W, GRID_W))
    for pi in range(NA_HEADS // 2):
        cols = slice(pi * LANES, (pi + 1) * LANES)
        s = jnp.concatenate([_dot_nt(qs[j, pi], kbuf[pl.ds(offs[j], NA_KEYS), cols]) + tbl_ref[shifts[j], pi]
                             for j in range(NA_STEP_ROWS)], axis=0)
        m = jnp.broadcast_to(jnp.max(s, axis=-1, keepdims=True), (s.shape[0], LANES))
        p = jnp.concatenate([jnp.exp2(s[:, k * LANES:(k + 1) * LANES] - m) for k in range(NA_KEYS // LANES)], axis=1)
        inv = 1.0 / jnp.broadcast_to(jnp.sum(p, axis=-1, keepdims=True), (s.shape[0], LANES))
        p = p.astype(BF16)
        for j in range(NA_STEP_ROWS):
            rows = slice(j * 2 * GRID_W, (j + 1) * 2 * GRID_W)
            o = jnp.dot(p[rows], vbuf[pl.ds(offs[j], NA_KEYS), cols], preferred_element_type=F32) * inv[rows]
            o_ref[j * GRID_W:(j + 1) * GRID_W, cols] = jnp.where(lane < HEAD_DIM, o[:GRID_W], o[GRID_W:]).astype(
                o_ref.dtype)


def _natten(p, tbl, layer, batch, seq):
    t = batch * seq
    nb = seq // NA_STEP_TOK
    tok = NA_STEP_TOK

    def spec(col, which):
        cb = col // A_W
        if which == 0:
            return pl.BlockSpec((tok, A_W), lambda b, i: (b * nb + i, cb))
        if which < 0:
            return pl.BlockSpec((tok, A_W), lambda b, i: (b * nb + jnp.maximum(i - 1, 0), cb))
        return pl.BlockSpec((tok, A_W), lambda b, i: (b * nb + jnp.minimum(i + 1, nb - 1), cb))

    return pl.pallas_call(
        functools.partial(_natten_kernel, n_rows=seq // GRID_W),
        grid=(batch, nb),
        in_specs=[spec(COL_AQ, 0),
                  spec(COL_AK, -1), spec(COL_AK, 0), spec(COL_AK, 1),
                  spec(COL_AV, -1), spec(COL_AV, 0), spec(COL_AV, 1),
                  pl.BlockSpec((None,) + tbl.shape[1:], lambda b, i: (layer, 0, 0, 0, 0))],
        out_specs=pl.BlockSpec((tok, A_W), lambda b, i: (b * nb + i, 0)),
        out_shape=jax.ShapeDtypeStruct((t, A_W), BF16),
        scratch_shapes=[pltpu.VMEM((NA_STEP_ROWS, NA_HEADS // 2, 2 * GRID_W, LANES), BF16),
                        pltpu.VMEM((3 * tok, A_W), BF16), pltpu.VMEM((3 * tok, A_W), BF16)],
        compiler_params=pltpu.CompilerParams(
            dimension_semantics=("parallel", "arbitrary"), vmem_limit_bytes=V7X_VMEM_LIMIT),
        name="natten",
    )(p, p, p, p, p, p, p, tbl)


BAND_SUB = 128


def _band_table(slopes, dist_scale, w, chains):
    delta = np.arange(BAND_SUB + 2 * w)[None, :] - w - np.arange(BAND_SUB)[:, None]
    bias = -slopes[:, None, None] * (dist_scale * np.abs(delta)).astype(np.float32)[None]
    tbl = np.where(np.abs(delta)[None] <= w, bias * LOG2E, NEG_INF).astype(np.float32)
    return tbl.reshape(chains, -1, BAND_SUB + 2 * w)


def _banded_kernel(*refs, w, tq, n, kvmap, has_sink, has_lse):
    refs = list(refs)
    sink_ref = refs.pop(0) if has_sink else None
    q_ref, kp_ref, kc_ref, kn_ref, vp_ref, vc_ref, vn_ref, tbl_ref, o_ref = refs[:9]
    lse_ref = refs[9] if has_lse else None
    qs, kbuf, vbuf = refs[-3:]
    sub = BAND_SUB
    nchains = len(kvmap)
    blocks = q_ref.shape[1] // LANES // nchains
    i = pl.program_id(1)
    lane, hm = _half_masks()

    def fill(buf, lo, hi, ref):
        if buf.shape[1] == ref.shape[1]:
            buf[lo:hi] = ref[...]
        else:
            x = ref[...].astype(F32)
            swapped = pltpu.roll(x, HEAD_DIM, 1)
            buf[lo:hi, 0:LANES] = jnp.where(lane < HEAD_DIM, x, swapped).astype(buf.dtype)
            buf[lo:hi, LANES:2 * LANES] = jnp.where(lane < HEAD_DIM, swapped, x).astype(buf.dtype)

    for buf, (prev_ref, cur_ref, next_ref) in ((kbuf, (kp_ref, kc_ref, kn_ref)), (vbuf, (vp_ref, vc_ref, vn_ref))):
        fill(buf, 0, w, prev_ref)
        fill(buf, w, w + tq, cur_ref)
        fill(buf, w + tq, 2 * w + tq, next_ref)
    for j in range(tq // sub):
        for pi in range(nchains * blocks):
            qpair = q_ref[j * sub:(j + 1) * sub, pi * LANES:(pi + 1) * LANES]
            for e in range(2):
                qs[j, (2 * pi + e) * sub:(2 * pi + e + 1) * sub, :] = qpair * hm[e]
    nsub = tq // sub
    rows = 2 * blocks * sub
    nkeys = sub + 2 * w
    range_bias = {}
    for j in {0, nsub - 1}:
        kpos = i * tq + j * sub - w + lax.broadcasted_iota(jnp.int32, (1, nkeys), 1)
        range_bias[j] = jnp.where(kpos < 0, NEG_INF, jnp.where(kpos >= n, NEG_INF, 0.0))
    for c, kv in enumerate(kvmap):
        kvcols = slice(kv * LANES, (kv + 1) * LANES)
        tbl = tbl_ref[c]
        parts = []
        for j in range(nsub):
            sj = _dot_nt(qs[j, c * rows:(c + 1) * rows, :], kbuf[j * sub:j * sub + nkeys, kvcols]) + tbl
            parts.append(sj + range_bias[j] if j in range_bias else sj)
        s = jnp.concatenate(parts, axis=0)
        m = jnp.max(s, axis=-1, keepdims=True)
        if has_sink:
            sink = jnp.concatenate([jnp.full((sub, LANES), sink_ref[c * 2 * blocks + h] * LOG2E, F32)
                                    for h in range(2 * blocks)] * nsub, axis=0)
            m = jnp.maximum(m, sink)
        else:
            m = jnp.broadcast_to(m, (m.shape[0], LANES))
        p = jnp.concatenate([jnp.exp2(s[:, k * LANES:(k + 1) * LANES] - m) for k in range(nkeys // LANES)], axis=1)
        l = jnp.sum(p, axis=-1, keepdims=True)
        if has_sink:
            l = l + jnp.exp2(sink - m)
        else:
            l = jnp.broadcast_to(l, (l.shape[0], LANES))
        inv = 1.0 / l
        lse = m + jnp.log2(l)
        p = p.astype(BF16)
        for j in range(nsub):
            jr = slice(j * rows, (j + 1) * rows)
            o = jnp.dot(p[jr], vbuf[j * sub:j * sub + nkeys, kvcols], preferred_element_type=F32) * inv[jr]
            for x in range(blocks):
                cols = slice((c * blocks + x) * LANES, (c * blocks + x + 1) * LANES)
                even, odd = slice(2 * x * sub, (2 * x + 1) * sub), slice((2 * x + 1) * sub, (2 * x + 2) * sub)
                o_ref[j * sub:(j + 1) * sub, cols] = jnp.where(lane < HEAD_DIM, o[even], o[odd]).astype(o_ref.dtype)
                if has_lse:
                    lse_j = lse[jr]
                    lse_ref[j * sub:(j + 1) * sub, cols] = jnp.where(lane < HEAD_DIM, lse_j[even], lse_j[odd])


def _banded(arr, *, w, col_q, col_k, col_v, wq, wkv, kvmap, table, sink, out_dtype, has_lse, name):
    assert w <= BAND_SUB
    classes, n, _ = arr.shape
    tq = min(512, n)
    sub = tq // w
    nblk = n // w

    def qspec(col, width):
        return pl.BlockSpec((None, tq, width), lambda c, i: (c, i, col // width))

    def prev_spec(col, width):
        return pl.BlockSpec((None, w, width), lambda c, i: (c, jnp.maximum(i * sub - 1, 0), col // width))

    def next_spec(col, width):
        return pl.BlockSpec((None, w, width), lambda c, i: (c, jnp.minimum((i + 1) * sub, nblk - 1), col // width))

    in_specs = [qspec(col_q, wq),
                prev_spec(col_k, wkv), qspec(col_k, wkv), next_spec(col_k, wkv),
                prev_spec(col_v, wkv), qspec(col_v, wkv), next_spec(col_v, wkv),
                pl.BlockSpec(table.shape, lambda c, i: (0, 0, 0))]
    args = [arr] * 7 + [jnp.asarray(table)]
    if sink is not None:
        in_specs = [pl.BlockSpec(memory_space=pltpu.SMEM)] + in_specs
        args = [sink.astype(F32)] + args
    out_block = pl.BlockSpec((None, tq, wq), lambda c, i: (c, i, 0))
    out_shape = [jax.ShapeDtypeStruct((classes, n, wq), out_dtype)]
    out_specs = [out_block]
    if has_lse:
        out_shape.append(jax.ShapeDtypeStruct((classes, n, wq), F32))
        out_specs.append(out_block)
    return pl.pallas_call(
        functools.partial(_banded_kernel, w=w, tq=tq, n=n, kvmap=kvmap, has_sink=sink is not None, has_lse=has_lse),
        grid=(classes, n // tq),
        in_specs=in_specs,
        out_specs=out_specs,
        out_shape=out_shape,
        scratch_shapes=[pltpu.VMEM((tq // BAND_SUB, 2 * (wq // LANES) * BAND_SUB, LANES), BF16),
                        pltpu.VMEM((tq + 2 * w, len(kvmap) * LANES), BF16),
                        pltpu.VMEM((tq + 2 * w, len(kvmap) * LANES), BF16)],
        compiler_params=pltpu.CompilerParams(
            dimension_semantics=("parallel", "arbitrary"), vmem_limit_bytes=V7X_VMEM_LIMIT),
        name=name,
    )(*args)


def _rope_tables(seq):
    t = jnp.arange(seq)
    npair = HEAD_DIM // 4
    freqs = ROPE_THETA ** (-jnp.arange(npair, dtype=F32) / npair)

    def cs(pos):
        ang = pos.astype(F32)[:, None] * freqs[None, :]
        return jnp.cos(ang), jnp.sin(ang)

    cr, sr = cs(t // GRID_W)
    cc, sc = cs(t % GRID_W)
    cos = jnp.concatenate([cr, cr, cc, cc], axis=-1)
    sin = jnp.concatenate([-sr, sr, -sc, sc], axis=-1)
    return jnp.tile(cos, (1, 2)), jnp.tile(sin, (1, 2))


def _cprep_kernel(q_ref, k_ref, v_ref, cos_ref, sin_ref, qg_ref, kg_ref, avg_ref, qo_ref, ko_ref, ve_ref, vo_ref):
    lane = lax.broadcasted_iota(jnp.int32, (1, LANES), 1)
    own_half = (lane // HEAD_DIM) == pl.program_id(1)

    def group_pair(x):
        x = x.astype(F32)
        return jnp.where(own_half, x, pltpu.roll(x, HEAD_DIM, 1))

    v = group_pair(v_ref[...])
    ve_ref[...] = jnp.where(lane < HEAD_DIM, v, 1.0).astype(ve_ref.dtype)
    vo_ref[...] = jnp.where(lane < HEAD_DIM, 1.0, v).astype(vo_ref.dtype)
    first = (lane & (HEAD_DIM // 4)) == 0
    cos = cos_ref[...]
    sin = sin_ref[...]
    avg = avg_ref[...]

    def norm_rope(x, gain):
        x = x.astype(F32)
        sq = x * x
        hi = sq.astype(BF16)
        r1 = sq - hi.astype(F32)
        mid = r1.astype(BF16)
        lo = (r1 - mid.astype(F32)).astype(BF16)
        ms = (jnp.dot(hi, avg, preferred_element_type=F32) + jnp.dot(mid, avg, preferred_element_type=F32)
              + jnp.dot(lo, avg, preferred_element_type=F32))
        y = x * lax.rsqrt(ms + RMS_EPS) * gain
        quarter = HEAD_DIM // 4
        partner = jnp.where(first, pltpu.roll(y, LANES - quarter, 1), pltpu.roll(y, quarter, 1))
        return y * cos + partner * sin

    for c in range(q_ref.shape[1] // LANES):
        cols = slice(c * LANES, (c + 1) * LANES)
        qo_ref[:, cols] = norm_rope(q_ref[:, cols], qg_ref[...]).astype(qo_ref.dtype)
    ko_ref[...] = norm_rope(group_pair(k_ref[...]), kg_ref[...]).astype(ko_ref.dtype)


def _cprep(p, cos, sin, q_gain, k_gain, batch, seq, tm=2048):
    t = batch * seq
    nt = seq // tm
    avg = np.zeros((LANES, LANES), np.float32)
    avg[:HEAD_DIM, :HEAD_DIM] = 1.0 / HEAD_DIM
    avg[HEAD_DIM:, HEAD_DIM:] = 1.0 / HEAD_DIM
    gw = C_QW // GA_KV_HEADS
    qg = jnp.tile(q_gain.astype(F32) * (HEAD_DIM ** -0.5 * np.log2(np.e)), 2).reshape(1, LANES)
    kg = jnp.tile(k_gain.astype(F32), 2).reshape(1, LANES)
    return pl.pallas_call(
        _cprep_kernel,
        grid=(t // tm, GA_KV_HEADS),
        in_specs=[pl.BlockSpec((tm, gw), lambda i, g: (i, COL_CQ // gw + g)),
                  pl.BlockSpec((tm, LANES), lambda i, g: (i, COL_CK // LANES)),
                  pl.BlockSpec((tm, LANES), lambda i, g: (i, COL_CV // LANES)),
                  pl.BlockSpec((tm, LANES), lambda i, g: (i % nt, 0)),
                  pl.BlockSpec((tm, LANES), lambda i, g: (i % nt, 0)),
                  pl.BlockSpec((1, LANES), lambda i, g: (0, 0)),
                  pl.BlockSpec((1, LANES), lambda i, g: (0, 0)),
                  pl.BlockSpec((LANES, LANES), lambda i, g: (0, 0))],
        out_specs=[pl.BlockSpec((tm, gw), lambda i, g: (i, g))] + [pl.BlockSpec((tm, LANES), lambda i, g: (i, g))] * 3,
        out_shape=[jax.ShapeDtypeStruct((t, C_QW), BF16)] + [jax.ShapeDtypeStruct((t, GA_KV_HEADS * LANES), BF16)] * 3,
        compiler_params=pltpu.CompilerParams(
            dimension_semantics=("parallel", "arbitrary"), vmem_limit_bytes=V7X_VMEM_LIMIT),
        name="cprep",
    )(p, p, p, cos, sin, qg, kg, jnp.asarray(avg, BF16))


def _flash_kernel(q_ref, k_ref, ve_ref, vo_ref, o_ref, qs_ref, m_ref, acc_ref, *, tk):
    nkv = k_ref.shape[0] // tk
    nheads = qs_ref.shape[0]
    lane, hm = _half_masks()
    for h in range(nheads):
        qs_ref[h] = q_ref[:, (h // 2) * LANES:(h // 2 + 1) * LANES] * hm[h % 2]
    m_ref[...] = jnp.full(m_ref.shape, NEG_INF, F32)
    acc_ref[...] = jnp.zeros(acc_ref.shape, F32)

    def body(t, carry):
        rows = pl.ds(pl.multiple_of(t * tk, tk), tk)
        k = k_ref[rows, :]
        vs = (ve_ref[rows, :], vo_ref[rows, :])
        for h in range(nheads):
            s = _dot_nt(qs_ref[h], k)
            m_old = m_ref[h]
            m_new = jnp.maximum(m_old, jnp.max(s, axis=-1, keepdims=True))
            alpha = jnp.exp2(m_old - m_new)
            p = jnp.concatenate([jnp.exp2(s[:, c * LANES:(c + 1) * LANES] - m_new) for c in range(tk // LANES)],
                                axis=1)
            acc_ref[h] = alpha * acc_ref[h] + jnp.dot(p.astype(BF16), vs[h % 2], preferred_element_type=F32)
            m_ref[h] = m_new
        return carry

    lax.fori_loop(0, nkv, body, 0)
    for pi in range(nheads // 2):
        even, odd = acc_ref[2 * pi], acc_ref[2 * pi + 1]
        even = even / pltpu.roll(even, HEAD_DIM, 1)
        odd = odd / pltpu.roll(odd, HEAD_DIM, 1)
        o_ref[:, pi * LANES:(pi + 1) * LANES] = jnp.where(lane < HEAD_DIM, even, odd).astype(o_ref.dtype)


def _flash(q, k, ve, vo, batch, seq, tq=2048, tk=1024):
    t = batch * seq
    nq = seq // tq
    gw = C_QW // GA_KV_HEADS
    nheads = GA_Q_HEADS // GA_KV_HEADS
    kv_spec = pl.BlockSpec((seq, LANES), lambda b, g, i: (b, g))
    return pl.pallas_call(
        functools.partial(_flash_kernel, tk=tk),
        grid=(batch, GA_KV_HEADS, nq),
        in_specs=[pl.BlockSpec((tq, gw), lambda b, g, i: (b * nq + i, g)), kv_spec, kv_spec, kv_spec],
        out_specs=pl.BlockSpec((tq, gw), lambda b, g, i: (b * nq + i, g)),
        out_shape=jax.ShapeDtypeStruct((t, C_QW), BF16),
        scratch_shapes=[pltpu.VMEM((nheads, tq, LANES), BF16), pltpu.VMEM((nheads, tq, LANES), F32),
                        pltpu.VMEM((nheads, tq, LANES), F32)],
        compiler_params=pltpu.CompilerParams(
            dimension_semantics=("parallel", "parallel", "arbitrary"), vmem_limit_bytes=V7X_VMEM_LIMIT),
        name="flash",
    )(q, k, ve, vo)


def _merge_kernel(x_ref, ya_ref, ob0_ref, ob1_ref, ob2_ref, ls0_ref, ls1_ref, ls2_ref, yc_ref, yd_ref,
                  ga_ref, gb_ref, gc_ref, gd_ref, wa_ref, wb_ref, wc_ref, wd_ref, wo_ref, gn_ref,
                  xo_ref, ho_ref, stage):
    tm = x_ref.shape[0]
    slots = iter(range(stage.shape[0]))

    def token_order(ref):
        dil = ref.shape[0]
        if dil == 1:
            return ref[0]
        slot = next(slots)
        for r in range(dil):
            for s in range(B_OUT // LANES):
                stage[slot, s, pl.ds(r, tm // dil, stride=dil), :] = ref[r, :, s * LANES:(s + 1) * LANES]
        return jnp.concatenate([stage[slot, s] for s in range(B_OUT // LANES)], axis=1)

    l0, l1, l2 = token_order(ls0_ref), token_order(ls1_ref), token_order(ls2_ref)
    mx = jnp.maximum(jnp.maximum(l0, l1), l2)
    e0, e1, e2 = jnp.exp2(l0 - mx), jnp.exp2(l1 - mx), jnp.exp2(l2 - mx)
    yb = (e0 * token_order(ob0_ref) + e1 * token_order(ob1_ref) + e2 * token_order(ob2_ref)) / (e0 + e1 + e2)

    def branch(g_ref, y, w_ref):
        return g_ref[...].astype(F32) * jnp.dot(y, w_ref[...], preferred_element_type=F32)

    merged = (branch(ga_ref, ya_ref[...], wa_ref) + branch(gb_ref, yb.astype(BF16), wb_ref)
              + branch(gc_ref, yc_ref[...], wc_ref) + branch(gd_ref, yd_ref[...], wd_ref))
    out = x_ref[...] + jnp.dot(merged.astype(BF16), wo_ref[...], preferred_element_type=F32)
    xo_ref[...] = out
    ho_ref[...] = _rms_rows(out, gn_ref[...]).astype(ho_ref.dtype)


def _merge(x, ya, obs, lss, yc, yd, gates, wa, wb, wc, wd, wo, gain, layer, seq, tm=256):
    t, d = x.shape
    nb = seq // tm

    def rows(width, col=0):
        return pl.BlockSpec((tm, width), lambda i: (i, col))

    def classes(arr):
        dil = arr.shape[1]
        return pl.BlockSpec((None, dil, tm // dil, B_OUT), lambda i: (i // nb, 0, i % nb, 0))

    def const(arr):
        return pl.BlockSpec((None,) + arr.shape[1:], lambda i: (layer, 0, 0), pipeline_mode=pl.Buffered(1))

    gain = gain.reshape(1, d)
    return pl.pallas_call(
        _merge_kernel,
        grid=(t // tm,),
        in_specs=[rows(d), rows(A_W)] + [classes(a) for a in obs] + [classes(a) for a in lss]
                 + [rows(C_QW), rows(D_QW), rows(d, 0), rows(d, 1), rows(d, 2), rows(d, 3),
                    const(wa), const(wb), const(wc), const(wd), const(wo),
                    pl.BlockSpec((1, d), lambda i: (0, 0), pipeline_mode=pl.Buffered(1))],
        out_specs=[rows(d), rows(d)],
        out_shape=[jax.ShapeDtypeStruct((t, d), F32), jax.ShapeDtypeStruct((t, d), BF16)],
        scratch_shapes=[pltpu.VMEM((2 * len(DILATED), B_OUT // LANES, tm, LANES), F32)],
        compiler_params=pltpu.CompilerParams(dimension_semantics=("parallel",), vmem_limit_bytes=V7X_VMEM_LIMIT),
        name="merge",
    )(x, ya, *obs, *lss, yc, yd, gates, gates, gates, gates, wa, wb, wc, wd, wo, gain)


def _mlp_kernel(h_ref, x_ref, wu_hbm, wd_hbm, gn_ref, *rest, final, row_chunk, layer, tf):
    if final:
        yo_ref, wu_buf, wd_buf, sem, acc_ref = rest
    else:
        xo_ref, ho_ref, wu_buf, wd_buf, sem, acc_ref = rest
    i = pl.program_id(0)
    nf = wu_hbm.shape[1]
    d = wu_hbm.shape[2]

    def tile_copies(f, slot):
        cps = []
        for half in range(2):
            ur = pl.ds(half * (d // 2), d // 2)
            dr = pl.ds(half * (tf // 2), tf // 2)
            cps.append(pltpu.make_async_copy(wu_hbm.at[layer, f, ur, :], wu_buf.at[slot, ur, :], sem.at[half, slot]))
            cps.append(pltpu.make_async_copy(wd_hbm.at[layer, pl.ds(f * tf + half * (tf // 2), tf // 2), :],
                                             wd_buf.at[slot, dr, :], sem.at[2 + half, slot]))
        return cps

    def start(f, slot):
        for cp in tile_copies(f, slot):
            cp.start()

    @pl.when(i == 0)
    def _():
        start(0, 0)

    acc_ref[...] = x_ref[...]
    for f in range(nf):
        slot = f % 2
        if f + 1 < nf:
            start(f + 1, 1 - slot)
        else:
            @pl.when(i + 1 < pl.num_programs(0))
            def _():
                start(0, 1 - slot)
        for cp in tile_copies(f, slot):
            cp.wait()
        for r in range(h_ref.shape[0] // row_chunk):
            rows = pl.ds(r * row_chunk, row_chunk)
            u = jnp.dot(h_ref[rows, :], wu_buf[slot], preferred_element_type=F32)
            a = jnp.square(jnp.maximum(u, 0.0)).astype(BF16)
            acc_ref[rows, :] += jnp.dot(a, wd_buf[slot], preferred_element_type=F32)

    out = acc_ref[...]
    normed = _rms_rows(out, gn_ref[...])
    if final:
        yo_ref[...] = normed
    else:
        xo_ref[...] = out
        ho_ref[...] = normed.astype(ho_ref.dtype)


MLP_TF = 1024


def _tile_major(w, tn):
    depth, k, n = w.shape
    return w.reshape(depth, k, n // tn, tn).transpose(0, 2, 1, 3)


def _mlp(h, x, wu, wd, gain, layer, *, final, tm=512):
    t, d = x.shape
    tf = wu.shape[3]
    assert wu.shape[1] % 2 == 0, "the two-slot weight stream needs an even number of hidden tiles"
    rows = pl.BlockSpec((tm, d), lambda i: (i, 0))
    if final:
        out_specs = [rows]
        out_shape = [jax.ShapeDtypeStruct((t, d), F32)]
    else:
        out_specs = [rows, rows]
        out_shape = [jax.ShapeDtypeStruct((t, d), F32), jax.ShapeDtypeStruct((t, d), BF16)]
    return pl.pallas_call(
        functools.partial(_mlp_kernel, final=final, row_chunk=256, layer=layer, tf=tf),
        grid=(t // tm,),
        in_specs=[rows, rows, pl.BlockSpec(memory_space=pl.ANY), pl.BlockSpec(memory_space=pl.ANY),
                  pl.BlockSpec((1, d), lambda i: (0, 0))],
        out_specs=out_specs,
        out_shape=out_shape,
        scratch_shapes=[pltpu.VMEM((2, d, tf), BF16), pltpu.VMEM((2, tf, d), BF16),
                        pltpu.SemaphoreType.DMA((4, 2)), pltpu.VMEM((tm, d), F32)],
        compiler_params=pltpu.CompilerParams(dimension_semantics=("arbitrary",), vmem_limit_bytes=V7X_VMEM_LIMIT),
        name="mlp_final" if final else "mlp",
    )(h, x, wu, wd, gain.reshape(1, d))


def _prep_in_weights(w_in):
    src, scale = [], []
    for col, width, is_query in _mix_segments():
        src += [col // LANES + c for c in range(width // LANES)]
        scale += [HEAD_DIM ** -0.5 * LOG2E if is_query else 1.0] * width
    w_mix = _relayout_cols(w_in, np.asarray(src, np.int32), np.asarray(scale, np.float32), LANES)
    gate_blk = 256
    gate0 = sum(SPLIT_SIZES[:-1]) // gate_blk
    w_gate = _relayout_cols(w_in, gate0 + np.arange(GATE_W // gate_blk, dtype=np.int32),
                            np.ones((GATE_W,), np.float32), gate_blk)
    return w_mix, w_gate


def _relayout_kernel(src_ref, w_ref, s_ref, o_ref):
    o_ref[...] = (w_ref[...] * s_ref[...]).astype(o_ref.dtype)


def _relayout_cols(w, src_blocks, scale, blk):
    depth, k, _ = w.shape
    nout = len(src_blocks)
    return pl.pallas_call(
        _relayout_kernel,
        grid_spec=pltpu.PrefetchScalarGridSpec(
            num_scalar_prefetch=1,
            grid=(depth, nout),
            in_specs=[pl.BlockSpec((None, k, blk), lambda l, c, src: (l, 0, src[c])),
                      pl.BlockSpec((1, blk), lambda l, c, src: (0, c))],
            out_specs=pl.BlockSpec((None, k, blk), lambda l, c, src: (l, 0, c))),
        out_shape=jax.ShapeDtypeStruct((depth, k, nout * blk), BF16),
        compiler_params=pltpu.CompilerParams(
            dimension_semantics=("parallel", "arbitrary"), vmem_limit_bytes=V7X_VMEM_LIMIT),
        name="relayout_cols",
    )(jnp.asarray(src_blocks), w, jnp.asarray(scale).reshape(1, nout * blk))


def _trunk(x, wts, tables):
    batch, seq, d = x.shape
    t = batch * seq
    depth = wts["w_mix"].shape[0]
    cos, sin = _rope_tables(seq)
    x2 = x.reshape(t, d)
    h = _rmsnorm(x2, wts["norm_mix"][0])
    b_slopes = _alibi_slopes(DIL_HEADS)
    d_table = _band_table(_alibi_slopes(SW_Q_HEADS), 1, SW_HALF_WINDOW, SW_KV_HEADS)
    for l in range(depth):
        p, *folded = _proj_mix(h, wts["w_mix"], l, batch, seq)
        gates = _proj(h, wts["w_gate"], l, sigmoid=True, tm=2048, name="proj_gate")
        ya = _natten(p, tables["natten"], l, batch, seq)
        obs, lss = [], []
        hg = DIL_HEADS_PER_GROUP
        p3 = p.reshape(batch, seq, P_WIDTH)
        for g, (win, dil) in enumerate(DIL_PATTERNS):
            if dil == 1:
                arr, cols = p3, (COL_B0Q, COL_B0K, COL_B0V)
            else:
                arr = folded[DILATED.index(dil)].reshape(batch * dil, seq // dil, BD_WIDTH)
                cols = (0, B_OUT, 2 * B_OUT)
            ob, ls = _banded(arr, w=win // (2 * dil), col_q=cols[0], col_k=cols[1], col_v=cols[2],
                             wq=B_OUT, wkv=B_OUT, kvmap=(0, 1),
                             table=_band_table(b_slopes[g * hg:(g + 1) * hg], dil, win // (2 * dil), 2),
                             sink=None, out_dtype=F32, has_lse=True, name="banded_d%d" % dil)
            obs.append(ob.reshape(batch, dil, seq // dil, B_OUT))
            lss.append(ls.reshape(batch, dil, seq // dil, B_OUT))
        qr, kr, ve, vo = _cprep(p, cos, sin, wts["q_gain_c"][l], wts["k_gain_c"][l], batch, seq)
        yc = _flash(qr, kr, ve, vo, batch, seq)
        (yd,) = _banded(p3, w=SW_HALF_WINDOW, col_q=COL_DQ, col_k=COL_DK, col_v=COL_DV, wq=D_QW, wkv=D_KVW,
                        kvmap=(0, 1), table=d_table, sink=wts["sink_d"][l], out_dtype=BF16, has_lse=False,
                        name="banded_sink")
        yd = yd.reshape(t, D_QW)
        x2, h = _merge(x2, ya, obs, lss, yc, yd, gates, wts["w_a"], wts["w_b"], wts["w_c"], wts["w_d"],
                       wts["w_out"], wts["norm_mlp"][l], l, seq)
        if l + 1 < depth:
            x2, h = _mlp(h, x2, wts["w_up"], wts["w_down"], wts["norm_mix"][l + 1], l, final=False)
        else:
            (y,) = _mlp(h, x2, wts["w_up"], wts["w_down"], wts["norm_final"], l, final=True)
    return y.reshape(batch, seq, d)


def kernel(x_prompt, x_sample, norm_mix, w_in, rel_bias_a, q_gain_c, k_gain_c, sink_d, w_branch_a, w_branch_b,
           w_branch_c, w_branch_d, w_out, norm_mlp, w_up, w_down, norm_final):
    w_mix, w_gate = _prep_in_weights(w_in)
    wts = dict(norm_mix=norm_mix, w_mix=w_mix, w_gate=w_gate, q_gain_c=q_gain_c, k_gain_c=k_gain_c, sink_d=sink_d,
               w_a=w_branch_a.astype(BF16), w_b=w_branch_b.astype(BF16), w_c=w_branch_c.astype(BF16),
               w_d=w_branch_d.astype(BF16), w_out=w_out.astype(BF16), norm_mlp=norm_mlp,
               w_up=_tile_major(w_up.astype(BF16), MLP_TF), w_down=w_down.astype(BF16), norm_final=norm_final)
    tables = dict(natten=jax.vmap(_natten_bias_table)(rel_bias_a))
    return (_trunk(x_prompt, wts, tables), _trunk(x_sample, wts, tables))
```

```python
import functools

import numpy as np
import jax
import jax.numpy as jnp
from jax import lax
from jax.experimental import pallas as pl
from jax.experimental.pallas import tpu as pltpu

F32 = jnp.float32
BF16 = jnp.bfloat16

D_MODEL = 2048
HEAD_DIM = 64
GRID_W = 64
NA_HEADS = 8
NA_ROWS = 8
NA_COLS = 16
DIL_PATTERNS = ((128, 1), (512, 4), (2048, 16))
DIL_HEADS_PER_GROUP = 4
DIL_HEADS = 12
DIL_BLOCK = 64
GA_Q_HEADS = 8
GA_KV_HEADS = 2
ROPE_THETA = 10000.0
SW_Q_HEADS = 8
SW_KV_HEADS = 2
SW_HALF_WINDOW = 128
N_BRANCHES = 4
D_FF = 4 * D_MODEL
RMS_EPS = 1e-6
NEG_INF = -1e30
LOG2E = float(np.log2(np.e))

A_W = NA_HEADS * HEAD_DIM
B_W = DIL_HEADS * HEAD_DIM
B_OUT = DIL_HEADS_PER_GROUP * HEAD_DIM
C_QW = GA_Q_HEADS * HEAD_DIM
C_KVW = GA_KV_HEADS * HEAD_DIM
D_QW = SW_Q_HEADS * HEAD_DIM
D_KVW = SW_KV_HEADS * HEAD_DIM
GATE_W = N_BRANCHES * D_MODEL
SPLIT_SIZES = (A_W, A_W, A_W, B_W, B_W, B_W, C_QW, C_KVW, C_KVW, D_QW, D_KVW, D_KVW, GATE_W)

LANES = 128
V7X_VMEM_LIMIT = 56 * 1024 * 1024

COL_AQ = 0
COL_AK = 512
COL_AV = 1024
COL_DK = 1536
COL_DV = 1664
COL_CQ = 1792
COL_B0Q = 2304
COL_DQ = 2560
COL_B0K = 3072
COL_B0V = 3328
COL_CK = 3584
COL_CV = 3712
COL_BD = 3840
P_WIDTH = 5376
MIX_TN = 1792
GATE_TN = 2048
BD_TILE = COL_BD // MIX_TN
BD_LOCAL = COL_BD % MIX_TN
BD_WIDTH = 3 * B_OUT
DILATED = tuple(d for _, d in DIL_PATTERNS if d > 1)


def _mix_segments():
    offs = dict(zip("aq ak av bq bk bv cq ck cv dq dk dv".split(), np.cumsum((0,) + SPLIT_SIZES)[:12]))
    b = lambda name, g: offs[name] + g * B_OUT
    segs = [(offs["aq"], A_W, True), (offs["ak"], A_W, False), (offs["av"], A_W, False),
            (offs["dk"], D_KVW, False), (offs["dv"], D_KVW, False),
            (offs["cq"], C_QW, False), (b("bq", 0), B_OUT, True), (offs["dq"], D_QW, True),
            (b("bk", 0), B_OUT, False), (b("bv", 0), B_OUT, False),
            (offs["ck"], C_KVW, False), (offs["cv"], C_KVW, False)]
    for g in (1, 2):
        segs += [(b("bq", g), B_OUT, True), (b("bk", g), B_OUT, False), (b("bv", g), B_OUT, False)]
    assert sum(s[1] for s in segs) == P_WIDTH
    return segs


def _alibi_slopes(n):
    return (2.0 ** (-8.0 * np.arange(1, n + 1) / n)).astype(np.float32)


def _half_masks():
    lane = lax.broadcasted_iota(jnp.int32, (1, LANES), 1)
    lo = jnp.where(lane < HEAD_DIM, 1.0, 0.0).astype(BF16)
    hi = jnp.where(lane >= HEAD_DIM, 1.0, 0.0).astype(BF16)
    return lane, (lo, hi)


def _dot_nt(a, b):
    return lax.dot_general(a, b, (((1,), (1,)), ((), ())), preferred_element_type=F32)


def _rms_rows(x, gain):
    ms = jnp.mean(x * x, axis=-1, keepdims=True)
    return x * lax.rsqrt(ms + RMS_EPS) * gain


def _rmsnorm_kernel(x_ref, g_ref, o_ref):
    o_ref[...] = _rms_rows(x_ref[...], g_ref[...]).astype(o_ref.dtype)


def _rmsnorm(x, gain, tm=512):
    t, d = x.shape
    return pl.pallas_call(
        _rmsnorm_kernel,
        grid=(t // tm,),
        in_specs=[pl.BlockSpec((tm, d), lambda i: (i, 0)), pl.BlockSpec((1, d), lambda i: (0, 0))],
        out_specs=pl.BlockSpec((tm, d), lambda i: (i, 0)),
        out_shape=jax.ShapeDtypeStruct((t, d), BF16),
        compiler_params=pltpu.CompilerParams(dimension_semantics=("parallel",), vmem_limit_bytes=V7X_VMEM_LIMIT),
        name="rmsnorm",
    )(x, gain.reshape(1, d))


def _proj_kernel(h_ref, w_ref, o_ref, *, row_chunk, sigmoid):
    for r in range(h_ref.shape[0] // row_chunk):
        rows = pl.ds(r * row_chunk, row_chunk)
        acc = jnp.dot(h_ref[rows, :], w_ref[...], preferred_element_type=F32)
        if sigmoid:
            acc = 1.0 / (1.0 + jnp.exp(-acc))
        o_ref[rows, :] = acc.astype(o_ref.dtype)


def _proj(h, w, layer, *, sigmoid, tm=1024, tn=GATE_TN, name):
    t, k = h.shape
    n = w.shape[2]
    return pl.pallas_call(
        functools.partial(_proj_kernel, row_chunk=256, sigmoid=sigmoid),
        grid=(t // tm, n // tn),
        in_specs=[pl.BlockSpec((tm, k), lambda i, j: (i, 0)),
                  pl.BlockSpec((None, k, tn), lambda i, j: (layer, 0, j))],
        out_specs=pl.BlockSpec((tm, tn), lambda i, j: (i, j)),
        out_shape=jax.ShapeDtypeStruct((t, n), BF16),
        compiler_params=pltpu.CompilerParams(
            dimension_semantics=("parallel", "arbitrary"), vmem_limit_bytes=V7X_VMEM_LIMIT),
        name=name,
    )(h, w)


def _proj_mix_kernel(h_ref, w_ref, o_ref, *rest, row_chunk):
    (f4_ref, f16_ref), (stage, stage4) = rest[:2], rest[2:]
    j = pl.program_id(1)
    tm = h_ref.shape[0]
    slabs = BD_WIDTH // LANES
    quarter = tm // 4
    for r in range(tm // row_chunk):
        rows = pl.ds(r * row_chunk, row_chunk)
        acc = jnp.dot(h_ref[rows, :], w_ref[...], preferred_element_type=F32)
        o_ref[rows, :] = acc.astype(o_ref.dtype)
        for s in range(2 * slabs):
            stage[s, rows, :] = acc[:, BD_LOCAL + s * LANES:BD_LOCAL + (s + 1) * LANES]

    @pl.when(j == BD_TILE)
    def _():
        for s in range(slabs):
            for r4 in range(4):
                f4_ref[r4, :, s * LANES:(s + 1) * LANES] = stage[s, pl.ds(r4, quarter, stride=4), :].astype(
                    f4_ref.dtype)
                stage4[s, r4 * quarter:(r4 + 1) * quarter, :] = stage[slabs + s, pl.ds(r4, quarter, stride=4), :]
        for s in range(slabs):
            for r4 in range(4):
                for r4b in range(4):
                    part = stage4[s, pl.ds(r4 * quarter + r4b, quarter // 4, stride=4), :]
                    f16_ref[4 * r4b + r4, :, s * LANES:(s + 1) * LANES] = part.astype(f16_ref.dtype)


def _proj_mix(h, w, layer, batch, seq, tm=1024):
    assert DILATED == (4, 16), "the fold in _proj_mix_kernel is written for dilations 4 and 16"
    t, k = h.shape
    nb = seq // tm
    fold_specs = [pl.BlockSpec((None, dil, tm // dil, BD_WIDTH), lambda i, j: (i // nb, 0, i % nb, 0))
                  for dil in DILATED]
    fold_shapes = [jax.ShapeDtypeStruct((batch, dil, seq // dil, BD_WIDTH), BF16) for dil in DILATED]
    return pl.pallas_call(
        functools.partial(_proj_mix_kernel, row_chunk=256),
        grid=(t // tm, P_WIDTH // MIX_TN),
        in_specs=[pl.BlockSpec((tm, k), lambda i, j: (i, 0)),
                  pl.BlockSpec((None, k, MIX_TN), lambda i, j: (layer, 0, j))],
        out_specs=[pl.BlockSpec((tm, MIX_TN), lambda i, j: (i, j))] + fold_specs,
        out_shape=[jax.ShapeDtypeStruct((t, P_WIDTH), BF16)] + fold_shapes,
        scratch_shapes=[pltpu.VMEM((2 * BD_WIDTH // LANES, tm, LANES), F32),
                        pltpu.VMEM((BD_WIDTH // LANES, tm, LANES), F32)],
        compiler_params=pltpu.CompilerParams(
            dimension_semantics=("parallel", "arbitrary"), vmem_limit_bytes=V7X_VMEM_LIMIT),
        name="proj_mix",
    )(h, w)


NA_STEP_ROWS = 8
NA_STEP_TOK = NA_STEP_ROWS * GRID_W
NA_KEYS = NA_ROWS * GRID_W


def _natten_bias_table(rel_bias):
    qc = np.arange(GRID_W)[:, None]
    kc = np.arange(GRID_W)[None, :]
    c0 = np.clip(qc - NA_COLS // 2, 0, GRID_W - NA_COLS)
    valid = (kc >= c0) & (kc < c0 + NA_COLS)
    heads, ndr, ndc = rel_bias.shape
    span = 2 * GRID_W
    left = GRID_W - NA_COLS
    padded = jnp.pad(rel_bias.astype(F32), ((0, 0), (0, 0), (left, span - ndc - left)))
    skew = jnp.tile(padded, (1, 1, GRID_W))[..., :GRID_W * (span - 1)].reshape(heads, ndr, GRID_W, span - 1)
    t = skew[..., GRID_W - 1:2 * GRID_W - 1]
    t = jnp.where(valid, t * LOG2E, NEG_INF).transpose(0, 2, 1, 3)
    tbl = jnp.stack([t[:, :, NA_ROWS - 1 - s:2 * NA_ROWS - 1 - s, :] for s in range(NA_ROWS)])
    return tbl.reshape(NA_ROWS, NA_HEADS // 2, 2 * GRID_W, NA_KEYS)


def _natten_kernel(q_ref, kp_ref, kc_ref, kn_ref, vp_ref, vc_ref, vn_ref, tbl_ref, o_ref, qs, kbuf, vbuf, *, n_rows):
    i = pl.program_id(1)
    for s, (kr, vr) in enumerate(((kp_ref, vp_ref), (kc_ref, vc_ref), (kn_ref, vn_ref))):
        kbuf[s * NA_STEP_TOK:(s + 1) * NA_STEP_TOK, :] = kr[...]
        vbuf[s * NA_STEP_TOK:(s + 1) * NA_STEP_TOK, :] = vr[...]
    lane, hm = _half_masks()
    for j in range(NA_STEP_ROWS):
        for pi in range(NA_HEADS // 2):
            qpair = q_ref[j * GRID_W:(j + 1) * GRID_W, pi * LANES:(pi + 1) * LANES]
            for e in range(2):
                qs[j, pi, e * GRID_W:(e + 1) * GRID_W, :] = qpair * hm[e]

    shifts, offs = [], []
    for j in range(NA_STEP_ROWS):
        r = i * NA_STEP_ROWS + j
        r0 = jnp.clip(r - NA_ROWS // 2, 0, n_rows - NA_ROWS)
        shifts.append(r - r0)
        offs.append(pl.multiple_of((r0 - (i - 1) * NA_STEP_ROWS) * GRID_W, GRID_W))
    for pi in range(NA_HEADS // 2):
        cols = slice(pi * LANES, (pi + 1) * LANES)
        s = jnp.concatenate([_dot_nt(qs[j, pi], kbuf[pl.ds(offs[j], NA_KEYS), cols]) + tbl_ref[shifts[j], pi]
                             for j in range(NA_STEP_ROWS)], axis=0)
        m = jnp.broadcast_to(jnp.max(s, axis=-1, keepdims=True), (s.shape[0], LANES))
        p = jnp.concatenate([jnp.exp2(s[:, k * LANES:(k + 1) * LANES] - m) for k in range(NA_KEYS // LANES)], axis=1)
        inv = 1.0 / jnp.broadcast_to(jnp.sum(p, axis=-1, keepdims=True), (s.shape[0], LANES))
        p = p.astype(BF16)
        for j in range(NA_STEP_ROWS):
            rows = slice(j * 2 * GRID_W, (j + 1) * 2 * GRID_W)
            o = jnp.dot(p[rows], vbuf[pl.ds(offs[j], NA_KEYS), cols], preferred_element_type=F32) * inv[rows]
            o_ref[j * GRID_W:(j + 1) * GRID_W, cols] = jnp.where(lane < HEAD_DIM, o[:GRID_W], o[GRID_W:]).astype(
                o_ref.dtype)


def _natten(p, tbl, layer, batch, seq):
    t = batch * seq
    nb = seq // NA_STEP_TOK
    tok = NA_STEP_TOK

    def spec(col, which):
        cb = col // A_W
        if which == 0:
            return pl.BlockSpec((tok, A_W), lambda b, i: (b * nb + i, cb))
        if which < 0:
            return pl.BlockSpec((tok, A_W), lambda b, i: (b * nb + jnp.maximum(i - 1, 0), cb))
        return pl.BlockSpec((tok, A_W), lambda b, i: (b * nb + jnp.minimum(i + 1, nb - 1), cb))

    return pl.pallas_call(
        functools.partial(_natten_kernel, n_rows=seq // GRID_W),
        grid=(batch, nb),
        in_specs=[spec(COL_AQ, 0),
                  spec(COL_AK, -1), spec(COL_AK, 0), spec(COL_AK, 1),
                  spec(COL_AV, -1), spec(COL_AV, 0), spec(COL_AV, 1),
                  pl.BlockSpec((None,) + tbl.shape[1:], lambda b, i: (layer, 0, 0, 0, 0))],
        out_specs=pl.BlockSpec((tok, A_W), lambda b, i: (b * nb + i, 0)),
        out_shape=jax.ShapeDtypeStruct((t, A_W), BF16),
        scratch_shapes=[pltpu.VMEM((NA_STEP_ROWS, NA_HEADS // 2, 2 * GRID_W, LANES), BF16),
                        pltpu.VMEM((3 * tok, A_W), BF16), pltpu.VMEM((3 * tok, A_W), BF16)],
        compiler_params=pltpu.CompilerParams(
            dimension_semantics=("parallel", "arbitrary"), vmem_limit_bytes=V7X_VMEM_LIMIT),
        name="natten",
    )(p, p, p, p, p, p, p, tbl)


BAND_SUB = 128


def _band_table(slopes, dist_scale, w, chains):
    delta = np.arange(BAND_SUB + 2 * w)[None, :] - w - np.arange(BAND_SUB)[:, None]
    bias = -slopes[:, None, None] * (dist_scale * np.abs(delta)).astype(np.float32)[None]
    tbl = np.where(np.abs(delta)[None] <= w, bias * LOG2E, NEG_INF).astype(np.float32)
    return tbl.reshape(chains, -1, BAND_SUB + 2 * w)


def _banded_kernel(*refs, w, tq, n, kvmap, has_sink, has_lse):
    refs = list(refs)
    sink_ref = refs.pop(0) if has_sink else None
    q_ref, kp_ref, kc_ref, kn_ref, vp_ref, vc_ref, vn_ref, tbl_ref, o_ref = refs[:9]
    lse_ref = refs[9] if has_lse else None
    qs, kbuf, vbuf = refs[-3:]
    sub = BAND_SUB
    nchains = len(kvmap)
    blocks = q_ref.shape[1] // LANES // nchains
    i = pl.program_id(1)
    lane, hm = _half_masks()

    def fill(buf, lo, hi, ref):
        if buf.shape[1] == ref.shape[1]:
            buf[lo:hi] = ref[...]
        else:
            x = ref[...].astype(F32)
            swapped = pltpu.roll(x, HEAD_DIM, 1)
            buf[lo:hi, 0:LANES] = jnp.where(lane < HEAD_DIM, x, swapped).astype(buf.dtype)
            buf[lo:hi, LANES:2 * LANES] = jnp.where(lane < HEAD_DIM, swapped, x).astype(buf.dtype)

    for buf, (prev_ref, cur_ref, next_ref) in ((kbuf, (kp_ref, kc_ref, kn_ref)), (vbuf, (vp_ref, vc_ref, vn_ref))):
        fill(buf, 0, w, prev_ref)
        fill(buf, w, w + tq, cur_ref)
        fill(buf, w + tq, 2 * w + tq, next_ref)
    for j in range(tq // sub):
        for pi in range(nchains * blocks):
            qpair = q_ref[j * sub:(j + 1) * sub, pi * LANES:(pi + 1) * LANES]
            for e in range(2):
                qs[j, (2 * pi + e) * sub:(2 * pi + e + 1) * sub, :] = qpair * hm[e]
    nsub = tq // sub
    rows = 2 * blocks * sub
    nkeys = sub + 2 * w
    range_bias = {}
    for j in {0, nsub - 1}:
        kpos = i * tq + j * sub - w + lax.broadcasted_iota(jnp.int32, (1, nkeys), 1)
        range_bias[j] = jnp.where(kpos < 0, NEG_INF, jnp.where(kpos >= n, NEG_INF, 0.0))
    for c, kv in enumerate(kvmap):
        kvcols = slice(kv * LANES, (kv + 1) * LANES)
        tbl = tbl_ref[c]
        parts = []
        for j in range(nsub):
            sj = _dot_nt(qs[j, c * rows:(c + 1) * rows, :], kbuf[j * sub:j * sub + nkeys, kvcols]) + tbl
            parts.append(sj + range_bias[j] if j in range_bias else sj)
        s = jnp.concatenate(parts, axis=0)
        m = jnp.max(s, axis=-1, keepdims=True)
        if has_sink:
            sink = jnp.concatenate([jnp.full((sub, LANES), sink_ref[c * 2 * blocks + h] * LOG2E, F32)
                                    for h in range(2 * blocks)] * nsub, axis=0)
            m = jnp.maximum(m, sink)
        else:
            m = jnp.broadcast_to(m, (m.shape[0], LANES))
        p = jnp.concatenate([jnp.exp2(s[:, k * LANES:(k + 1) * LANES] - m) for k in range(nkeys // LANES)], axis=1)
        l = jnp.sum(p, axis=-1, keepdims=True)
        if has_sink:
            l = l + jnp.exp2(sink - m)
        else:
            l = jnp.broadcast_to(l, (l.shape[0], LANES))
        inv = 1.0 / l
        lse = m + jnp.log2(l)
        p = p.astype(BF16)
        for j in range(nsub):
            jr = slice(j * rows, (j + 1) * rows)
            o = jnp.dot(p[jr], vbuf[j * sub:j * sub + nkeys, kvcols], preferred_element_type=F32) * inv[jr]
            for x in range(blocks):
                cols = slice((c * blocks + x) * LANES, (c * blocks + x + 1) * LANES)
                even, odd = slice(2 * x * sub, (2 * x + 1) * sub), slice((2 * x + 1) * sub, (2 * x + 2) * sub)
                o_ref[j * sub:(j + 1) * sub, cols] = jnp.where(lane < HEAD_DIM, o[even], o[odd]).astype(o_ref.dtype)
                if has_lse:
                    lse_j = lse[jr]
                    lse_ref[j * sub:(j + 1) * sub, cols] = jnp.where(lane < HEAD_DIM, lse_j[even], lse_j[odd])


def _banded(arr, *, w, col_q, col_k, col_v, wq, wkv, kvmap, table, sink, out_dtype, has_lse, name):
    assert w <= BAND_SUB
    classes, n, _ = arr.shape
    tq = min(512, n)
    sub = tq // w
    nblk = n // w

    def qspec(col, width):
        return pl.BlockSpec((None, tq, width), lambda c, i: (c, i, col // width))

    def prev_spec(col, width):
        return pl.BlockSpec((None, w, width), lambda c, i: (c, jnp.maximum(i * sub - 1, 0), col // width))

    def next_spec(col, width):
        return pl.BlockSpec((None, w, width), lambda c, i: (c, jnp.minimum((i + 1) * sub, nblk - 1), col // width))

    in_specs = [qspec(col_q, wq),
                prev_spec(col_k, wkv), qspec(col_k, wkv), next_spec(col_k, wkv),
                prev_spec(col_v, wkv), qspec(col_v, wkv), next_spec(col_v, wkv),
                pl.BlockSpec(table.shape, lambda c, i: (0, 0, 0))]
    args = [arr] * 7 + [jnp.asarray(table)]
    if sink is not None:
        in_specs = [pl.BlockSpec(memory_space=pltpu.SMEM)] + in_specs
        args = [sink.astype(F32)] + args
    out_block = pl.BlockSpec((None, tq, wq), lambda c, i: (c, i, 0))
    out_shape = [jax.ShapeDtypeStruct((classes, n, wq), out_dtype)]
    out_specs = [out_block]
    if has_lse:
        out_shape.append(jax.ShapeDtypeStruct((classes, n, wq), F32))
        out_specs.append(out_block)
    return pl.pallas_call(
        functools.partial(_banded_kernel, w=w, tq=tq, n=n, kvmap=kvmap, has_sink=sink is not None, has_lse=has_lse),
        grid=(classes, n // tq),
        in_specs=in_specs,
        out_specs=out_specs,
        out_shape=out_shape,
        scratch_shapes=[pltpu.VMEM((tq // BAND_SUB, 2 * (wq // LANES) * BAND_SUB, LANES), BF16),
                        pltpu.VMEM((tq + 2 * w, len(kvmap) * LANES), BF16),
                        pltpu.VMEM((tq + 2 * w, len(kvmap) * LANES), BF16)],
        compiler_params=pltpu.CompilerParams(
            dimension_semantics=("parallel", "arbitrary"), vmem_limit_bytes=V7X_VMEM_LIMIT),
        name=name,
    )(*args)


def _rope_tables(seq):
    t = jnp.arange(seq)
    npair = HEAD_DIM // 4
    freqs = ROPE_THETA ** (-jnp.arange(npair, dtype=F32) / npair)

    def cs(pos):
        ang = pos.astype(F32)[:, None] * freqs[None, :]
        return jnp.cos(ang), jnp.sin(ang)

    cr, sr = cs(t // GRID_W)
    cc, sc = cs(t % GRID_W)
    cos = jnp.concatenate([cr, cr, cc, cc], axis=-1)
    sin = jnp.concatenate([-sr, sr, -sc, sc], axis=-1)
    return jnp.tile(cos, (1, 2)), jnp.tile(sin, (1, 2))


def _cprep_kernel(q_ref, k_ref, v_ref, cos_ref, sin_ref, qg_ref, kg_ref, avg_ref, qo_ref, ko_ref, ve_ref, vo_ref):
    lane = lax.broadcasted_iota(jnp.int32, (1, LANES), 1)
    own_half = (lane // HEAD_DIM) == pl.program_id(1)

    def group_pair(x):
        x = x.astype(F32)
        return jnp.where(own_half, x, pltpu.roll(x, HEAD_DIM, 1))

    v = group_pair(v_ref[...])
    ve_ref[...] = jnp.where(lane < HEAD_DIM, v, 1.0).astype(ve_ref.dtype)
    vo_ref[...] = jnp.where(lane < HEAD_DIM, 1.0, v).astype(vo_ref.dtype)
    first = (lane & (HEAD_DIM // 4)) == 0
    cos = cos_ref[...]
    sin = sin_ref[...]
    avg = avg_ref[...]

    def norm_rope(x, gain):
        x = x.astype(F32)
        sq = x * x
        hi = sq.astype(BF16)
        r1 = sq - hi.astype(F32)
        mid = r1.astype(BF16)
        lo = (r1 - mid.astype(F32)).astype(BF16)
        ms = (jnp.dot(hi, avg, preferred_element_type=F32) + jnp.dot(mid, avg, preferred_element_type=F32)
              + jnp.dot(lo, avg, preferred_element_type=F32))
        y = x * lax.rsqrt(ms + RMS_EPS) * gain
        quarter = HEAD_DIM // 4
        partner = jnp.where(first, pltpu.roll(y, LANES - quarter, 1), pltpu.roll(y, quarter, 1))
        return y * cos + partner * sin

    for c in range(q_ref.shape[1] // LANES):
        cols = slice(c * LANES, (c + 1) * LANES)
        qo_ref[:, cols] = norm_rope(q_ref[:, cols], qg_ref[...]).astype(qo_ref.dtype)
    ko_ref[...] = norm_rope(group_pair(k_ref[...]), kg_ref[...]).astype(ko_ref.dtype)


def _cprep(p, cos, sin, q_gain, k_gain, batch, seq, tm=2048):
    t = batch * seq
    nt = seq // tm
    avg = np.zeros((LANES, LANES), np.float32)
    avg[:HEAD_DIM, :HEAD_DIM] = 1.0 / HEAD_DIM
    avg[HEAD_DIM:, HEAD_DIM:] = 1.0 / HEAD_DIM
    gw = C_QW // GA_KV_HEADS
    qg = jnp.tile(q_gain.astype(F32) * (HEAD_DIM ** -0.5 * np.log2(np.e)), 2).reshape(1, LANES)
    kg = jnp.tile(k_gain.astype(F32), 2).reshape(1, LANES)
    return pl.pallas_call(
        _cprep_kernel,
        grid=(t // tm, GA_KV_HEADS),
        in_specs=[pl.BlockSpec((tm, gw), lambda i, g: (i, COL_CQ // gw + g)),
                  pl.BlockSpec((tm, LANES), lambda i, g: (i, COL_CK // LANES)),
                  pl.BlockSpec((tm, LANES), lambda i, g: (i, COL_CV // LANES)),
                  pl.BlockSpec((tm, LANES), lambda i, g: (i % nt, 0)),
                  pl.BlockSpec((tm, LANES), lambda i, g: (i % nt, 0)),
                  pl.BlockSpec((1, LANES), lambda i, g: (0, 0)),
                  pl.BlockSpec((1, LANES), lambda i, g: (0, 0)),
                  pl.BlockSpec((LANES, LANES), lambda i, g: (0, 0))],
        out_specs=[pl.BlockSpec((tm, gw), lambda i, g: (i, g))] + [pl.BlockSpec((tm, LANES), lambda i, g: (i, g))] * 3,
        out_shape=[jax.ShapeDtypeStruct((t, C_QW), BF16)] + [jax.ShapeDtypeStruct((t, GA_KV_HEADS * LANES), BF16)] * 3,
        compiler_params=pltpu.CompilerParams(
            dimension_semantics=("parallel", "arbitrary"), vmem_limit_bytes=V7X_VMEM_LIMIT),
        name="cprep",
    )(p, p, p, cos, sin, qg, kg, jnp.asarray(avg, BF16))


def _flash_kernel(q_ref, k_ref, ve_ref, vo_ref, o_ref, qs_ref, m_ref, acc_ref, *, tk):
    nkv = k_ref.shape[0] // tk
    nheads = qs_ref.shape[0]
    lane, hm = _half_masks()
    for h in range(nheads):
        qs_ref[h] = q_ref[:, (h // 2) * LANES:(h // 2 + 1) * LANES] * hm[h % 2]
    m_ref[...] = jnp.full(m_ref.shape, NEG_INF, F32)
    acc_ref[...] = jnp.zeros(acc_ref.shape, F32)

    def body(t, carry):
        rows = pl.ds(pl.multiple_of(t * tk, tk), tk)
        k = k_ref[rows, :]
        vs = (ve_ref[rows, :], vo_ref[rows, :])
        for h in range(nheads):
            s = _dot_nt(qs_ref[h], k)
            m_old = m_ref[h]
            m_new = jnp.maximum(m_old, jnp.max(s, axis=-1, keepdims=True))
            alpha = jnp.exp2(m_old - m_new)
            p = jnp.concatenate([jnp.exp2(s[:, c * LANES:(c + 1) * LANES] - m_new) for c in range(tk // LANES)],
                                axis=1)
            acc_ref[h] = alpha * acc_ref[h] + jnp.dot(p.astype(BF16), vs[h % 2], preferred_element_type=F32)
            m_ref[h] = m_new
        return carry

    lax.fori_loop(0, nkv, body, 0)
    for pi in range(nheads // 2):
        even, odd = acc_ref[2 * pi], acc_ref[2 * pi + 1]
        even = even / pltpu.roll(even, HEAD_DIM, 1)
        odd = odd / pltpu.roll(odd, HEAD_DIM, 1)
        o_ref[:, pi * LANES:(pi + 1) * LANES] = jnp.where(lane < HEAD_DIM, even, odd).astype(o_ref.dtype)


def _flash(q, k, ve, vo, batch, seq, tq=2048, tk=1024):
    t = batch * seq
    nq = seq // tq
    gw = C_QW // GA_KV_HEADS
    nheads = GA_Q_HEADS // GA_KV_HEADS
    kv_spec = pl.BlockSpec((seq, LANES), lambda b, g, i: (b, g))
    return pl.pallas_call(
        functools.partial(_flash_kernel, tk=tk),
        grid=(batch, GA_KV_HEADS, nq),
        in_specs=[pl.BlockSpec((tq, gw), lambda b, g, i: (b * nq + i, g)), kv_spec, kv_spec, kv_spec],
        out_specs=pl.BlockSpec((tq, gw), lambda b, g, i: (b * nq + i, g)),
        out_shape=jax.ShapeDtypeStruct((t, C_QW), BF16),
        scratch_shapes=[pltpu.VMEM((nheads, tq, LANES), BF16), pltpu.VMEM((nheads, tq, LANES), F32),
                        pltpu.VMEM((nheads, tq, LANES), F32)],
        compiler_params=pltpu.CompilerParams(
            dimension_semantics=("parallel", "parallel", "arbitrary"), vmem_limit_bytes=V7X_VMEM_LIMIT),
        name="flash",
    )(q, k, ve, vo)


def _merge_kernel(x_ref, ya_ref, ob0_ref, ob1_ref, ob2_ref, ls0_ref, ls1_ref, ls2_ref, yc_ref, yd_ref,
                  ga_ref, gb_ref, gc_ref, gd_ref, wa_ref, wb_ref, wc_ref, wd_ref, wo_ref, gn_ref,
                  xo_ref, ho_ref, stage):
    tm = x_ref.shape[0]
    slots = iter(range(stage.shape[0]))

    def token_order(ref):
        dil = ref.shape[0]
        if dil == 1:
            return ref[0]
        slot = next(slots)
        for r in range(dil):
            for s in range(B_OUT // LANES):
                stage[slot, s, pl.ds(r, tm // dil, stride=dil), :] = ref[r, :, s * LANES:(s + 1) * LANES]
        return jnp.concatenate([stage[slot, s] for s in range(B_OUT // LANES)], axis=1)

    l0, l1, l2 = token_order(ls0_ref), token_order(ls1_ref), token_order(ls2_ref)
    mx = jnp.maximum(jnp.maximum(l0, l1), l2)
    e0, e1, e2 = jnp.exp2(l0 - mx), jnp.exp2(l1 - mx), jnp.exp2(l2 - mx)
    yb = (e0 * token_order(ob0_ref) + e1 * token_order(ob1_ref) + e2 * token_order(ob2_ref)) / (e0 + e1 + e2)

    def branch(g_ref, y, w_ref):
        return g_ref[...].astype(F32) * jnp.dot(y, w_ref[...], preferred_element_type=F32)

    merged = (branch(ga_ref, ya_ref[...], wa_ref) + branch(gb_ref, yb.astype(BF16), wb_ref)
              + branch(gc_ref, yc_ref[...], wc_ref) + branch(gd_ref, yd_ref[...], wd_ref))
    out = x_ref[...] + jnp.dot(merged.astype(BF16), wo_ref[...], preferred_element_type=F32)
    xo_ref[...] = out
    ho_ref[...] = _rms_rows(out, gn_ref[...]).astype(ho_ref.dtype)


def _merge(x, ya, obs, lss, yc, yd, gates, wa, wb, wc, wd, wo, gain, layer, seq, tm=256):
    t, d = x.shape
    nb = seq // tm

    def rows(width, col=0):
        return pl.BlockSpec((tm, width), lambda i: (i, col))

    def classes(arr):
        dil = arr.shape[1]
        return pl.BlockSpec((None, dil, tm // dil, B_OUT), lambda i: (i // nb, 0, i % nb, 0))

    def const(arr):
        return pl.BlockSpec((None,) + arr.shape[1:], lambda i: (layer, 0, 0), pipeline_mode=pl.Buffered(1))

    gain = gain.reshape(1, d)
    return pl.pallas_call(
        _merge_kernel,
        grid=(t // tm,),
        in_specs=[rows(d), rows(A_W)] + [classes(a) for a in obs] + [classes(a) for a in lss]
                 + [rows(C_QW), rows(D_QW), rows(d, 0), rows(d, 1), rows(d, 2), rows(d, 3),
                    const(wa), const(wb), const(wc), const(wd), const(wo),
                    pl.BlockSpec((1, d), lambda i: (0, 0), pipeline_mode=pl.Buffered(1))],
        out_specs=[rows(d), rows(d)],
        out_shape=[jax.ShapeDtypeStruct((t, d), F32), jax.ShapeDtypeStruct((t, d), BF16)],
        scratch_shapes=[pltpu.VMEM((2 * len(DILATED), B_OUT // LANES, tm, LANES), F32)],
        compiler_params=pltpu.CompilerParams(dimension_semantics=("parallel",), vmem_limit_bytes=V7X_VMEM_LIMIT),
        name="merge",
    )(x, ya, *obs, *lss, yc, yd, gates, gates, gates, gates, wa, wb, wc, wd, wo, gain)


def _mlp_kernel(h_ref, x_ref, wu_ref, wd_ref, gn_ref, *rest, final, row_chunk):
    if final:
        yo_ref, acc_ref = rest
    else:
        xo_ref, ho_ref, acc_ref = rest
    f = pl.program_id(1)

    @pl.when(f == 0)
    def _():
        acc_ref[...] = x_ref[...]

    for r in range(h_ref.shape[0] // row_chunk):
        rows = pl.ds(r * row_chunk, row_chunk)
        u = jnp.dot(h_ref[rows, :], wu_ref[...], preferred_element_type=F32)
        a = jnp.square(jnp.maximum(u, 0.0)).astype(BF16)
        acc_ref[rows, :] += jnp.dot(a, wd_ref[...], preferred_element_type=F32)

    @pl.when(f == pl.num_programs(1) - 1)
    def _():
        out = acc_ref[...]
        normed = _rms_rows(out, gn_ref[...])
        if final:
            yo_ref[...] = normed
        else:
            xo_ref[...] = out
            ho_ref[...] = normed.astype(ho_ref.dtype)


def _mlp(h, x, wu, wd, gain, layer, *, final, tm=512, tf=1024):
    t, d = x.shape
    ff = wu.shape[2]
    rows = pl.BlockSpec((tm, d), lambda i, f: (i, 0))
    if final:
        out_specs = [rows]
        out_shape = [jax.ShapeDtypeStruct((t, d), F32)]
    else:
        out_specs = [rows, rows]
        out_shape = [jax.ShapeDtypeStruct((t, d), F32), jax.ShapeDtypeStruct((t, d), BF16)]
    return pl.pallas_call(
        functools.partial(_mlp_kernel, final=final, row_chunk=256),
        grid=(t // tm, ff // tf),
        in_specs=[rows, rows,
                  pl.BlockSpec((None, d, tf), lambda i, f: (layer, 0, f)),
                  pl.BlockSpec((None, tf, d), lambda i, f: (layer, f, 0)),
                  pl.BlockSpec((1, d), lambda i, f: (0, 0))],
        out_specs=out_specs,
        out_shape=out_shape,
        scratch_shapes=[pltpu.VMEM((tm, d), F32)],
        compiler_params=pltpu.CompilerParams(
            dimension_semantics=("parallel", "arbitrary"), vmem_limit_bytes=V7X_VMEM_LIMIT),
        name="mlp_final" if final else "mlp",
    )(h, x, wu, wd, gain.reshape(1, d))


def _prep_in_weights(w_in):
    src, scale = [], []
    for col, width, is_query in _mix_segments():
        src += [col // LANES + c for c in range(width // LANES)]
        scale += [HEAD_DIM ** -0.5 * LOG2E if is_query else 1.0] * width
    w_mix = _relayout_cols(w_in, np.asarray(src, np.int32), np.asarray(scale, np.float32), LANES)
    gate_blk = 256
    gate0 = sum(SPLIT_SIZES[:-1]) // gate_blk
    w_gate = _relayout_cols(w_in, gate0 + np.arange(GATE_W // gate_blk, dtype=np.int32),
                            np.ones((GATE_W,), np.float32), gate_blk)
    return w_mix, w_gate


def _relayout_kernel(src_ref, w_ref, s_ref, o_ref):
    o_ref[...] = (w_ref[...] * s_ref[...]).astype(o_ref.dtype)


def _relayout_cols(w, src_blocks, scale, blk):
    depth, k, _ = w.shape
    nout = len(src_blocks)
    return pl.pallas_call(
        _relayout_kernel,
        grid_spec=pltpu.PrefetchScalarGridSpec(
            num_scalar_prefetch=1,
            grid=(depth, nout),
            in_specs=[pl.BlockSpec((None, k, blk), lambda l, c, src: (l, 0, src[c])),
                      pl.BlockSpec((1, blk), lambda l, c, src: (0, c))],
            out_specs=pl.BlockSpec((None, k, blk), lambda l, c, src: (l, 0, c))),
        out_shape=jax.ShapeDtypeStruct((depth, k, nout * blk), BF16),
        compiler_params=pltpu.CompilerParams(
            dimension_semantics=("parallel", "arbitrary"), vmem_limit_bytes=V7X_VMEM_LIMIT),
        name="relayout_cols",
    )(jnp.asarray(src_blocks), w, jnp.asarray(scale).reshape(1, nout * blk))


def _trunk(x, wts, tables):
    batch, seq, d = x.shape
    t = batch * seq
    depth = wts["w_mix"].shape[0]
    cos, sin = _rope_tables(seq)
    x2 = x.reshape(t, d)
    h = _rmsnorm(x2, wts["norm_mix"][0])
    b_slopes = _alibi_slopes(DIL_HEADS)
    d_table = _band_table(_alibi_slopes(SW_Q_HEADS), 1, SW_HALF_WINDOW, SW_KV_HEADS)
    for l in range(depth):
        p, *folded = _proj_mix(h, wts["w_mix"], l, batch, seq)
        gates = _proj(h, wts["w_gate"], l, sigmoid=True, tm=2048, name="proj_gate")
        ya = _natten(p, tables["natten"], l, batch, seq)
        obs, lss = [], []
        hg = DIL_HEADS_PER_GROUP
        p3 = p.reshape(batch, seq, P_WIDTH)
        for g, (win, dil) in enumerate(DIL_PATTERNS):
            if dil == 1:
                arr, cols = p3, (COL_B0Q, COL_B0K, COL_B0V)
            else:
                arr = folded[DILATED.index(dil)].reshape(batch * dil, seq // dil, BD_WIDTH)
                cols = (0, B_OUT, 2 * B_OUT)
            ob, ls = _banded(arr, w=win // (2 * dil), col_q=cols[0], col_k=cols[1], col_v=cols[2],
                             wq=B_OUT, wkv=B_OUT, kvmap=(0, 1),
                             table=_band_table(b_slopes[g * hg:(g + 1) * hg], dil, win // (2 * dil), 2),
                             sink=None, out_dtype=F32, has_lse=True, name="banded_d%d" % dil)
            obs.append(ob.reshape(batch, dil, seq // dil, B_OUT))
            lss.append(ls.reshape(batch, dil, seq // dil, B_OUT))
        qr, kr, ve, vo = _cprep(p, cos, sin, wts["q_gain_c"][l], wts["k_gain_c"][l], batch, seq)
        yc = _flash(qr, kr, ve, vo, batch, seq)
        (yd,) = _banded(p3, w=SW_HALF_WINDOW, col_q=COL_DQ, col_k=COL_DK, col_v=COL_DV, wq=D_QW, wkv=D_KVW,
                        kvmap=(0, 1), table=d_table, sink=wts["sink_d"][l], out_dtype=BF16, has_lse=False,
                        name="banded_sink")
        yd = yd.reshape(t, D_QW)
        x2, h = _merge(x2, ya, obs, lss, yc, yd, gates, wts["w_a"], wts["w_b"], wts["w_c"], wts["w_d"],
                       wts["w_out"], wts["norm_mlp"][l], l, seq)
        if l + 1 < depth:
            x2, h = _mlp(h, x2, wts["w_up"], wts["w_down"], wts["norm_mix"][l + 1], l, final=False)
        else:
            (y,) = _mlp(h, x2, wts["w_up"], wts["w_down"], wts["norm_final"], l, final=True)
    return y.reshape(batch, seq, d)


def kernel(x_prompt, x_sample, norm_mix, w_in, rel_bias_a, q_gain_c, k_gain_c, sink_d, w_branch_a, w_branch_b,
           w_branch_c, w_branch_d, w_out, norm_mlp, w_up, w_down, norm_final):
    w_mix, w_gate = _prep_in_weights(w_in)
    wts = dict(norm_mix=norm_mix, w_mix=w_mix, w_gate=w_gate, q_gain_c=q_gain_c, k_gain_c=k_gain_c, sink_d=sink_d,
               w_a=w_branch_a.astype(BF16), w_b=w_branch_b.astype(BF16), w_c=w_branch_c.astype(BF16),
               w_d=w_branch_d.astype(BF16), w_out=w_out.astype(BF16), norm_mlp=norm_mlp,
               w_up=w_up.astype(BF16), w_down=w_down.astype(BF16), norm_final=norm_final)
    tables = dict(natten=jax.vmap(_natten_bias_table)(rel_bias_a))
    return (_trunk(x_prompt, wts, tables), _trunk(x_sample, wts, tables))
```

```python
import functools

import numpy as np
import jax
import jax.numpy as jnp
from jax import lax
from jax.experimental import pallas as pl
from jax.experimental.pallas import tpu as pltpu

F32 = jnp.float32
BF16 = jnp.bfloat16

D_MODEL = 2048
HEAD_DIM = 64
GRID_W = 64
NA_HEADS = 8
NA_ROWS = 8
NA_COLS = 16
DIL_PATTERNS = ((128, 1), (512, 4), (2048, 16))
DIL_HEADS_PER_GROUP = 4
DIL_HEADS = 12
DIL_BLOCK = 64
GA_Q_HEADS = 8
GA_KV_HEADS = 2
ROPE_THETA = 10000.0
SW_Q_HEADS = 8
SW_KV_HEADS = 2
SW_HALF_WINDOW = 128
N_BRANCHES = 4
D_FF = 4 * D_MODEL
RMS_EPS = 1e-6
NEG_INF = -1e30
LOG2E = float(np.log2(np.e))

A_W = NA_HEADS * HEAD_DIM
B_W = DIL_HEADS * HEAD_DIM
B_OUT = DIL_HEADS_PER_GROUP * HEAD_DIM
C_QW = GA_Q_HEADS * HEAD_DIM
C_KVW = GA_KV_HEADS * HEAD_DIM
D_QW = SW_Q_HEADS * HEAD_DIM
D_KVW = SW_KV_HEADS * HEAD_DIM
GATE_W = N_BRANCHES * D_MODEL
SPLIT_SIZES = (A_W, A_W, A_W, B_W, B_W, B_W, C_QW, C_KVW, C_KVW, D_QW, D_KVW, D_KVW, GATE_W)

LANES = 128
V7X_VMEM_LIMIT = 56 * 1024 * 1024

COL_AQ = 0
COL_AK = 512
COL_AV = 1024
COL_DK = 1536
COL_DV = 1664
COL_CQ = 1792
COL_B0Q = 2304
COL_DQ = 2560
COL_B0K = 3072
COL_B0V = 3328
COL_CK = 3584
COL_CV = 3712
COL_BD = 3840
P_WIDTH = 5376
MIX_TN = 1792
GATE_TN = 2048
BD_TILE = COL_BD // MIX_TN
BD_LOCAL = COL_BD % MIX_TN
BD_WIDTH = 3 * B_OUT
DILATED = tuple(d for _, d in DIL_PATTERNS if d > 1)


def _mix_segments():
    offs = dict(zip("aq ak av bq bk bv cq ck cv dq dk dv".split(), np.cumsum((0,) + SPLIT_SIZES)[:12]))
    b = lambda name, g: offs[name] + g * B_OUT
    segs = [(offs["aq"], A_W, True), (offs["ak"], A_W, False), (offs["av"], A_W, False),
            (offs["dk"], D_KVW, False), (offs["dv"], D_KVW, False),
            (offs["cq"], C_QW, False), (b("bq", 0), B_OUT, True), (offs["dq"], D_QW, True),
            (b("bk", 0), B_OUT, False), (b("bv", 0), B_OUT, False),
            (offs["ck"], C_KVW, False), (offs["cv"], C_KVW, False)]
    for g in (1, 2):
        segs += [(b("bq", g), B_OUT, True), (b("bk", g), B_OUT, False), (b("bv", g), B_OUT, False)]
    assert sum(s[1] for s in segs) == P_WIDTH
    return segs


def _alibi_slopes(n):
    return (2.0 ** (-8.0 * np.arange(1, n + 1) / n)).astype(np.float32)


def _half_masks():
    lane = lax.broadcasted_iota(jnp.int32, (1, LANES), 1)
    lo = jnp.where(lane < HEAD_DIM, 1.0, 0.0).astype(BF16)
    hi = jnp.where(lane >= HEAD_DIM, 1.0, 0.0).astype(BF16)
    return lane, (lo, hi)


def _dot_nt(a, b):
    return lax.dot_general(a, b, (((1,), (1,)), ((), ())), preferred_element_type=F32)


def _rms_rows(x, gain):
    ms = jnp.mean(x * x, axis=-1, keepdims=True)
    return x * lax.rsqrt(ms + RMS_EPS) * gain


def _rmsnorm_kernel(x_ref, g_ref, o_ref):
    o_ref[...] = _rms_rows(x_ref[...], g_ref[...]).astype(o_ref.dtype)


def _rmsnorm(x, gain, tm=512):
    t, d = x.shape
    return pl.pallas_call(
        _rmsnorm_kernel,
        grid=(t // tm,),
        in_specs=[pl.BlockSpec((tm, d), lambda i: (i, 0)), pl.BlockSpec((1, d), lambda i: (0, 0))],
        out_specs=pl.BlockSpec((tm, d), lambda i: (i, 0)),
        out_shape=jax.ShapeDtypeStruct((t, d), BF16),
        compiler_params=pltpu.CompilerParams(dimension_semantics=("parallel",), vmem_limit_bytes=V7X_VMEM_LIMIT),
        name="rmsnorm",
    )(x, gain.reshape(1, d))


def _proj_kernel(h_ref, w_ref, o_ref, *, row_chunk, sigmoid):
    for r in range(h_ref.shape[0] // row_chunk):
        rows = pl.ds(r * row_chunk, row_chunk)
        acc = jnp.dot(h_ref[rows, :], w_ref[...], preferred_element_type=F32)
        if sigmoid:
            acc = 1.0 / (1.0 + jnp.exp(-acc))
        o_ref[rows, :] = acc.astype(o_ref.dtype)


def _proj(h, w, layer, *, sigmoid, tm=1024, tn=GATE_TN, name):
    t, k = h.shape
    n = w.shape[2]
    return pl.pallas_call(
        functools.partial(_proj_kernel, row_chunk=256, sigmoid=sigmoid),
        grid=(t // tm, n // tn),
        in_specs=[pl.BlockSpec((tm, k), lambda i, j: (i, 0)),
                  pl.BlockSpec((None, k, tn), lambda i, j: (layer, 0, j))],
        out_specs=pl.BlockSpec((tm, tn), lambda i, j: (i, j)),
        out_shape=jax.ShapeDtypeStruct((t, n), BF16),
        compiler_params=pltpu.CompilerParams(
            dimension_semantics=("parallel", "arbitrary"), vmem_limit_bytes=V7X_VMEM_LIMIT),
        name=name,
    )(h, w)


def _proj_mix_kernel(h_ref, w_ref, o_ref, *rest, row_chunk):
    (f4_ref, f16_ref), (stage, stage4) = rest[:2], rest[2:]
    j = pl.program_id(1)
    tm = h_ref.shape[0]
    slabs = BD_WIDTH // LANES
    quarter = tm // 4
    for r in range(tm // row_chunk):
        rows = pl.ds(r * row_chunk, row_chunk)
        acc = jnp.dot(h_ref[rows, :], w_ref[...], preferred_element_type=F32)
        o_ref[rows, :] = acc.astype(o_ref.dtype)
        for s in range(2 * slabs):
            stage[s, rows, :] = acc[:, BD_LOCAL + s * LANES:BD_LOCAL + (s + 1) * LANES]

    @pl.when(j == BD_TILE)
    def _():
        for s in range(slabs):
            for r4 in range(4):
                f4_ref[r4, :, s * LANES:(s + 1) * LANES] = stage[s, pl.ds(r4, quarter, stride=4), :].astype(
                    f4_ref.dtype)
                stage4[s, r4 * quarter:(r4 + 1) * quarter, :] = stage[slabs + s, pl.ds(r4, quarter, stride=4), :]
        for s in range(slabs):
            for r4 in range(4):
                for r4b in range(4):
                    part = stage4[s, pl.ds(r4 * quarter + r4b, quarter // 4, stride=4), :]
                    f16_ref[4 * r4b + r4, :, s * LANES:(s + 1) * LANES] = part.astype(f16_ref.dtype)


def _proj_mix(h, w, layer, batch, seq, tm=1024):
    assert DILATED == (4, 16), "the fold in _proj_mix_kernel is written for dilations 4 and 16"
    t, k = h.shape
    nb = seq // tm
    fold_specs = [pl.BlockSpec((None, dil, tm // dil, BD_WIDTH), lambda i, j: (i // nb, 0, i % nb, 0))
                  for dil in DILATED]
    fold_shapes = [jax.ShapeDtypeStruct((batch, dil, seq // dil, BD_WIDTH), BF16) for dil in DILATED]
    return pl.pallas_call(
        functools.partial(_proj_mix_kernel, row_chunk=256),
        grid=(t // tm, P_WIDTH // MIX_TN),
        in_specs=[pl.BlockSpec((tm, k), lambda i, j: (i, 0)),
                  pl.BlockSpec((None, k, MIX_TN), lambda i, j: (layer, 0, j))],
        out_specs=[pl.BlockSpec((tm, MIX_TN), lambda i, j: (i, j))] + fold_specs,
        out_shape=[jax.ShapeDtypeStruct((t, P_WIDTH), BF16)] + fold_shapes,
        scratch_shapes=[pltpu.VMEM((2 * BD_WIDTH // LANES, tm, LANES), F32),
                        pltpu.VMEM((BD_WIDTH // LANES, tm, LANES), F32)],
        compiler_params=pltpu.CompilerParams(
            dimension_semantics=("parallel", "arbitrary"), vmem_limit_bytes=V7X_VMEM_LIMIT),
        name="proj_mix",
    )(h, w)


NA_STEP_ROWS = 8
NA_STEP_TOK = NA_STEP_ROWS * GRID_W
NA_KEYS = NA_ROWS * GRID_W


def _natten_bias_table(rel_bias):
    qc = np.arange(GRID_W)[:, None]
    kc = np.arange(GRID_W)[None, :]
    c0 = np.clip(qc - NA_COLS // 2, 0, GRID_W - NA_COLS)
    valid = (kc >= c0) & (kc < c0 + NA_COLS)
    heads, ndr, ndc = rel_bias.shape
    span = 2 * GRID_W
    left = GRID_W - NA_COLS
    padded = jnp.pad(rel_bias.astype(F32), ((0, 0), (0, 0), (left, span - ndc - left)))
    skew = jnp.tile(padded, (1, 1, GRID_W))[..., :GRID_W * (span - 1)].reshape(heads, ndr, GRID_W, span - 1)
    t = skew[..., GRID_W - 1:2 * GRID_W - 1]
    t = jnp.where(valid, t * LOG2E, NEG_INF).transpose(0, 2, 1, 3)
    tbl = jnp.stack([t[:, :, NA_ROWS - 1 - s:2 * NA_ROWS - 1 - s, :] for s in range(NA_ROWS)])
    return tbl.reshape(NA_ROWS, NA_HEADS // 2, 2 * GRID_W, NA_KEYS)


def _natten_kernel(q_ref, kp_ref, kc_ref, kn_ref, vp_ref, vc_ref, vn_ref, tbl_ref, o_ref, qs, kbuf, vbuf, *, n_rows):
    i = pl.program_id(1)
    for s, (kr, vr) in enumerate(((kp_ref, vp_ref), (kc_ref, vc_ref), (kn_ref, vn_ref))):
        kbuf[s * NA_STEP_TOK:(s + 1) * NA_STEP_TOK, :] = kr[...]
        vbuf[s * NA_STEP_TOK:(s + 1) * NA_STEP_TOK, :] = vr[...]
    lane, hm = _half_masks()
    for j in range(NA_STEP_ROWS):
        for pi in range(NA_HEADS // 2):
            qpair = q_ref[j * GRID_W:(j + 1) * GRID_W, pi * LANES:(pi + 1) * LANES]
            for e in range(2):
                qs[j, pi, e * GRID_W:(e + 1) * GRID_W, :] = qpair * hm[e]

    shifts, offs = [], []
    for j in range(NA_STEP_ROWS):
        r = i * NA_STEP_ROWS + j
        r0 = jnp.clip(r - NA_ROWS // 2, 0, n_rows - NA_ROWS)
        shifts.append(r - r0)
        offs.append(pl.multiple_of((r0 - (i - 1) * NA_STEP_ROWS) * GRID_W, GRID_W))
    half_rows = NA_STEP_ROWS // 2
    for pi in range(NA_HEADS // 2):
        cols = slice(pi * LANES, (pi + 1) * LANES)
        for j0 in range(0, NA_STEP_ROWS, half_rows):
            s = jnp.concatenate([_dot_nt(qs[j, pi], kbuf[pl.ds(offs[j], NA_KEYS), cols]) + tbl_ref[shifts[j], pi]
                                 for j in range(j0, j0 + half_rows)], axis=0)
            m = jnp.broadcast_to(jnp.max(s, axis=-1, keepdims=True), (s.shape[0], LANES))
            p = jnp.concatenate([jnp.exp2(s[:, k * LANES:(k + 1) * LANES] - m) for k in range(NA_KEYS // LANES)],
                                axis=1)
            inv = 1.0 / jnp.broadcast_to(jnp.sum(p, axis=-1, keepdims=True), (s.shape[0], LANES))
            p = p.astype(BF16)
            for j in range(j0, j0 + half_rows):
                rows = slice((j - j0) * 2 * GRID_W, (j - j0 + 1) * 2 * GRID_W)
                o = jnp.dot(p[rows], vbuf[pl.ds(offs[j], NA_KEYS), cols], preferred_element_type=F32) * inv[rows]
                o_ref[j * GRID_W:(j + 1) * GRID_W, cols] = jnp.where(lane < HEAD_DIM, o[:GRID_W], o[GRID_W:]).astype(
                    o_ref.dtype)


def _natten(p, tbl, layer, batch, seq):
    t = batch * seq
    nb = seq // NA_STEP_TOK
    tok = NA_STEP_TOK

    def spec(col, which):
        cb = col // A_W
        if which == 0:
            return pl.BlockSpec((tok, A_W), lambda b, i: (b * nb + i, cb))
        if which < 0:
            return pl.BlockSpec((tok, A_W), lambda b, i: (b * nb + jnp.maximum(i - 1, 0), cb))
        return pl.BlockSpec((tok, A_W), lambda b, i: (b * nb + jnp.minimum(i + 1, nb - 1), cb))

    return pl.pallas_call(
        functools.partial(_natten_kernel, n_rows=seq // GRID_W),
        grid=(batch, nb),
        in_specs=[spec(COL_AQ, 0),
                  spec(COL_AK, -1), spec(COL_AK, 0), spec(COL_AK, 1),
                  spec(COL_AV, -1), spec(COL_AV, 0), spec(COL_AV, 1),
                  pl.BlockSpec((None,) + tbl.shape[1:], lambda b, i: (layer, 0, 0, 0, 0))],
        out_specs=pl.BlockSpec((tok, A_W), lambda b, i: (b * nb + i, 0)),
        out_shape=jax.ShapeDtypeStruct((t, A_W), BF16),
        scratch_shapes=[pltpu.VMEM((NA_STEP_ROWS, NA_HEADS // 2, 2 * GRID_W, LANES), BF16),
                        pltpu.VMEM((3 * tok, A_W), BF16), pltpu.VMEM((3 * tok, A_W), BF16)],
        compiler_params=pltpu.CompilerParams(
            dimension_semantics=("parallel", "arbitrary"), vmem_limit_bytes=V7X_VMEM_LIMIT),
        name="natten",
    )(p, p, p, p, p, p, p, tbl)


BAND_SUB = 128


def _band_table(slopes, dist_scale, w, chains):
    delta = np.arange(BAND_SUB + 2 * w)[None, :] - w - np.arange(BAND_SUB)[:, None]
    bias = -slopes[:, None, None] * (dist_scale * np.abs(delta)).astype(np.float32)[None]
    tbl = np.where(np.abs(delta)[None] <= w, bias * LOG2E, NEG_INF).astype(np.float32)
    return tbl.reshape(chains, -1, BAND_SUB + 2 * w)


def _banded_kernel(*refs, w, tq, n, kvmap, has_sink, has_lse):
    refs = list(refs)
    sink_ref = refs.pop(0) if has_sink else None
    q_ref, kp_ref, kc_ref, kn_ref, vp_ref, vc_ref, vn_ref, tbl_ref, o_ref = refs[:9]
    lse_ref = refs[9] if has_lse else None
    qs, kbuf, vbuf = refs[-3:]
    sub = BAND_SUB
    nchains = len(kvmap)
    blocks = q_ref.shape[1] // LANES // nchains
    i = pl.program_id(1)
    lane, hm = _half_masks()

    def fill(buf, lo, hi, ref):
        if buf.shape[1] == ref.shape[1]:
            buf[lo:hi] = ref[...]
        else:
            x = ref[...].astype(F32)
            swapped = pltpu.roll(x, HEAD_DIM, 1)
            buf[lo:hi, 0:LANES] = jnp.where(lane < HEAD_DIM, x, swapped).astype(buf.dtype)
            buf[lo:hi, LANES:2 * LANES] = jnp.where(lane < HEAD_DIM, swapped, x).astype(buf.dtype)

    for buf, (prev_ref, cur_ref, next_ref) in ((kbuf, (kp_ref, kc_ref, kn_ref)), (vbuf, (vp_ref, vc_ref, vn_ref))):
        fill(buf, 0, w, prev_ref)
        fill(buf, w, w + tq, cur_ref)
        fill(buf, w + tq, 2 * w + tq, next_ref)
    for j in range(tq // sub):
        for pi in range(nchains * blocks):
            qpair = q_ref[j * sub:(j + 1) * sub, pi * LANES:(pi + 1) * LANES]
            for e in range(2):
                qs[j, (2 * pi + e) * sub:(2 * pi + e + 1) * sub, :] = qpair * hm[e]
    nsub = tq // sub
    rows = 2 * blocks * sub
    nkeys = sub + 2 * w
    range_bias = {}
    for j in {0, nsub - 1}:
        kpos = i * tq + j * sub - w + lax.broadcasted_iota(jnp.int32, (1, nkeys), 1)
        range_bias[j] = jnp.where(kpos < 0, NEG_INF, jnp.where(kpos >= n, NEG_INF, 0.0))
    for c, kv in enumerate(kvmap):
        kvcols = slice(kv * LANES, (kv + 1) * LANES)
        tbl = tbl_ref[c]
        parts = []
        for j in range(nsub):
            sj = _dot_nt(qs[j, c * rows:(c + 1) * rows, :], kbuf[j * sub:j * sub + nkeys, kvcols]) + tbl
            parts.append(sj + range_bias[j] if j in range_bias else sj)
        s = jnp.concatenate(parts, axis=0)
        m = jnp.max(s, axis=-1, keepdims=True)
        if has_sink:
            sink = jnp.concatenate([jnp.full((sub, LANES), sink_ref[c * 2 * blocks + h] * LOG2E, F32)
                                    for h in range(2 * blocks)] * nsub, axis=0)
            m = jnp.maximum(m, sink)
        else:
            m = jnp.broadcast_to(m, (m.shape[0], LANES))
        p = jnp.concatenate([jnp.exp2(s[:, k * LANES:(k + 1) * LANES] - m) for k in range(nkeys // LANES)], axis=1)
        l = jnp.sum(p, axis=-1, keepdims=True)
        if has_sink:
            l = l + jnp.exp2(sink - m)
        else:
            l = jnp.broadcast_to(l, (l.shape[0], LANES))
        inv = 1.0 / l
        lse = m + jnp.log2(l)
        p = p.astype(BF16)
        for j in range(nsub):
            jr = slice(j * rows, (j + 1) * rows)
            o = jnp.dot(p[jr], vbuf[j * sub:j * sub + nkeys, kvcols], preferred_element_type=F32) * inv[jr]
            for x in range(blocks):
                cols = slice((c * blocks + x) * LANES, (c * blocks + x + 1) * LANES)
                even, odd = slice(2 * x * sub, (2 * x + 1) * sub), slice((2 * x + 1) * sub, (2 * x + 2) * sub)
                o_ref[j * sub:(j + 1) * sub, cols] = jnp.where(lane < HEAD_DIM, o[even], o[odd]).astype(o_ref.dtype)
                if has_lse:
                    lse_j = lse[jr]
                    lse_ref[j * sub:(j + 1) * sub, cols] = jnp.where(lane < HEAD_DIM, lse_j[even], lse_j[odd])


def _banded(arr, *, w, col_q, col_k, col_v, wq, wkv, kvmap, table, sink, out_dtype, has_lse, name):
    assert w <= BAND_SUB
    classes, n, _ = arr.shape
    tq = min(512, n)
    sub = tq // w
    nblk = n // w

    def qspec(col, width):
        return pl.BlockSpec((None, tq, width), lambda c, i: (c, i, col // width))

    def prev_spec(col, width):
        return pl.BlockSpec((None, w, width), lambda c, i: (c, jnp.maximum(i * sub - 1, 0), col // width))

    def next_spec(col, width):
        return pl.BlockSpec((None, w, width), lambda c, i: (c, jnp.minimum((i + 1) * sub, nblk - 1), col // width))

    in_specs = [qspec(col_q, wq),
                prev_spec(col_k, wkv), qspec(col_k, wkv), next_spec(col_k, wkv),
                prev_spec(col_v, wkv), qspec(col_v, wkv), next_spec(col_v, wkv),
                pl.BlockSpec(table.shape, lambda c, i: (0, 0, 0))]
    args = [arr] * 7 + [jnp.asarray(table)]
    if sink is not None:
        in_specs = [pl.BlockSpec(memory_space=pltpu.SMEM)] + in_specs
        args = [sink.astype(F32)] + args
    out_block = pl.BlockSpec((None, tq, wq), lambda c, i: (c, i, 0))
    out_shape = [jax.ShapeDtypeStruct((classes, n, wq), out_dtype)]
    out_specs = [out_block]
    if has_lse:
        out_shape.append(jax.ShapeDtypeStruct((classes, n, wq), F32))
        out_specs.append(out_block)
    return pl.pallas_call(
        functools.partial(_banded_kernel, w=w, tq=tq, n=n, kvmap=kvmap, has_sink=sink is not None, has_lse=has_lse),
        grid=(classes, n // tq),
        in_specs=in_specs,
        out_specs=out_specs,
        out_shape=out_shape,
        scratch_shapes=[pltpu.VMEM((tq // BAND_SUB, 2 * (wq // LANES) * BAND_SUB, LANES), BF16),
                        pltpu.VMEM((tq + 2 * w, len(kvmap) * LANES), BF16),
                        pltpu.VMEM((tq + 2 * w, len(kvmap) * LANES), BF16)],
        compiler_params=pltpu.CompilerParams(
            dimension_semantics=("parallel", "arbitrary"), vmem_limit_bytes=V7X_VMEM_LIMIT),
        name=name,
    )(*args)


def _rope_tables(seq):
    t = jnp.arange(seq)
    npair = HEAD_DIM // 4
    freqs = ROPE_THETA ** (-jnp.arange(npair, dtype=F32) / npair)

    def cs(pos):
        ang = pos.astype(F32)[:, None] * freqs[None, :]
        return jnp.cos(ang), jnp.sin(ang)

    cr, sr = cs(t // GRID_W)
    cc, sc = cs(t % GRID_W)
    cos = jnp.concatenate([cr, cr, cc, cc], axis=-1)
    sin = jnp.concatenate([-sr, sr, -sc, sc], axis=-1)
    return jnp.tile(cos, (1, 2)), jnp.tile(sin, (1, 2))


def _cprep_kernel(q_ref, k_ref, v_ref, cos_ref, sin_ref, qg_ref, kg_ref, avg_ref, qo_ref, ko_ref, ve_ref, vo_ref):
    lane = lax.broadcasted_iota(jnp.int32, (1, LANES), 1)
    own_half = (lane // HEAD_DIM) == pl.program_id(1)

    def group_pair(x):
        x = x.astype(F32)
        return jnp.where(own_half, x, pltpu.roll(x, HEAD_DIM, 1))

    v = group_pair(v_ref[...])
    ve_ref[...] = jnp.where(lane < HEAD_DIM, v, 1.0).astype(ve_ref.dtype)
    vo_ref[...] = jnp.where(lane < HEAD_DIM, 1.0, v).astype(vo_ref.dtype)
    first = (lane & (HEAD_DIM // 4)) == 0
    cos = cos_ref[...]
    sin = sin_ref[...]
    avg = avg_ref[...]

    def norm_rope(x, gain):
        x = x.astype(F32)
        sq = x * x
        hi = sq.astype(BF16)
        r1 = sq - hi.astype(F32)
        mid = r1.astype(BF16)
        lo = (r1 - mid.astype(F32)).astype(BF16)
        ms = (jnp.dot(hi, avg, preferred_element_type=F32) + jnp.dot(mid, avg, preferred_element_type=F32)
              + jnp.dot(lo, avg, preferred_element_type=F32))
        y = x * lax.rsqrt(ms + RMS_EPS) * gain
        quarter = HEAD_DIM // 4
        partner = jnp.where(first, pltpu.roll(y, LANES - quarter, 1), pltpu.roll(y, quarter, 1))
        return y * cos + partner * sin

    for c in range(q_ref.shape[1] // LANES):
        cols = slice(c * LANES, (c + 1) * LANES)
        qo_ref[:, cols] = norm_rope(q_ref[:, cols], qg_ref[...]).astype(qo_ref.dtype)
    ko_ref[...] = norm_rope(group_pair(k_ref[...]), kg_ref[...]).astype(ko_ref.dtype)


def _cprep(p, cos, sin, q_gain, k_gain, batch, seq, tm=2048):
    t = batch * seq
    nt = seq // tm
    avg = np.zeros((LANES, LANES), np.float32)
    avg[:HEAD_DIM, :HEAD_DIM] = 1.0 / HEAD_DIM
    avg[HEAD_DIM:, HEAD_DIM:] = 1.0 / HEAD_DIM
    gw = C_QW // GA_KV_HEADS
    qg = jnp.tile(q_gain.astype(F32) * (HEAD_DIM ** -0.5 * np.log2(np.e)), 2).reshape(1, LANES)
    kg = jnp.tile(k_gain.astype(F32), 2).reshape(1, LANES)
    return pl.pallas_call(
        _cprep_kernel,
        grid=(t // tm, GA_KV_HEADS),
        in_specs=[pl.BlockSpec((tm, gw), lambda i, g: (i, COL_CQ // gw + g)),
                  pl.BlockSpec((tm, LANES), lambda i, g: (i, COL_CK // LANES)),
                  pl.BlockSpec((tm, LANES), lambda i, g: (i, COL_CV // LANES)),
                  pl.BlockSpec((tm, LANES), lambda i, g: (i % nt, 0)),
                  pl.BlockSpec((tm, LANES), lambda i, g: (i % nt, 0)),
                  pl.BlockSpec((1, LANES), lambda i, g: (0, 0)),
                  pl.BlockSpec((1, LANES), lambda i, g: (0, 0)),
                  pl.BlockSpec((LANES, LANES), lambda i, g: (0, 0))],
        out_specs=[pl.BlockSpec((tm, gw), lambda i, g: (i, g))] + [pl.BlockSpec((tm, LANES), lambda i, g: (i, g))] * 3,
        out_shape=[jax.ShapeDtypeStruct((t, C_QW), BF16)] + [jax.ShapeDtypeStruct((t, GA_KV_HEADS * LANES), BF16)] * 3,
        compiler_params=pltpu.CompilerParams(
            dimension_semantics=("parallel", "arbitrary"), vmem_limit_bytes=V7X_VMEM_LIMIT),
        name="cprep",
    )(p, p, p, cos, sin, qg, kg, jnp.asarray(avg, BF16))


def _flash_kernel(q_ref, k_ref, ve_ref, vo_ref, o_ref, qs_ref, m_ref, acc_ref, *, tk):
    nkv = k_ref.shape[0] // tk
    nheads = qs_ref.shape[0]
    lane, hm = _half_masks()
    for h in range(nheads):
        qs_ref[h] = q_ref[:, (h // 2) * LANES:(h // 2 + 1) * LANES] * hm[h % 2]
    m_ref[...] = jnp.full(m_ref.shape, NEG_INF, F32)
    acc_ref[...] = jnp.zeros(acc_ref.shape, F32)

    def body(t, carry):
        rows = pl.ds(pl.multiple_of(t * tk, tk), tk)
        k = k_ref[rows, :]
        vs = (ve_ref[rows, :], vo_ref[rows, :])
        for h in range(nheads):
            s = _dot_nt(qs_ref[h], k)
            m_old = m_ref[h]
            m_new = jnp.maximum(m_old, jnp.max(s, axis=-1, keepdims=True))
            alpha = jnp.exp2(m_old - m_new)
            p = jnp.concatenate([jnp.exp2(s[:, c * LANES:(c + 1) * LANES] - m_new) for c in range(tk // LANES)],
                                axis=1)
            acc_ref[h] = alpha * acc_ref[h] + jnp.dot(p.astype(BF16), vs[h % 2], preferred_element_type=F32)
            m_ref[h] = m_new
        return carry

    lax.fori_loop(0, nkv, body, 0)
    for pi in range(nheads // 2):
        even, odd = acc_ref[2 * pi], acc_ref[2 * pi + 1]
        even = even / pltpu.roll(even, HEAD_DIM, 1)
        odd = odd / pltpu.roll(odd, HEAD_DIM, 1)
        o_ref[:, pi * LANES:(pi + 1) * LANES] = jnp.where(lane < HEAD_DIM, even, odd).astype(o_ref.dtype)


def _flash(q, k, ve, vo, batch, seq, tq=2048, tk=1024):
    t = batch * seq
    nq = seq // tq
    gw = C_QW // GA_KV_HEADS
    nheads = GA_Q_HEADS // GA_KV_HEADS
    kv_spec = pl.BlockSpec((seq, LANES), lambda b, g, i: (b, g))
    return pl.pallas_call(
        functools.partial(_flash_kernel, tk=tk),
        grid=(batch, GA_KV_HEADS, nq),
        in_specs=[pl.BlockSpec((tq, gw), lambda b, g, i: (b * nq + i, g)), kv_spec, kv_spec, kv_spec],
        out_specs=pl.BlockSpec((tq, gw), lambda b, g, i: (b * nq + i, g)),
        out_shape=jax.ShapeDtypeStruct((t, C_QW), BF16),
        scratch_shapes=[pltpu.VMEM((nheads, tq, LANES), BF16), pltpu.VMEM((nheads, tq, LANES), F32),
                        pltpu.VMEM((nheads, tq, LANES), F32)],
        compiler_params=pltpu.CompilerParams(
            dimension_semantics=("parallel", "parallel", "arbitrary"), vmem_limit_bytes=V7X_VMEM_LIMIT),
        name="flash",
    )(q, k, ve, vo)


def _merge_kernel(x_ref, ya_ref, ob0_ref, ob1_ref, ob2_ref, ls0_ref, ls1_ref, ls2_ref, yc_ref, yd_ref,
                  ga_ref, gb_ref, gc_ref, gd_ref, wa_ref, wb_ref, wc_ref, wd_ref, wo_ref, gn_ref,
                  xo_ref, ho_ref, stage):
    tm = x_ref.shape[0]
    slots = iter(range(stage.shape[0]))

    def token_order(ref):
        dil = ref.shape[0]
        if dil == 1:
            return ref[0]
        slot = next(slots)
        for r in range(dil):
            for s in range(B_OUT // LANES):
                stage[slot, s, pl.ds(r, tm // dil, stride=dil), :] = ref[r, :, s * LANES:(s + 1) * LANES]
        return jnp.concatenate([stage[slot, s] for s in range(B_OUT // LANES)], axis=1)

    l0, l1, l2 = token_order(ls0_ref), token_order(ls1_ref), token_order(ls2_ref)
    mx = jnp.maximum(jnp.maximum(l0, l1), l2)
    e0, e1, e2 = jnp.exp2(l0 - mx), jnp.exp2(l1 - mx), jnp.exp2(l2 - mx)
    yb = (e0 * token_order(ob0_ref) + e1 * token_order(ob1_ref) + e2 * token_order(ob2_ref)) / (e0 + e1 + e2)

    def branch(g_ref, y, w_ref):
        return g_ref[...].astype(F32) * jnp.dot(y, w_ref[...], preferred_element_type=F32)

    merged = (branch(ga_ref, ya_ref[...], wa_ref) + branch(gb_ref, yb.astype(BF16), wb_ref)
              + branch(gc_ref, yc_ref[...], wc_ref) + branch(gd_ref, yd_ref[...], wd_ref))
    out = x_ref[...] + jnp.dot(merged.astype(BF16), wo_ref[...], preferred_element_type=F32)
    xo_ref[...] = out
    ho_ref[...] = _rms_rows(out, gn_ref[...]).astype(ho_ref.dtype)


def _merge(x, ya, obs, lss, yc, yd, gates, wa, wb, wc, wd, wo, gain, layer, seq, tm=256):
    t, d = x.shape
    nb = seq // tm

    def rows(width, col=0):
        return pl.BlockSpec((tm, width), lambda i: (i, col))

    def classes(arr):
        dil = arr.shape[1]
        return pl.BlockSpec((None, dil, tm // dil, B_OUT), lambda i: (i // nb, 0, i % nb, 0))

    def const(arr):
        return pl.BlockSpec((None,) + arr.shape[1:], lambda i: (layer, 0, 0), pipeline_mode=pl.Buffered(1))

    gain = gain.reshape(1, d)
    return pl.pallas_call(
        _merge_kernel,
        grid=(t // tm,),
        in_specs=[rows(d), rows(A_W)] + [classes(a) for a in obs] + [classes(a) for a in lss]
                 + [rows(C_QW), rows(D_QW), rows(d, 0), rows(d, 1), rows(d, 2), rows(d, 3),
                    const(wa), const(wb), const(wc), const(wd), const(wo),
                    pl.BlockSpec((1, d), lambda i: (0, 0), pipeline_mode=pl.Buffered(1))],
        out_specs=[rows(d), rows(d)],
        out_shape=[jax.ShapeDtypeStruct((t, d), F32), jax.ShapeDtypeStruct((t, d), BF16)],
        scratch_shapes=[pltpu.VMEM((2 * len(DILATED), B_OUT // LANES, tm, LANES), F32)],
        compiler_params=pltpu.CompilerParams(dimension_semantics=("parallel",), vmem_limit_bytes=V7X_VMEM_LIMIT),
        name="merge",
    )(x, ya, *obs, *lss, yc, yd, gates, gates, gates, gates, wa, wb, wc, wd, wo, gain)


def _mlp_kernel(h_ref, x_ref, wu_ref, wd_ref, gn_ref, *rest, final, row_chunk):
    if final:
        yo_ref, acc_ref = rest
    else:
        xo_ref, ho_ref, acc_ref = rest
    f = pl.program_id(1)

    @pl.when(f == 0)
    def _():
        acc_ref[...] = x_ref[...]

    for r in range(h_ref.shape[0] // row_chunk):
        rows = pl.ds(r * row_chunk, row_chunk)
        u = jnp.dot(h_ref[rows, :], wu_ref[...], preferred_element_type=F32)
        a = jnp.square(jnp.maximum(u, 0.0)).astype(BF16)
        acc_ref[rows, :] += jnp.dot(a, wd_ref[...], preferred_element_type=F32)

    @pl.when(f == pl.num_programs(1) - 1)
    def _():
        out = acc_ref[...]
        normed = _rms_rows(out, gn_ref[...])
        if final:
            yo_ref[...] = normed
        else:
            xo_ref[...] = out
            ho_ref[...] = normed.astype(ho_ref.dtype)


def _mlp(h, x, wu, wd, gain, layer, *, final, tm=512, tf=1024):
    t, d = x.shape
    ff = wu.shape[2]
    rows = pl.BlockSpec((tm, d), lambda i, f: (i, 0))
    if final:
        out_specs = [rows]
        out_shape = [jax.ShapeDtypeStruct((t, d), F32)]
    else:
        out_specs = [rows, rows]
        out_shape = [jax.ShapeDtypeStruct((t, d), F32), jax.ShapeDtypeStruct((t, d), BF16)]
    return pl.pallas_call(
        functools.partial(_mlp_kernel, final=final, row_chunk=256),
        grid=(t // tm, ff // tf),
        in_specs=[rows, rows,
                  pl.BlockSpec((None, d, tf), lambda i, f: (layer, 0, f)),
                  pl.BlockSpec((None, tf, d), lambda i, f: (layer, f, 0)),
                  pl.BlockSpec((1, d), lambda i, f: (0, 0))],
        out_specs=out_specs,
        out_shape=out_shape,
        scratch_shapes=[pltpu.VMEM((tm, d), F32)],
        compiler_params=pltpu.CompilerParams(
            dimension_semantics=("parallel", "arbitrary"), vmem_limit_bytes=V7X_VMEM_LIMIT),
        name="mlp_final" if final else "mlp",
    )(h, x, wu, wd, gain.reshape(1, d))


def _prep_in_weights(w_in):
    src, scale = [], []
    for col, width, is_query in _mix_segments():
        src += [col // LANES + c for c in range(width // LANES)]
        scale += [HEAD_DIM ** -0.5 * LOG2E if is_query else 1.0] * width
    w_mix = _relayout_cols(w_in, np.asarray(src, np.int32), np.asarray(scale, np.float32), LANES)
    gate_blk = 256
    gate0 = sum(SPLIT_SIZES[:-1]) // gate_blk
    w_gate = _relayout_cols(w_in, gate0 + np.arange(GATE_W // gate_blk, dtype=np.int32),
                            np.ones((GATE_W,), np.float32), gate_blk)
    return w_mix, w_gate


def _relayout_kernel(src_ref, w_ref, s_ref, o_ref):
    o_ref[...] = (w_ref[...] * s_ref[...]).astype(o_ref.dtype)


def _relayout_cols(w, src_blocks, scale, blk):
    depth, k, _ = w.shape
    nout = len(src_blocks)
    return pl.pallas_call(
        _relayout_kernel,
        grid_spec=pltpu.PrefetchScalarGridSpec(
            num_scalar_prefetch=1,
            grid=(depth, nout),
            in_specs=[pl.BlockSpec((None, k, blk), lambda l, c, src: (l, 0, src[c])),
                      pl.BlockSpec((1, blk), lambda l, c, src: (0, c))],
            out_specs=pl.BlockSpec((None, k, blk), lambda l, c, src: (l, 0, c))),
        out_shape=jax.ShapeDtypeStruct((depth, k, nout * blk), BF16),
        compiler_params=pltpu.CompilerParams(
            dimension_semantics=("parallel", "arbitrary"), vmem_limit_bytes=V7X_VMEM_LIMIT),
        name="relayout_cols",
    )(jnp.asarray(src_blocks), w, jnp.asarray(scale).reshape(1, nout * blk))


def _trunk(x, wts, tables):
    batch, seq, d = x.shape
    t = batch * seq
    depth = wts["w_mix"].shape[0]
    cos, sin = _rope_tables(seq)
    x2 = x.reshape(t, d)
    h = _rmsnorm(x2, wts["norm_mix"][0])
    b_slopes = _alibi_slopes(DIL_HEADS)
    d_table = _band_table(_alibi_slopes(SW_Q_HEADS), 1, SW_HALF_WINDOW, SW_KV_HEADS)
    for l in range(depth):
        p, *folded = _proj_mix(h, wts["w_mix"], l, batch, seq)
        gates = _proj(h, wts["w_gate"], l, sigmoid=True, tm=2048, name="proj_gate")
        ya = _natten(p, tables["natten"], l, batch, seq)
        obs, lss = [], []
        hg = DIL_HEADS_PER_GROUP
        p3 = p.reshape(batch, seq, P_WIDTH)
        for g, (win, dil) in enumerate(DIL_PATTERNS):
            if dil == 1:
                arr, cols = p3, (COL_B0Q, COL_B0K, COL_B0V)
            else:
                arr = folded[DILATED.index(dil)].reshape(batch * dil, seq // dil, BD_WIDTH)
                cols = (0, B_OUT, 2 * B_OUT)
            ob, ls = _banded(arr, w=win // (2 * dil), col_q=cols[0], col_k=cols[1], col_v=cols[2],
                             wq=B_OUT, wkv=B_OUT, kvmap=(0, 1),
                             table=_band_table(b_slopes[g * hg:(g + 1) * hg], dil, win // (2 * dil), 2),
                             sink=None, out_dtype=F32, has_lse=True, name="banded_d%d" % dil)
            obs.append(ob.reshape(batch, dil, seq // dil, B_OUT))
            lss.append(ls.reshape(batch, dil, seq // dil, B_OUT))
        qr, kr, ve, vo = _cprep(p, cos, sin, wts["q_gain_c"][l], wts["k_gain_c"][l], batch, seq)
        yc = _flash(qr, kr, ve, vo, batch, seq)
        (yd,) = _banded(p3, w=SW_HALF_WINDOW, col_q=COL_DQ, col_k=COL_DK, col_v=COL_DV, wq=D_QW, wkv=D_KVW,
                        kvmap=(0, 1), table=d_table, sink=wts["sink_d"][l], out_dtype=BF16, has_lse=False,
                        name="banded_sink")
        yd = yd.reshape(t, D_QW)
        x2, h = _merge(x2, ya, obs, lss, yc, yd, gates, wts["w_a"], wts["w_b"], wts["w_c"], wts["w_d"],
                       wts["w_out"], wts["norm_mlp"][l], l, seq)
        if l + 1 < depth:
            x2, h = _mlp(h, x2, wts["w_up"], wts["w_down"], wts["norm_mix"][l + 1], l, final=False)
        else:
            (y,) = _mlp(h, x2, wts["w_up"], wts["w_down"], wts["norm_final"], l, final=True)
    return y.reshape(batch, seq, d)


def kernel(x_prompt, x_sample, norm_mix, w_in, rel_bias_a, q_gain_c, k_gain_c, sink_d, w_branch_a, w_branch_b,
           w_branch_c, w_branch_d, w_out, norm_mlp, w_up, w_down, norm_final):
    w_mix, w_gate = _prep_in_weights(w_in)
    wts = dict(norm_mix=norm_mix, w_mix=w_mix, w_gate=w_gate, q_gain_c=q_gain_c, k_gain_c=k_gain_c, sink_d=sink_d,
               w_a=w_branch_a.astype(BF16), w_b=w_branch_b.astype(BF16), w_c=w_branch_c.astype(BF16),
               w_d=w_branch_d.astype(BF16), w_out=w_out.astype(BF16), norm_mlp=norm_mlp,
               w_up=w_up.astype(BF16), w_down=w_down.astype(BF16), norm_final=norm_final)
    tables = dict(natten=jax.vmap(_natten_bias_table)(rel_bias_a))
    return (_trunk(x_prompt, wts, tables), _trunk(x_sample, wts, tables))
```
